```python
import math
import jax, jax.numpy as jnp
from jax import lax
import numpy as np

D_MODEL = 1024
BATCH = 2
SEQ = 8192
DEPTH = 1

MEM_LEN = 256
EPS = 1e-6
D_FF = 2816
GM_WIDTH = 512
GM_GROUPS = 4
GM_CH = GM_WIDTH // GM_GROUPS
CHUNK = 128
NSA_WIDTH = D_MODEL - GM_WIDTH
HEAD_DIM = 64
NSA_HEADS = NSA_WIDTH // HEAD_DIM
NSA_KV = 2
NSA_REP = NSA_HEADS // NSA_KV
KV_W = NSA_KV * HEAD_DIM
CMP_BLOCK = 32
CMP_STRIDE = 16
CMP_HIDDEN = 256
SEL_BLOCK = 64
N_SELECT = 16
WINDOW = 512
Q_BLOCK = 128
N_BUCKETS = 32
MAX_DISTANCE = 128
XA_HEADS = 4
XA_HEAD_DIM = D_MODEL // XA_HEADS
IN_COLS = 2 * GM_WIDTH + NSA_WIDTH + 6 * KV_W + 3 * NSA_HEADS
NEG = -1e30
FORCE_SCORE = 1e4

kernel_name = 'hybrid_gmlp_nsa_macaron_block'


def rms_norm(x, g):
    xf = x.astype(jnp.float32)
    y = xf * lax.rsqrt(jnp.mean(xf * xf, axis=-1, keepdims=True) + EPS)
    return (y * g.astype(jnp.float32)).astype(x.dtype)


def layer_norm(x, g, b):
    xf = x.astype(jnp.float32)
    mu = jnp.mean(xf, axis=-1, keepdims=True)
    var = jnp.mean(jnp.square(xf - mu), axis=-1, keepdims=True)
    y = (xf - mu) * lax.rsqrt(var + EPS)
    return (y * g.astype(jnp.float32) + b.astype(jnp.float32)).astype(x.dtype)


def swiglu(h, wg, wu, wd):
    return (jax.nn.silu(h @ wg) * (h @ wu)) @ wd


def t5_bucket(dist):
    n = jnp.maximum(dist, 0)
    max_exact = N_BUCKETS // 2
    nf = jnp.maximum(n, 1).astype(jnp.float32)
    large = max_exact + (jnp.log(nf / max_exact) / math.log(MAX_DISTANCE / max_exact)
                         * (N_BUCKETS - max_exact)).astype(jnp.int32)
    large = jnp.minimum(large, N_BUCKETS - 1)
    return jnp.where(n < max_exact, n, large)


def masked_softmax(logits, mask):
    l = jnp.where(mask, logits, NEG)
    m = jnp.max(l, axis=-1, keepdims=True)
    e = jnp.where(mask, jnp.exp(l - m), 0.0)
    return e / jnp.maximum(jnp.sum(e, axis=-1, keepdims=True), 1e-30)


def gmlp_mix(u, v, ln_g, ln_b, ws, bs):
    B, S, _ = u.shape
    u = jax.nn.gelu(u).reshape(B, S // CHUNK, CHUNK, GM_GROUPS, GM_CH)
    v = jax.nn.gelu(v).reshape(B, S // CHUNK, CHUNK, GM_GROUPS, GM_CH)
    v = layer_norm(v, ln_g, ln_b)
    causal = jnp.tril(jnp.ones((CHUNK, CHUNK), dtype=bool))
    w = jnp.where(causal[None], ws, 0.0).astype(v.dtype)
    s = jnp.einsum('gpq,bnqgc->bnpgc', w, v) + bs.T.astype(v.dtype)[None, None, :, :, None]
    return (u * s).reshape(B, S, GM_WIDTH)


def compress(k, pe, w1, b1, w2):
    B, S, G, Dh = k.shape
    units = k.reshape(B, S // CMP_STRIDE, CMP_STRIDE, G, Dh)
    blk = jnp.concatenate([units[:, :-1], units[:, 1:]], axis=2)
    blk = blk + pe[None, None, :, None, :]
    nc = blk.shape[1]
    blk = blk.transpose(0, 1, 3, 2, 4).reshape(B, nc, G, CMP_BLOCK * Dh)
    return jax.nn.gelu(blk @ w1 + b1) @ w2


def nsa_attention(q, kc_raw, vc_raw, ks, vs, kw, vw, gates,
                  ck_pe, ck_w1, ck_b1, ck_w2, cv_pe, cv_w1, cv_b1, cv_w2, rel_bias):
    B, S, G, R, Dh = q.shape
    kc = compress(kc_raw, ck_pe, ck_w1, ck_b1, ck_w2)
    vc = compress(vc_raw, cv_pe, cv_w1, cv_b1, cv_w2)
    nc = kc.shape[1]
    ns = S // SEL_BLOCK
    n_sel = min(N_SELECT, ns)
    nq = S // Q_BLOCK
    scale = HEAD_DIM ** -0.5
    tab = rel_bias.astype(jnp.float32).reshape(N_BUCKETS, G, R)
    cmp_end = jnp.arange(nc) * CMP_STRIDE + CMP_BLOCK - 1
    u_c = CMP_BLOCK // CMP_STRIDE
    u_s = SEL_BLOCK // CMP_STRIDE
    ci = jnp.arange(nc)[:, None]
    sj = jnp.arange(ns)[None, :]
    overlap = jnp.clip(jnp.minimum(ci + u_c, u_s * (sj + 1)) - jnp.maximum(ci, u_s * sj), 0).astype(jnp.float32)
    ks_blk = ks.transpose(0, 2, 1, 3).reshape(B, G, ns, SEL_BLOCK, Dh)
    vs_blk = vs.transpose(0, 2, 1, 3).reshape(B, G, ns, SEL_BLOCK, Dh)
    kw_pad = jnp.pad(kw, ((0, 0), (WINDOW, 0), (0, 0), (0, 0)))
    vw_pad = jnp.pad(vw, ((0, 0), (WINDOW, 0), (0, 0), (0, 0)))
    q_blocks = q.reshape(B, nq, Q_BLOCK, G, R, Dh).transpose(1, 0, 2, 3, 4, 5)
    g_blocks = gates.reshape(B, nq, Q_BLOCK, G, R, 3).transpose(1, 0, 2, 3, 4, 5)
    b_ix = jnp.arange(B)[:, None, None, None]
    g_ix = jnp.arange(G)[None, :, None, None]
    g_ix5 = jnp.arange(G)[None, :, None, None, None]
    blk_start = jnp.arange(ns) * SEL_BLOCK
    j_ix = jnp.arange(ns)

    def one_block(args):
        qb, gb, i = args
        t = i * Q_BLOCK + jnp.arange(Q_BLOCK)
        lc = jnp.einsum('bqgrd,bcgd->bgrqc', qb, kc).astype(jnp.float32) * scale
        lc = lc + tab[t5_bucket(t[:, None] - cmp_end[None, :])].transpose(2, 3, 0, 1)
        pc = masked_softmax(lc, cmp_end[None, :] <= t[:, None])
        oc = jnp.einsum('bgrqc,bcgd->bqgrd', pc.astype(vc.dtype), vc)
        imp = jnp.einsum('bgrqc,cj->bgqj', pc, overlap)
        cur = t // SEL_BLOCK
        causal = blk_start[None, :] <= t[:, None]
        forced = (j_ix[None, :] == 0) | (j_ix[None, :] == cur[:, None]) | (j_ix[None, :] == cur[:, None] - 1)
        score = jnp.where(forced, FORCE_SCORE, jnp.where(causal, imp, -1.0))
        top_score, top_idx = lax.top_k(score, n_sel)
        blk_ok = top_score >= 0.0
        ksel = ks_blk[b_ix, g_ix, top_idx]
        vsel = vs_blk[b_ix, g_ix, top_idx]
        pos = top_idx[..., None] * SEL_BLOCK + jnp.arange(SEL_BLOCK)
        dist = t[None, None, :, None, None] - pos
        ls = jnp.einsum('bqgrd,bgqnkd->bgrqnk', qb, ksel).astype(jnp.float32) * scale
        ls = ls + tab[t5_bucket(dist), g_ix5].transpose(0, 1, 5, 2, 3, 4)
        ms = (blk_ok[..., None] & (dist >= 0))[:, :, None]
        L = n_sel * SEL_BLOCK
        ps = masked_softmax(ls.reshape(B, G, R, Q_BLOCK, L), ms.reshape(B, G, 1, Q_BLOCK, L))
        osl = jnp.einsum('bgrql,bgqld->bqgrd', ps.astype(vsel.dtype), vsel.reshape(B, G, Q_BLOCK, L, Dh))
        start = i * Q_BLOCK
        kwb = lax.dynamic_slice_in_dim(kw_pad, start, Q_BLOCK + WINDOW, axis=1)
        vwb = lax.dynamic_slice_in_dim(vw_pad, start, Q_BLOCK + WINDOW, axis=1)
        pos_w = start - WINDOW + jnp.arange(Q_BLOCK + WINDOW)
        dw = t[:, None] - pos_w[None, :]
        mw = (pos_w[None, :] >= 0) & (dw >= 0) & (dw < WINDOW)
        lw = jnp.einsum('bqgrd,bkgd->bgrqk', qb, kwb).astype(jnp.float32) * scale
        lw = lw + tab[t5_bucket(dw)].transpose(2, 3, 0, 1)
        pw = masked_softmax(lw, mw)
        ow = jnp.einsum('bgrqk,bkgd->bqgrd', pw.astype(vwb.dtype), vwb)
        g = jax.nn.sigmoid(gb.astype(jnp.float32))
        o = g[..., 0:1] * oc + g[..., 1:2] * osl + g[..., 2:3] * ow
        return o.reshape(B, Q_BLOCK, NSA_WIDTH).astype(qb.dtype)

    out = lax.map(one_block, (q_blocks, g_blocks, jnp.arange(nq)))
    return out.transpose(1, 0, 2, 3).reshape(B, S, NSA_WIDTH)


def cross_attention(h, mem, g_mem, wq, wkv, wo):
    B, S, _ = h.shape
    M = mem.shape[1]
    q = (h @ wq).reshape(B, S, XA_HEADS, XA_HEAD_DIM)
    k, v = jnp.split(rms_norm(mem, g_mem) @ wkv, 2, axis=-1)
    k = k.reshape(B, M, XA_HEADS, XA_HEAD_DIM)
    v = v.reshape(B, M, XA_HEADS, XA_HEAD_DIM)
    logits = jnp.einsum('bshd,bmhd->bhsm', q, k).astype(jnp.float32) * (XA_HEAD_DIM ** -0.5)
    p = jax.nn.softmax(logits, axis=-1)
    o = jnp.einsum('bhsm,bmhd->bshd', p.astype(v.dtype), v).reshape(B, S, D_MODEL)
    return o @ wo


def setup_inputs(seed: int = 0) -> dict:
    key = jax.random.key(seed)
    ks = iter(jax.random.split(key, 48))
    L = DEPTH

    def nrm(shape, scale):
        return jax.random.normal(next(ks), shape, jnp.float32) * scale

    def gain(shape):
        return 1.0 + nrm(shape, 0.05)

    return {
        'x': nrm((BATCH, SEQ, D_MODEL), 1.0),
        'mem': nrm((BATCH, MEM_LEN, D_MODEL), 1.0),
        'ffn1_pre': gain((L, D_MODEL)),
        'ffn1_post': gain((L, D_MODEL)),
        'ffn1_wg': nrm((L, D_MODEL, D_FF), D_MODEL ** -0.5),
        'ffn1_wu': nrm((L, D_MODEL, D_FF), D_MODEL ** -0.5),
        'ffn1_wd': nrm((L, D_FF, D_MODEL), D_FF ** -0.5),
        'mix_pre': gain((L, D_MODEL)),
        'mix_post': gain((L, D_MODEL)),
        'w_in': nrm((L, D_MODEL, IN_COLS), D_MODEL ** -0.5),
        'gm_ln_g': gain((L, GM_GROUPS, GM_CH)),
        'gm_ln_b': nrm((L, GM_GROUPS, GM_CH), 0.02),
        'gm_ws': nrm((L, GM_GROUPS, CHUNK, CHUNK), CHUNK ** -0.5),
        'gm_bs': 1.0 + nrm((L, GM_GROUPS, CHUNK), 0.1),
        'ck_pe': nrm((L, CMP_BLOCK, HEAD_DIM), 0.1),
        'ck_w1': nrm((L, CMP_BLOCK * HEAD_DIM, CMP_HIDDEN), (CMP_BLOCK * HEAD_DIM) ** -0.5),
        'ck_b1': nrm((L, CMP_HIDDEN), 0.02),
        'ck_w2': nrm((L, CMP_HIDDEN, HEAD_DIM), CMP_HIDDEN ** -0.5),
        'cv_pe': nrm((L, CMP_BLOCK, HEAD_DIM), 0.1),
        'cv_w1': nrm((L, CMP_BLOCK * HEAD_DIM, CMP_HIDDEN), (CMP_BLOCK * HEAD_DIM) ** -0.5),
        'cv_b1': nrm((L, CMP_HIDDEN), 0.02),
        'cv_w2': nrm((L, CMP_HIDDEN, HEAD_DIM), CMP_HIDDEN ** -0.5),
        'rel_bias': nrm((N_BUCKETS, NSA_HEADS), 0.5),
        'out_gain_a': gain((L, GM_WIDTH)),
        'out_gain_b': gain((L, NSA_WIDTH)),
        'w_out': nrm((L, D_MODEL, D_MODEL), D_MODEL ** -0.5),
        'xa_pre': gain((L, D_MODEL)),
        'xa_post': gain((L, D_MODEL)),
        'mem_norm': gain((L, D_MODEL)),
        'xa_wq': nrm((L, D_MODEL, D_MODEL), D_MODEL ** -0.5),
        'xa_wkv': nrm((L, D_MODEL, 2 * D_MODEL), D_MODEL ** -0.5),
        'xa_wo': nrm((L, D_MODEL, D_MODEL), D_MODEL ** -0.5),
        'ffn2_pre': gain((L, D_MODEL)),
        'ffn2_post': gain((L, D_MODEL)),
        'ffn2_wg': nrm((L, D_MODEL, D_FF), D_MODEL ** -0.5),
        'ffn2_wu': nrm((L, D_MODEL, D_FF), D_MODEL ** -0.5),
        'ffn2_wd': nrm((L, D_FF, D_MODEL), D_FF ** -0.5),
    }


def reference(x, mem, ffn1_pre, ffn1_post, ffn1_wg, ffn1_wu, ffn1_wd,
              mix_pre, mix_post, w_in, gm_ln_g, gm_ln_b, gm_ws, gm_bs,
              ck_pe, ck_w1, ck_b1, ck_w2, cv_pe, cv_w1, cv_b1, cv_w2, rel_bias,
              out_gain_a, out_gain_b, w_out, xa_pre, xa_post, mem_norm, xa_wq, xa_wkv, xa_wo,
              ffn2_pre, ffn2_post, ffn2_wg, ffn2_wu, ffn2_wd):
    B, S, _ = x.shape
    widths = [GM_WIDTH, GM_WIDTH, NSA_WIDTH] + [KV_W] * 6 + [3 * NSA_HEADS]
    offsets = np.cumsum(widths)[:-1].tolist()
    for l in range(DEPTH):
        h = rms_norm(x, ffn1_pre[l])
        x = x + 0.5 * rms_norm(swiglu(h, ffn1_wg[l], ffn1_wu[l], ffn1_wd[l]), ffn1_post[l])
        h = rms_norm(x, mix_pre[l])
        z = h @ w_in[l]
        u, v, q, kc, vc, ksl, vsl, kwn, vwn, gt = jnp.split(z, offsets, axis=-1)
        y_a = gmlp_mix(u, v, gm_ln_g[l], gm_ln_b[l], gm_ws[l], gm_bs[l])
        kv_shape = (B, S, NSA_KV, HEAD_DIM)
        y_b = nsa_attention(q.reshape(B, S, NSA_KV, NSA_REP, HEAD_DIM),
                            kc.reshape(kv_shape), vc.reshape(kv_shape),
                            ksl.reshape(kv_shape), vsl.reshape(kv_shape),
                            kwn.reshape(kv_shape), vwn.reshape(kv_shape), gt,
                            ck_pe[l], ck_w1[l], ck_b1[l], ck_w2[l],
                            cv_pe[l], cv_w1[l], cv_b1[l], cv_w2[l], rel_bias)
        y = jnp.concatenate([rms_norm(y_a, out_gain_a[l]), rms_norm(y_b, out_gain_b[l])], axis=-1) @ w_out[l]
        x = x + rms_norm(y, mix_post[l])
        h = rms_norm(x, xa_pre[l])
        x = x + rms_norm(cross_attention(h, mem, mem_norm[l], xa_wq[l], xa_wkv[l], xa_wo[l]), xa_post[l])
        h = rms_norm(x, ffn2_pre[l])
        x = x + 0.5 * rms_norm(swiglu(h, ffn2_wg[l], ffn2_wu[l], ffn2_wd[l]), ffn2_post[l])
    return x
```

```python
import functools
import math

import jax
import jax.numpy as jnp
from jax import lax
from jax.experimental import pallas as pl
from jax.experimental.pallas import tpu as pltpu

F32 = jnp.float32
BF16 = jnp.bfloat16

D_MODEL = 1024
MEM_LEN = 256
EPS = 1e-6
D_FF = 2816
GM_WIDTH = 512
GM_GROUPS = 4
GM_CH = GM_WIDTH // GM_GROUPS
CHUNK = 128
NSA_WIDTH = D_MODEL - GM_WIDTH
HEAD_DIM = 64
NSA_HEADS = NSA_WIDTH // HEAD_DIM
NSA_KV = 2
NSA_REP = NSA_HEADS // NSA_KV
KV_W = NSA_KV * HEAD_DIM
CMP_BLOCK = 32
CMP_STRIDE = 16
CMP_HIDDEN = 256
SEL_BLOCK = 64
N_SELECT = 16
WINDOW = 512
Q_BLOCK = 128
N_BUCKETS = 32
MAX_DISTANCE = 128
XA_HEADS = 4
XA_HEAD_DIM = D_MODEL // XA_HEADS
FORCE_SCORE = 1e4

LANE = 128
TOKEN_TILE = 512
FF_CHUNK = 256
KV_PAD = 512
CMP_PAD = 128
SEL_TAIL = 5 * Q_BLOCK
SEL_FAR_TILE = 512
MASK_BF16 = -(2.0 ** 100)
MASK_F32 = -1e30
MAX_CLAMP = -(2.0 ** 99)
VMEM_LIMIT = 56 * 1024 * 1024

_NT = (((1,), (1,)), ((), ()))


def _rms(x, g):
    ms = jnp.mean(x * x, axis=-1, keepdims=True)
    return x * lax.rsqrt(ms + EPS) * g


def _dot(a, b):
    return jnp.dot(a, b, preferred_element_type=F32)


def _dot_nt(a, b):
    return lax.dot_general(a, b, _NT, preferred_element_type=F32)


def _params(sem, vmem=VMEM_LIMIT):
    return pltpu.CompilerParams(dimension_semantics=sem, vmem_limit_bytes=vmem)


def _const_spec(shape):
    nd = len(shape)
    return pl.BlockSpec(shape, lambda *_: (0,) * nd)


def _ffn_kernel(x_ref, pre_ref, wg_ref, wu_ref, wd_ref, post_ref, o_ref, acc_ref):
    x = x_ref[...]
    h = _rms(x, pre_ref[...]).astype(BF16)
    for c in range(D_FF // FF_CHUNK):
        sl = slice(c * FF_CHUNK, (c + 1) * FF_CHUNK)
        g = _dot(h, wg_ref[:, sl])
        u = _dot(h, wu_ref[:, sl])
        a = (jax.nn.silu(g) * u).astype(BF16)
        d = _dot(a, wd_ref[sl, :])
        if c == 0:
            acc_ref[...] = d
        else:
            acc_ref[...] += d
    o_ref[...] = x + 0.5 * _rms(acc_ref[...], post_ref[...])


def _ffn(x2d, pre, wg, wu, wd, post):
    n = x2d.shape[0]
    tm = TOKEN_TILE
    row = pl.BlockSpec((tm, D_MODEL), lambda t: (t, 0))
    return pl.pallas_call(
        _ffn_kernel,
        grid=(n // tm,),
        in_specs=[row, _const_spec((1, D_MODEL)), _const_spec((D_MODEL, D_FF)),
                  _const_spec((D_MODEL, D_FF)), _const_spec((D_FF, D_MODEL)),
                  _const_spec((1, D_MODEL))],
        out_specs=row,
        out_shape=jax.ShapeDtypeStruct((n, D_MODEL), F32),
        scratch_shapes=[pltpu.VMEM((tm, D_MODEL), F32)],
        compiler_params=_params(("arbitrary",)),
        name="ffn",
    )(x2d, pre.reshape(1, -1), wg.astype(BF16), wu.astype(BF16), wd.astype(BF16),
      post.reshape(1, -1))


def _mixproj_kernel(x_ref, pre_ref, wuv_ref, wq_ref, wc_ref, wkv_ref, wgt_ref,
                    lng_ref, lnb_ref, ws_ref, bs_ref, ga_ref,
                    ya_ref, q_ref, cmp_ref, ksel_ref, vsel_ref, kwin_ref, vwin_ref, gt_ref):
    t = pl.program_id(1)
    tm = x_ref.shape[0]
    lane = lax.broadcasted_iota(jnp.int32, (tm, LANE), 1)
    flag_col = jnp.where(lane == HEAD_DIM, MASK_BF16, 0.0).astype(BF16)

    @pl.when(t == 0)
    def _pad_rows():
        for g in range(NSA_KV):
            ksel_ref[g, :, 0:LANE] = flag_col
            ksel_ref[g, :, LANE:2 * LANE] = jnp.zeros((tm, LANE), BF16)
            kwin_ref[g] = flag_col
            vsel_ref[g] = jnp.zeros((tm, LANE), BF16)
            vwin_ref[g] = jnp.zeros((tm, LANE), BF16)

    @pl.when(t > 0)
    def _tile():
        h = _rms(x_ref[...], pre_ref[...]).astype(BF16)
        u = jax.nn.gelu(_dot(h, wuv_ref[:, 0:GM_WIDTH]))
        v = jax.nn.gelu(_dot(h, wuv_ref[:, GM_WIDTH:2 * GM_WIDTH]))
        r_i = lax.broadcasted_iota(jnp.int32, (CHUNK, CHUNK), 0)
        c_i = lax.broadcasted_iota(jnp.int32, (CHUNK, CHUNK), 1)
        for g in range(GM_GROUPS):
            gs = slice(g * GM_CH, (g + 1) * GM_CH)
            vg = v[:, gs]
            mu = jnp.mean(vg, axis=-1, keepdims=True)
            var = jnp.mean(jnp.square(vg - mu), axis=-1, keepdims=True)
            vn = ((vg - mu) * lax.rsqrt(var + EPS) * lng_ref[:, gs] + lnb_ref[:, gs]).astype(BF16)
            w = jnp.where(c_i <= r_i, ws_ref[g], 0.0).astype(BF16)
            for n in range(tm // CHUNK):
                rs = slice(n * CHUNK, (n + 1) * CHUNK)
                s = _dot(w, vn[rs]) + bs_ref[:, g:g + 1]
                ya_ref[rs, gs] = u[rs, gs] * s
        ya_ref[...] = _rms(ya_ref[...], ga_ref[...])
        one_col = jnp.where(lane == HEAD_DIM, 1.0, 0.0)
        zq = _dot(h, wq_ref[...])
        for hd in range(NSA_HEADS):
            q_ref[hd] = (zq[:, hd * LANE:(hd + 1) * LANE] + one_col).astype(BF16)
        zc = _dot(h, wc_ref[...])
        cmp_ref[0] = zc[:, 0:KV_W]
        cmp_ref[1] = zc[:, KV_W:2 * KV_W]
        zkv = _dot(h, wkv_ref[...])
        ones_hi = jnp.where(lane >= HEAD_DIM, 1.0, 0.0)
        pos_blk = ((t - 1) * tm + lax.broadcasted_iota(jnp.int32, (tm, LANE), 0)) // SEL_BLOCK
        blk_onehot = jnp.where(lane == pos_blk, MASK_BF16, 0.0).astype(BF16)
        for g in range(NSA_KV):
            ksel_ref[g, :, 0:LANE] = zkv[:, g * LANE:(g + 1) * LANE].astype(BF16)
            ksel_ref[g, :, LANE:2 * LANE] = blk_onehot
            vsel_ref[g] = (zkv[:, (2 + g) * LANE:(3 + g) * LANE] + ones_hi).astype(BF16)
            kwin_ref[g] = zkv[:, (4 + g) * LANE:(5 + g) * LANE].astype(BF16)
            vwin_ref[g] = (zkv[:, (6 + g) * LANE:(7 + g) * LANE] + ones_hi).astype(BF16)
        gt_ref[...] = jax.nn.sigmoid(_dot(h, wgt_ref[...]))


def _mixproj(x2d, batch, seq, mix_pre, w_in, ln_g, ln_b, ws, bs, gain_a):
    tm = TOKEN_TILE
    nt = seq // tm
    assert seq % tm == 0 and KV_PAD == tm
    scale = HEAD_DIM ** -0.5
    o_q = 2 * GM_WIDTH
    o_kv = o_q + NSA_WIDTH
    w_uv = w_in[:, :o_q]
    w_q = (w_in[:, o_q:o_kv] * scale).reshape(D_MODEL, NSA_HEADS, HEAD_DIM)
    w_q = jnp.pad(w_q, ((0, 0), (0, 0), (0, LANE - HEAD_DIM))).reshape(D_MODEL, NSA_HEADS * LANE)
    w_c = w_in[:, o_kv:o_kv + 2 * KV_W]
    w_kv = w_in[:, o_kv + 2 * KV_W:o_kv + 6 * KV_W].reshape(D_MODEL, 4 * NSA_KV, HEAD_DIM)
    w_kv = jnp.pad(w_kv, ((0, 0), (0, 0), (0, LANE - HEAD_DIM))).reshape(D_MODEL, 4 * NSA_KV * LANE)
    w_g = w_in[:, o_kv + 6 * KV_W:].reshape(D_MODEL, NSA_KV, NSA_REP, 3).transpose(0, 1, 3, 2)
    w_g = jnp.pad(w_g.reshape(D_MODEL, NSA_KV, 3 * NSA_REP),
                  ((0, 0), (0, 0), (0, LANE - 3 * NSA_REP))).reshape(D_MODEL, NSA_KV * LANE)

    def tile(b, t):
        return b * nt + jnp.maximum(t - 1, 0)

    n = batch * seq
    sp = seq + KV_PAD
    out_shape = [
        jax.ShapeDtypeStruct((n, GM_WIDTH), F32),
        jax.ShapeDtypeStruct((batch, NSA_HEADS, seq, LANE), BF16),
        jax.ShapeDtypeStruct((2, n, KV_W), F32),
        jax.ShapeDtypeStruct((batch, NSA_KV, sp, 2 * LANE), BF16),
        jax.ShapeDtypeStruct((batch, NSA_KV, sp, LANE), BF16),
        jax.ShapeDtypeStruct((batch, NSA_KV, sp, LANE), BF16),
        jax.ShapeDtypeStruct((batch, NSA_KV, sp, LANE), BF16),
        jax.ShapeDtypeStruct((n, NSA_KV * LANE), F32),
    ]
    padded = lambda w: pl.BlockSpec((None, NSA_KV, tm, w), lambda b, t: (b, 0, t, 0))
    out_specs = [
        pl.BlockSpec((tm, GM_WIDTH), lambda b, t: (tile(b, t), 0)),
        pl.BlockSpec((None, NSA_HEADS, tm, LANE), lambda b, t: (b, 0, jnp.maximum(t - 1, 0), 0)),
        pl.BlockSpec((2, tm, KV_W), lambda b, t: (0, tile(b, t), 0)),
        padded(2 * LANE), padded(LANE), padded(LANE), padded(LANE),
        pl.BlockSpec((tm, NSA_KV * LANE), lambda b, t: (tile(b, t), 0)),
    ]
    in_specs = [
        pl.BlockSpec((tm, D_MODEL), lambda b, t: (tile(b, t), 0)),
        _const_spec((1, D_MODEL)),
        _const_spec(w_uv.shape), _const_spec(w_q.shape), _const_spec(w_c.shape),
        _const_spec(w_kv.shape), _const_spec(w_g.shape),
        _const_spec((1, GM_WIDTH)), _const_spec((1, GM_WIDTH)),
        _const_spec((GM_GROUPS, CHUNK, CHUNK)), _const_spec((CHUNK, GM_GROUPS)),
        _const_spec((1, GM_WIDTH)),
    ]
    return pl.pallas_call(
        _mixproj_kernel,
        grid=(batch, nt + 1),
        in_specs=in_specs,
        out_specs=out_specs,
        out_shape=out_shape,
        compiler_params=_params(("arbitrary", "arbitrary")),
        name="mixproj",
    )(x2d, mix_pre.reshape(1, -1), w_uv.astype(BF16), w_q.astype(BF16), w_c.astype(BF16),
      w_kv.astype(BF16), w_g.astype(BF16), ln_g.reshape(1, -1), ln_b.reshape(1, -1),
      ws, bs.T, gain_a.reshape(1, -1))


def _compress_kernel(raw_ref, pea_ref, peb_ref, wa_ref, wb_ref, b1_ref, w2_ref, o_ref):
    is_v = pl.program_id(0) == 1
    nu = raw_ref.shape[0]
    u = raw_ref[...]
    a = _dot((u + pea_ref[...]).astype(BF16), wa_ref[...])
    b = _dot((u + peb_ref[...]).astype(BF16), wb_ref[...])
    hid = jax.nn.gelu(a + pltpu.roll(b, nu - 1, 0) + b1_ref[...])
    out = _dot(hid.astype(BF16), w2_ref[...])
    lane = lax.broadcasted_iota(jnp.int32, (nu, LANE), 1)
    ci = lax.broadcasted_iota(jnp.int32, (nu, LANE), 0)
    u_c = CMP_BLOCK // CMP_STRIDE
    u_s = SEL_BLOCK // CMP_STRIDE
    overlap = jnp.maximum(jnp.minimum(ci + u_c, u_s * (lane + 1)) - jnp.maximum(ci, u_s * lane), 0)
    extra = jnp.where(is_v, overlap.astype(F32), 0.0)
    ones_hi = jnp.where(jnp.logical_and(is_v, lane >= HEAD_DIM), 1.0, 0.0)
    lane_p = lax.broadcasted_iota(jnp.int32, (CMP_PAD, LANE), 1)
    pad_lo = jnp.where(jnp.logical_and(jnp.logical_not(is_v), lane_p == HEAD_DIM), MASK_BF16, 0.0)
    for g in range(NSA_KV):
        o_ref[g, 0:CMP_PAD, 0:LANE] = pad_lo
        o_ref[g, 0:CMP_PAD, LANE:2 * LANE] = jnp.zeros((CMP_PAD, LANE), F32)
        o_ref[g, CMP_PAD:CMP_PAD + nu, 0:LANE] = out[:, g * LANE:(g + 1) * LANE] + ones_hi
        o_ref[g, CMP_PAD:CMP_PAD + nu, LANE:2 * LANE] = extra


def _compress(cmp_raw, batch, seq, pe, w1, b1, w2):
    nu = seq // CMP_STRIDE
    half = CMP_STRIDE * HEAD_DIM
    raw = cmp_raw.reshape(2, batch, nu, CMP_STRIDE * KV_W)

    def spread(w):
        w = w.reshape(2, CMP_STRIDE, 1, HEAD_DIM, 1, CMP_HIDDEN)
        eye = jnp.eye(NSA_KV, dtype=w.dtype).reshape(1, 1, NSA_KV, 1, NSA_KV, 1)
        return (w * eye).reshape(2, CMP_STRIDE * KV_W, NSA_KV * CMP_HIDDEN)

    wa = spread(w1[:, :half])
    wb = spread(w1[:, half:])
    pe_u = jnp.broadcast_to(pe.reshape(2, 2, CMP_STRIDE, 1, HEAD_DIM),
                            (2, 2, CMP_STRIDE, NSA_KV, HEAD_DIM)).reshape(2, 2, 1, CMP_STRIDE * KV_W)
    b1t = jnp.tile(b1, (1, NSA_KV)).reshape(2, 1, NSA_KV * CMP_HIDDEN)
    w2p = jnp.pad(w2, ((0, 0), (0, 0), (0, LANE - HEAD_DIM)))
    eye = jnp.eye(NSA_KV, dtype=w2.dtype)
    w2b = (eye[None, :, None, :, None] * w2p[:, None, :, None, :]).reshape(
        2, NSA_KV * CMP_HIDDEN, NSA_KV * LANE)
    rows = CMP_PAD + nu
    kd = CMP_STRIDE * KV_W
    return pl.pallas_call(
        _compress_kernel,
        grid=(2, batch),
        in_specs=[
            pl.BlockSpec((None, None, nu, kd), lambda k, b: (k, b, 0, 0)),
            pl.BlockSpec((None, None, 1, kd), lambda k, b: (k, 0, 0, 0)),
            pl.BlockSpec((None, None, 1, kd), lambda k, b: (k, 1, 0, 0)),
            pl.BlockSpec((None, kd, NSA_KV * CMP_HIDDEN), lambda k, b: (k, 0, 0)),
            pl.BlockSpec((None, kd, NSA_KV * CMP_HIDDEN), lambda k, b: (k, 0, 0)),
            pl.BlockSpec((None, 1, NSA_KV * CMP_HIDDEN), lambda k, b: (k, 0, 0)),
            pl.BlockSpec((None, NSA_KV * CMP_HIDDEN, NSA_KV * LANE), lambda k, b: (k, 0, 0)),
        ],
        out_specs=pl.BlockSpec((None, None, NSA_KV, rows, 2 * LANE), lambda k, b: (k, b, 0, 0, 0)),
        out_shape=jax.ShapeDtypeStruct((2, batch, NSA_KV, rows, 2 * LANE), F32),
        compiler_params=_params(("arbitrary", "arbitrary")),
        name="compress",
    )(raw, pe_u, pe_u, wa.astype(BF16), wb.astype(BF16), b1t, w2b.astype(BF16))


def _t5_bucket(dist):
    n = jnp.maximum(dist, 0)
    max_exact = N_BUCKETS // 2
    nf = jnp.maximum(n, 1).astype(F32)
    large = max_exact + (jnp.log(nf * (1.0 / max_exact)) / math.log(MAX_DISTANCE / max_exact)
                         * (N_BUCKETS - max_exact)).astype(jnp.int32)
    large = jnp.minimum(large, N_BUCKETS - 1)
    return jnp.where(n < max_exact, n, large)


def _tables_kernel(rb_ref, tb_ref, tc_ref, wm_ref):
    def fill(out_ref, dist):
        bk = _t5_bucket(dist)
        for h in range(NSA_HEADS):
            far = rb_ref[N_BUCKETS - 1, h]
            val = jnp.zeros(dist.shape, F32)
            for b in range(N_BUCKETS - 1):
                val = jnp.where(bk == b, rb_ref[b, h] - far, val)
            out_ref[h] = jnp.where(dist >= 0, val, MASK_F32)

    tq = lax.broadcasted_iota(jnp.int32, (Q_BLOCK, 2 * Q_BLOCK), 0)
    j = lax.broadcasted_iota(jnp.int32, (Q_BLOCK, 2 * Q_BLOCK), 1)
    fill(tb_ref, tq + Q_BLOCK - j)
    tq = lax.broadcasted_iota(jnp.int32, (Q_BLOCK, LANE), 0)
    c = lax.broadcasted_iota(jnp.int32, (Q_BLOCK, LANE), 1)
    cmp_first = Q_BLOCK // CMP_STRIDE - LANE
    fill(tc_ref, tq - CMP_STRIDE * (cmp_first + c) - (CMP_BLOCK - 1))
    wm_ref[...] = jnp.where(c > tq, 0.0, MASK_F32)


def _bias_tables(rel_bias):
    return pl.pallas_call(
        _tables_kernel,
        in_specs=[pl.BlockSpec(memory_space=pltpu.SMEM)],
        out_shape=[jax.ShapeDtypeStruct((NSA_HEADS, Q_BLOCK, 2 * Q_BLOCK), F32),
                   jax.ShapeDtypeStruct((NSA_HEADS, Q_BLOCK, LANE), F32),
                   jax.ShapeDtypeStruct((Q_BLOCK, LANE), F32)],
        name="bias_tables",
    )(rel_bias)


def _nsa_kernel(q_ref, gt_ref, kc_ref, vc_ref, ks_ref, vs_ref, kw_ref, vw_ref,
                tb_ref, tc_ref, wm_ref, o_ref, qsel_ref, m_ref, acc_ref, *, n_sel):
    i = pl.program_id(2)
    rows = NSA_REP * Q_BLOCK
    qa = q_ref[...].reshape(rows, LANE)

    def online_step(qx, k_tile, v_tile, width):
        s = _dot_nt(qx, k_tile)
        m_old = m_ref[...]
        m_new = jnp.maximum(m_old, jnp.max(s, axis=1, keepdims=True))
        p = jnp.exp(s - m_new)
        acc_ref[:, 0:width] = (jnp.exp(m_old - m_new) * acc_ref[:, 0:width]
                               + _dot(p.astype(BF16), v_tile))
        m_ref[...] = m_new

    near0 = pl.multiple_of(8 * i + 8, 8)
    s = _dot_nt(qa, kc_ref[pl.ds(near0, LANE), :].astype(BF16)) + tc_ref[...].reshape(rows, LANE)
    m = jnp.maximum(jnp.max(s, axis=1, keepdims=True), MAX_CLAMP)
    p = jnp.exp(s - m)
    m_ref[...] = m
    acc_ref[...] = _dot(p.astype(BF16), vc_ref[pl.ds(near0, LANE), :].astype(BF16))
    n_far_c = (jnp.maximum(8 * i - (LANE - 8), 0) + LANE - 1) // LANE

    def cmp_body(k, carry):
        st = pl.multiple_of(near0 - LANE * (k + 1), 8)
        online_step(qa, kc_ref[pl.ds(st, LANE), :].astype(BF16),
                    vc_ref[pl.ds(st, LANE), :].astype(BF16), 2 * LANE)
        return carry

    lax.fori_loop(0, n_far_c, cmp_body, 0)
    acc = acc_ref[...]
    inv = 1.0 / jnp.maximum(acc[:, HEAD_DIM:HEAD_DIM + 1], 1e-30)
    oc = acc[:, 0:LANE] * inv
    impu = acc[:, LANE:2 * LANE] * inv
    imp = impu[0:Q_BLOCK]
    for r in range(1, NSA_REP):
        imp = imp + impu[r * Q_BLOCK:(r + 1) * Q_BLOCK]

    imp_t = imp.T
    jb = lax.broadcasted_iota(jnp.int32, (LANE, Q_BLOCK), 0)
    t = i * Q_BLOCK + lax.broadcasted_iota(jnp.int32, (LANE, Q_BLOCK), 1)
    cur = t // SEL_BLOCK
    forced = (jb == 0) | (jb == cur) | (jb == cur - 1)
    score = jnp.where(forced, FORCE_SCORE, jnp.where(jb * SEL_BLOCK <= t, imp_t, -1.0))
    notsel = jnp.ones((LANE, Q_BLOCK), F32)
    for _ in range(n_sel):
        mx = jnp.max(score, axis=0, keepdims=True)
        first = jnp.min(jnp.where(score == mx, jb, LANE), axis=0, keepdims=True)
        hit = jb == first
        notsel = jnp.where(hit, 0.0, notsel)
        score = jnp.where(hit, -2.0, score)
    notsel_q = notsel.T.astype(BF16)
    qsel_ref[:, 0:LANE] = qa
    for r in range(NSA_REP):
        qsel_ref[r * Q_BLOCK:(r + 1) * Q_BLOCK, LANE:2 * LANE] = notsel_q
    qs = qsel_ref[...]

    tail0 = pl.multiple_of(Q_BLOCK * i, Q_BLOCK)
    split = SEL_TAIL - 2 * Q_BLOCK
    tb = tb_ref[...].reshape(rows, 2 * Q_BLOCK)
    sa = _dot_nt(qs, ks_ref[pl.ds(tail0, split), :])
    sb = _dot_nt(qs, ks_ref[pl.ds(tail0 + split, 2 * Q_BLOCK), :]) + tb
    m = jnp.maximum(jnp.max(sa, axis=1, keepdims=True), jnp.max(sb, axis=1, keepdims=True))
    m_ref[...] = m
    acc_ref[:, 0:LANE] = (
        _dot(jnp.exp(sa - m).astype(BF16), vs_ref[pl.ds(tail0, split), :])
        + _dot(jnp.exp(sb - m).astype(BF16), vs_ref[pl.ds(tail0 + split, 2 * Q_BLOCK), :]))
    n_far_s = (jnp.maximum(Q_BLOCK * i - KV_PAD, 0) + SEL_FAR_TILE - 1) // SEL_FAR_TILE

    def sel_body(k, carry):
        st = pl.multiple_of(tail0 - SEL_FAR_TILE * (k + 1), Q_BLOCK)
        online_step(qs, ks_ref[pl.ds(st, SEL_FAR_TILE), :], vs_ref[pl.ds(st, SEL_FAR_TILE), :], LANE)
        return carry

    lax.fori_loop(0, n_far_s, sel_body, 0)
    acc = acc_ref[:, 0:LANE]
    osl = acc * (1.0 / acc[:, HEAD_DIM:HEAD_DIM + 1])

    s0 = _dot_nt(qa, kw_ref[pl.ds(tail0, Q_BLOCK), :]) + jnp.tile(wm_ref[...], (NSA_REP, 1))
    s1 = _dot_nt(qa, kw_ref[pl.ds(tail0 + Q_BLOCK, 2 * Q_BLOCK), :])
    s2 = _dot_nt(qa, kw_ref[pl.ds(tail0 + split, 2 * Q_BLOCK), :]) + tb
    m = jnp.maximum(jnp.maximum(jnp.max(s0, axis=1, keepdims=True), jnp.max(s1, axis=1, keepdims=True)),
                    jnp.max(s2, axis=1, keepdims=True))
    acc = (_dot(jnp.exp(s0 - m).astype(BF16), vw_ref[pl.ds(tail0, Q_BLOCK), :])
           + _dot(jnp.exp(s1 - m).astype(BF16), vw_ref[pl.ds(tail0 + Q_BLOCK, 2 * Q_BLOCK), :])
           + _dot(jnp.exp(s2 - m).astype(BF16), vw_ref[pl.ds(tail0 + split, 2 * Q_BLOCK), :]))
    ow = acc * (1.0 / acc[:, HEAD_DIM:HEAD_DIM + 1])

    gts = gt_ref[...]
    lane = lax.broadcasted_iota(jnp.int32, (Q_BLOCK, LANE), 1)
    heads = []
    for r in range(NSA_REP):
        rs = slice(r * Q_BLOCK, (r + 1) * Q_BLOCK)
        heads.append(gts[:, r:r + 1] * oc[rs]
                     + gts[:, NSA_REP + r:NSA_REP + r + 1] * osl[rs]
                     + gts[:, 2 * NSA_REP + r:2 * NSA_REP + r + 1] * ow[rs])
    for pr in range(NSA_REP // 2):
        o_ref[:, pr * LANE:(pr + 1) * LANE] = jnp.where(
            lane < HEAD_DIM, heads[2 * pr], pltpu.roll(heads[2 * pr + 1], HEAD_DIM, 1))


def _nsa(q_aug, gates, cmp_aug, ksel, vsel, kwin, vwin, tb, tc, wm, batch, seq):
    nq = seq // Q_BLOCK
    sp = seq + KV_PAD
    crow = cmp_aug.shape[3]
    n_sel = min(N_SELECT, seq // SEL_BLOCK)
    assert seq // SEL_BLOCK <= LANE and SEL_TAIL - Q_BLOCK == KV_PAD
    kv_spec = lambda w: pl.BlockSpec((None, None, sp, w), lambda b, g, i: (b, g, 0, 0))
    in_specs = [
        pl.BlockSpec((None, NSA_REP, Q_BLOCK, LANE), lambda b, g, i: (b, g, i, 0)),
        pl.BlockSpec((Q_BLOCK, LANE), lambda b, g, i: (b * nq + i, g)),
        pl.BlockSpec((None, None, None, crow, LANE), lambda b, g, i: (0, b, g, 0, 0)),
        pl.BlockSpec((None, None, None, crow, 2 * LANE), lambda b, g, i: (1, b, g, 0, 0)),
        kv_spec(2 * LANE), kv_spec(LANE), kv_spec(LANE), kv_spec(LANE),
        pl.BlockSpec((NSA_REP, Q_BLOCK, 2 * Q_BLOCK), lambda b, g, i: (g, 0, 0)),
        pl.BlockSpec((NSA_REP, Q_BLOCK, LANE), lambda b, g, i: (g, 0, 0)),
        pl.BlockSpec((Q_BLOCK, LANE), lambda b, g, i: (0, 0)),
    ]
    rows = NSA_REP * Q_BLOCK
    return pl.pallas_call(
        functools.partial(_nsa_kernel, n_sel=n_sel),
        grid=(batch, NSA_KV, nq),
        in_specs=in_specs,
        out_specs=pl.BlockSpec((Q_BLOCK, NSA_REP * HEAD_DIM), lambda b, g, i: (b * nq + i, g)),
        out_shape=jax.ShapeDtypeStruct((batch * seq, NSA_WIDTH), F32),
        scratch_shapes=[pltpu.VMEM((rows, 2 * LANE), BF16),
                        pltpu.VMEM((rows, 1), F32),
                        pltpu.VMEM((rows, 2 * LANE), F32)],
        compiler_params=_params(("arbitrary", "arbitrary", "arbitrary")),
        name="nsa",
    )(q_aug, gates, cmp_aug, cmp_aug, ksel, vsel, kwin, vwin, tb, tc, wm)


def _memkv_kernel(mem_ref, g_ref, wk_ref, wv_ref, k_ref, v_ref):
    h = _rms(mem_ref[...], g_ref[...]).astype(BF16)
    k_ref[...] = _dot(h, wk_ref[...]).astype(BF16)
    v_ref[...] = _dot(h, wv_ref[...]).astype(BF16)


def _mem_kv(mem, g_mem, wkv):
    batch, m, _ = mem.shape
    blk = pl.BlockSpec((None, m, D_MODEL), lambda b: (b, 0, 0))
    return pl.pallas_call(
        _memkv_kernel,
        grid=(batch,),
        in_specs=[blk, _const_spec((1, D_MODEL)), _const_spec((D_MODEL, D_MODEL)),
                  _const_spec((D_MODEL, D_MODEL))],
        out_specs=[blk, blk],
        out_shape=[jax.ShapeDtypeStruct((batch, m, D_MODEL), BF16)] * 2,
        compiler_params=_params(("arbitrary",)),
        name="mem_kv",
    )(mem, g_mem.reshape(1, -1), wkv[:, :D_MODEL].astype(BF16), wkv[:, D_MODEL:].astype(BF16))


def _mixout_kernel(x_ref, ya_ref, yb_ref, gb_ref, wo_ref, post_ref, xpre_ref, wq_ref,
                   km_ref, vm_ref, wxo_ref, xpost_ref, o_ref):
    yb = _rms(yb_ref[...], gb_ref[...]).astype(BF16)
    y = _dot(ya_ref[...].astype(BF16), wo_ref[0:GM_WIDTH, :]) + _dot(yb, wo_ref[GM_WIDTH:, :])
    x1 = x_ref[...] + _rms(y, post_ref[...])
    h = _rms(x1, xpre_ref[...]).astype(BF16)
    q = _dot(h, wq_ref[...]).astype(BF16)
    outs = []
    for hd in range(XA_HEADS):
        hs = slice(hd * XA_HEAD_DIM, (hd + 1) * XA_HEAD_DIM)
        s = _dot_nt(q[:, hs], km_ref[:, hs])
        e = jnp.exp(s - jnp.max(s, axis=1, keepdims=True))
        o = _dot(e.astype(BF16), vm_ref[:, hs]) / jnp.sum(e, axis=1, keepdims=True)
        outs.append(o.astype(BF16))
    o = _dot(jnp.concatenate(outs, axis=1), wxo_ref[...])
    o_ref[...] = x1 + _rms(o, xpost_ref[...])


def _mix_out(x2d, ya, yb, batch, seq, gain_b, w_out, mix_post, xa_pre, wq, kmem, vmem, wo, xa_post):
    tm = TOKEN_TILE
    nt = seq // tm
    row = lambda w: pl.BlockSpec((tm, w), lambda b, t: (b * nt + t, 0))
    mem_spec = pl.BlockSpec((None, kmem.shape[1], D_MODEL), lambda b, t: (b, 0, 0))
    sq = _const_spec((D_MODEL, D_MODEL))
    vec = _const_spec((1, D_MODEL))
    return pl.pallas_call(
        _mixout_kernel,
        grid=(batch, nt),
        in_specs=[row(D_MODEL), row(GM_WIDTH), row(NSA_WIDTH), _const_spec((1, NSA_WIDTH)), sq, vec,
                  vec, sq, mem_spec, mem_spec, sq, vec],
        out_specs=row(D_MODEL),
        out_shape=jax.ShapeDtypeStruct(x2d.shape, F32),
        compiler_params=_params(("arbitrary", "arbitrary")),
        name="mix_out",
    )(x2d, ya, yb, gain_b.reshape(1, -1), w_out.astype(BF16), mix_post.reshape(1, -1),
      xa_pre.reshape(1, -1), (wq * XA_HEAD_DIM ** -0.5).astype(BF16), kmem, vmem,
      wo.astype(BF16), xa_post.reshape(1, -1))


def kernel(x, mem, ffn1_pre, ffn1_post, ffn1_wg, ffn1_wu, ffn1_wd, mix_pre, mix_post, w_in, gm_ln_g, gm_ln_b, gm_ws, gm_bs, ck_pe, ck_w1, ck_b1, ck_w2, cv_pe, cv_w1, cv_b1, cv_w2, rel_bias, out_gain_a, out_gain_b, w_out, xa_pre, xa_post, mem_norm, xa_wq, xa_wkv, xa_wo, ffn2_pre, ffn2_post, ffn2_wg, ffn2_wu, ffn2_wd):
    batch, seq, _ = x.shape
    depth = ffn1_pre.shape[0]
    x2d = x.reshape(batch * seq, D_MODEL)
    tb, tc, wm = _bias_tables(rel_bias)
    for l in range(depth):
        x2d = _ffn(x2d, ffn1_pre[l], ffn1_wg[l], ffn1_wu[l], ffn1_wd[l], ffn1_post[l])
        ya, q_aug, cmp_raw, ksel, vsel, kwin, vwin, gates = _mixproj(
            x2d, batch, seq, mix_pre[l], w_in[l], gm_ln_g[l], gm_ln_b[l], gm_ws[l], gm_bs[l],
            out_gain_a[l])
        cmp_aug = _compress(cmp_raw, batch, seq,
                            jnp.stack([ck_pe[l], cv_pe[l]]), jnp.stack([ck_w1[l], cv_w1[l]]),
                            jnp.stack([ck_b1[l], cv_b1[l]]), jnp.stack([ck_w2[l], cv_w2[l]]))
        yb = _nsa(q_aug, gates, cmp_aug, ksel, vsel, kwin, vwin, tb, tc, wm, batch, seq)
        kmem, vmem = _mem_kv(mem, mem_norm[l], xa_wkv[l])
        x2d = _mix_out(x2d, ya, yb, batch, seq, out_gain_b[l], w_out[l], mix_post[l], xa_pre[l],
                       xa_wq[l], kmem, vmem, xa_wo[l], xa_post[l])
        x2d = _ffn(x2d, ffn2_pre[l], ffn2_wg[l], ffn2_wu[l], ffn2_wd[l], ffn2_post[l])
    return x2d.reshape(batch, seq, D_MODEL)
```

```python
import functools
import math

import jax
import jax.numpy as jnp
from jax import lax
from jax.experimental import pallas as pl
from jax.experimental.pallas import tpu as pltpu

F32 = jnp.float32
BF16 = jnp.bfloat16

D_MODEL = 1024
MEM_LEN = 256
EPS = 1e-6
D_FF = 2816
GM_WIDTH = 512
GM_GROUPS = 4
GM_CH = GM_WIDTH // GM_GROUPS
CHUNK = 128
NSA_WIDTH = D_MODEL - GM_WIDTH
HEAD_DIM = 64
NSA_HEADS = NSA_WIDTH // HEAD_DIM
NSA_KV = 2
NSA_REP = NSA_HEADS // NSA_KV
KV_W = NSA_KV * HEAD_DIM
CMP_BLOCK = 32
CMP_STRIDE = 16
CMP_HIDDEN = 256
SEL_BLOCK = 64
N_SELECT = 16
WINDOW = 512
Q_BLOCK = 128
N_BUCKETS = 32
MAX_DISTANCE = 128
XA_HEADS = 4
XA_HEAD_DIM = D_MODEL // XA_HEADS
FORCE_SCORE = 1e4

LANE = 128
TOKEN_TILE = 512
FF_CHUNK = 256
KV_PAD = 512
KV_TILE = 512
CMP_PAD = 512
CMP_SPAN = 512
MASK_BF16 = -(2.0 ** 100)
MASK_F32 = -1e30
MAX_CLAMP = -(2.0 ** 99)
VMEM_LIMIT = 56 * 1024 * 1024

_NT = (((1,), (1,)), ((), ()))


def _rms(x, g):
    ms = jnp.mean(x * x, axis=-1, keepdims=True)
    return x * lax.rsqrt(ms + EPS) * g


def _dot(a, b):
    return jnp.dot(a, b, preferred_element_type=F32)


def _dot_nt(a, b):
    return lax.dot_general(a, b, _NT, preferred_element_type=F32)


def _params(sem, vmem=VMEM_LIMIT):
    return pltpu.CompilerParams(dimension_semantics=sem, vmem_limit_bytes=vmem)


def _const_spec(shape):
    nd = len(shape)
    return pl.BlockSpec(shape, lambda *_: (0,) * nd)


def _ffn_kernel(x_ref, pre_ref, wg_ref, wu_ref, wd_ref, post_ref, o_ref, acc_ref):
    x = x_ref[...]
    h = _rms(x, pre_ref[...]).astype(BF16)
    for c in range(D_FF // FF_CHUNK):
        sl = slice(c * FF_CHUNK, (c + 1) * FF_CHUNK)
        g = _dot(h, wg_ref[:, sl])
        u = _dot(h, wu_ref[:, sl])
        a = (jax.nn.silu(g) * u).astype(BF16)
        d = _dot(a, wd_ref[sl, :])
        if c == 0:
            acc_ref[...] = d
        else:
            acc_ref[...] += d
    o_ref[...] = x + 0.5 * _rms(acc_ref[...], post_ref[...])


def _ffn(x2d, pre, wg, wu, wd, post):
    n = x2d.shape[0]
    tm = TOKEN_TILE
    row = pl.BlockSpec((tm, D_MODEL), lambda t: (t, 0))
    return pl.pallas_call(
        _ffn_kernel,
        grid=(n // tm,),
        in_specs=[row, _const_spec((1, D_MODEL)), _const_spec((D_MODEL, D_FF)),
                  _const_spec((D_MODEL, D_FF)), _const_spec((D_FF, D_MODEL)),
                  _const_spec((1, D_MODEL))],
        out_specs=row,
        out_shape=jax.ShapeDtypeStruct((n, D_MODEL), F32),
        scratch_shapes=[pltpu.VMEM((tm, D_MODEL), F32)],
        compiler_params=_params(("arbitrary",)),
        name="ffn",
    )(x2d, pre.reshape(1, -1), wg.astype(BF16), wu.astype(BF16), wd.astype(BF16),
      post.reshape(1, -1))


def _mixproj_kernel(x_ref, pre_ref, wuv_ref, wq_ref, wc_ref, wkv_ref, wgt_ref,
                    lng_ref, lnb_ref, ws_ref, bs_ref, ga_ref,
                    ya_ref, q_ref, cmp_ref, ksel_ref, vsel_ref, kwin_ref, vwin_ref, gt_ref):
    t = pl.program_id(1)
    tm = x_ref.shape[0]
    lane = lax.broadcasted_iota(jnp.int32, (tm, LANE), 1)
    flag_col = jnp.where(lane == HEAD_DIM, MASK_BF16, 0.0).astype(BF16)

    @pl.when(t == 0)
    def _pad_rows():
        for g in range(NSA_KV):
            ksel_ref[g, :, 0:LANE] = flag_col
            ksel_ref[g, :, LANE:2 * LANE] = jnp.zeros((tm, LANE), BF16)
            kwin_ref[g] = flag_col
            vsel_ref[g] = jnp.zeros((tm, LANE), BF16)
            vwin_ref[g] = jnp.zeros((tm, LANE), BF16)

    @pl.when(t > 0)
    def _tile():
        h = _rms(x_ref[...], pre_ref[...]).astype(BF16)
        u = jax.nn.gelu(_dot(h, wuv_ref[:, 0:GM_WIDTH]))
        v = jax.nn.gelu(_dot(h, wuv_ref[:, GM_WIDTH:2 * GM_WIDTH]))
        r_i = lax.broadcasted_iota(jnp.int32, (CHUNK, CHUNK), 0)
        c_i = lax.broadcasted_iota(jnp.int32, (CHUNK, CHUNK), 1)
        for g in range(GM_GROUPS):
            gs = slice(g * GM_CH, (g + 1) * GM_CH)
            vg = v[:, gs]
            mu = jnp.mean(vg, axis=-1, keepdims=True)
            var = jnp.mean(jnp.square(vg - mu), axis=-1, keepdims=True)
            vn = ((vg - mu) * lax.rsqrt(var + EPS) * lng_ref[:, gs] + lnb_ref[:, gs]).astype(BF16)
            w = jnp.where(c_i <= r_i, ws_ref[g], 0.0).astype(BF16)
            for n in range(tm // CHUNK):
                rs = slice(n * CHUNK, (n + 1) * CHUNK)
                s = _dot(w, vn[rs]) + bs_ref[:, g:g + 1]
                ya_ref[rs, gs] = u[rs, gs] * s
        ya_ref[...] = _rms(ya_ref[...], ga_ref[...])
        one_col = jnp.where(lane == HEAD_DIM, 1.0, 0.0)
        zq = _dot(h, wq_ref[...])
        for hd in range(NSA_HEADS):
            q_ref[hd] = (zq[:, hd * LANE:(hd + 1) * LANE] + one_col).astype(BF16)
        zc = _dot(h, wc_ref[...])
        cmp_ref[0] = zc[:, 0:KV_W]
        cmp_ref[1] = zc[:, KV_W:2 * KV_W]
        zkv = _dot(h, wkv_ref[...])
        ones_hi = jnp.where(lane >= HEAD_DIM, 1.0, 0.0)
        pos_blk = ((t - 1) * tm + lax.broadcasted_iota(jnp.int32, (tm, LANE), 0)) // SEL_BLOCK
        blk_onehot = jnp.where(lane == pos_blk, MASK_BF16, 0.0).astype(BF16)
        for g in range(NSA_KV):
            ksel_ref[g, :, 0:LANE] = zkv[:, g * LANE:(g + 1) * LANE].astype(BF16)
            ksel_ref[g, :, LANE:2 * LANE] = blk_onehot
            vsel_ref[g] = (zkv[:, (2 + g) * LANE:(3 + g) * LANE] + ones_hi).astype(BF16)
            kwin_ref[g] = zkv[:, (4 + g) * LANE:(5 + g) * LANE].astype(BF16)
            vwin_ref[g] = (zkv[:, (6 + g) * LANE:(7 + g) * LANE] + ones_hi).astype(BF16)
        gt_ref[...] = jax.nn.sigmoid(_dot(h, wgt_ref[...]))


def _mixproj(x2d, batch, seq, mix_pre, w_in, ln_g, ln_b, ws, bs, gain_a):
    tm = TOKEN_TILE
    nt = seq // tm
    assert seq % tm == 0 and KV_PAD == tm
    scale = HEAD_DIM ** -0.5
    o_q = 2 * GM_WIDTH
    o_kv = o_q + NSA_WIDTH
    w_uv = w_in[:, :o_q]
    w_q = (w_in[:, o_q:o_kv] * scale).reshape(D_MODEL, NSA_HEADS, HEAD_DIM)
    w_q = jnp.pad(w_q, ((0, 0), (0, 0), (0, LANE - HEAD_DIM))).reshape(D_MODEL, NSA_HEADS * LANE)
    w_c = w_in[:, o_kv:o_kv + 2 * KV_W]
    w_kv = w_in[:, o_kv + 2 * KV_W:o_kv + 6 * KV_W].reshape(D_MODEL, 4 * NSA_KV, HEAD_DIM)
    w_kv = jnp.pad(w_kv, ((0, 0), (0, 0), (0, LANE - HEAD_DIM))).reshape(D_MODEL, 4 * NSA_KV * LANE)
    w_g = w_in[:, o_kv + 6 * KV_W:].reshape(D_MODEL, NSA_KV, NSA_REP, 3).transpose(0, 1, 3, 2)
    w_g = jnp.pad(w_g.reshape(D_MODEL, NSA_KV, 3 * NSA_REP),
                  ((0, 0), (0, 0), (0, LANE - 3 * NSA_REP))).reshape(D_MODEL, NSA_KV * LANE)

    def tile(b, t):
        return b * nt + jnp.maximum(t - 1, 0)

    n = batch * seq
    sp = seq + KV_PAD
    out_shape = [
        jax.ShapeDtypeStruct((n, GM_WIDTH), F32),
        jax.ShapeDtypeStruct((batch, NSA_HEADS, seq, LANE), BF16),
        jax.ShapeDtypeStruct((2, n, KV_W), F32),
        jax.ShapeDtypeStruct((batch, NSA_KV, sp, 2 * LANE), BF16),
        jax.ShapeDtypeStruct((batch, NSA_KV, sp, LANE), BF16),
        jax.ShapeDtypeStruct((batch, NSA_KV, sp, LANE), BF16),
        jax.ShapeDtypeStruct((batch, NSA_KV, sp, LANE), BF16),
        jax.ShapeDtypeStruct((n, NSA_KV * LANE), F32),
    ]
    padded = lambda w: pl.BlockSpec((None, NSA_KV, tm, w), lambda b, t: (b, 0, t, 0))
    out_specs = [
        pl.BlockSpec((tm, GM_WIDTH), lambda b, t: (tile(b, t), 0)),
        pl.BlockSpec((None, NSA_HEADS, tm, LANE), lambda b, t: (b, 0, jnp.maximum(t - 1, 0), 0)),
        pl.BlockSpec((2, tm, KV_W), lambda b, t: (0, tile(b, t), 0)),
        padded(2 * LANE), padded(LANE), padded(LANE), padded(LANE),
        pl.BlockSpec((tm, NSA_KV * LANE), lambda b, t: (tile(b, t), 0)),
    ]
    in_specs = [
        pl.BlockSpec((tm, D_MODEL), lambda b, t: (tile(b, t), 0)),
        _const_spec((1, D_MODEL)),
        _const_spec(w_uv.shape), _const_spec(w_q.shape), _const_spec(w_c.shape),
        _const_spec(w_kv.shape), _const_spec(w_g.shape),
        _const_spec((1, GM_WIDTH)), _const_spec((1, GM_WIDTH)),
        _const_spec((GM_GROUPS, CHUNK, CHUNK)), _const_spec((CHUNK, GM_GROUPS)),
        _const_spec((1, GM_WIDTH)),
    ]
    return pl.pallas_call(
        _mixproj_kernel,
        grid=(batch, nt + 1),
        in_specs=in_specs,
        out_specs=out_specs,
        out_shape=out_shape,
        compiler_params=_params(("arbitrary", "arbitrary")),
        name="mixproj",
    )(x2d, mix_pre.reshape(1, -1), w_uv.astype(BF16), w_q.astype(BF16), w_c.astype(BF16),
      w_kv.astype(BF16), w_g.astype(BF16), ln_g.reshape(1, -1), ln_b.reshape(1, -1),
      ws, bs.T, gain_a.reshape(1, -1))


def _compress_kernel(raw_ref, pea_ref, peb_ref, wa_ref, wb_ref, b1_ref, w2_ref, o_ref):
    is_v = pl.program_id(0) == 1
    nu = raw_ref.shape[0]
    u = raw_ref[...]
    a = _dot((u + pea_ref[...]).astype(BF16), wa_ref[...])
    b = _dot((u + peb_ref[...]).astype(BF16), wb_ref[...])
    hid = jax.nn.gelu(a + pltpu.roll(b, nu - 1, 0) + b1_ref[...])
    out = _dot(hid.astype(BF16), w2_ref[...])
    lane = lax.broadcasted_iota(jnp.int32, (nu, LANE), 1)
    ci = lax.broadcasted_iota(jnp.int32, (nu, LANE), 0)
    u_c = CMP_BLOCK // CMP_STRIDE
    u_s = SEL_BLOCK // CMP_STRIDE
    overlap = jnp.maximum(jnp.minimum(ci + u_c, u_s * (lane + 1)) - jnp.maximum(ci, u_s * lane), 0)
    extra = jnp.where(is_v, overlap.astype(F32), 0.0)
    ones_hi = jnp.where(jnp.logical_and(is_v, lane >= HEAD_DIM), 1.0, 0.0)
    lane_p = lax.broadcasted_iota(jnp.int32, (CMP_PAD, LANE), 1)
    pad_lo = jnp.where(jnp.logical_and(jnp.logical_not(is_v), lane_p == HEAD_DIM), MASK_BF16, 0.0)
    for g in range(NSA_KV):
        o_ref[g, 0:CMP_PAD, 0:LANE] = pad_lo
        o_ref[g, 0:CMP_PAD, LANE:2 * LANE] = jnp.zeros((CMP_PAD, LANE), F32)
        o_ref[g, CMP_PAD:CMP_PAD + nu, 0:LANE] = out[:, g * LANE:(g + 1) * LANE] + ones_hi
        o_ref[g, CMP_PAD:CMP_PAD + nu, LANE:2 * LANE] = extra


def _compress(cmp_raw, batch, seq, pe, w1, b1, w2):
    nu = seq // CMP_STRIDE
    half = CMP_STRIDE * HEAD_DIM
    raw = cmp_raw.reshape(2, batch, nu, CMP_STRIDE * KV_W)

    def spread(w):
        w = w.reshape(2, CMP_STRIDE, 1, HEAD_DIM, 1, CMP_HIDDEN)
        eye = jnp.eye(NSA_KV, dtype=w.dtype).reshape(1, 1, NSA_KV, 1, NSA_KV, 1)
        return (w * eye).reshape(2, CMP_STRIDE * KV_W, NSA_KV * CMP_HIDDEN)

    wa = spread(w1[:, :half])
    wb = spread(w1[:, half:])
    pe_u = jnp.broadcast_to(pe.reshape(2, 2, CMP_STRIDE, 1, HEAD_DIM),
                            (2, 2, CMP_STRIDE, NSA_KV, HEAD_DIM)).reshape(2, 2, 1, CMP_STRIDE * KV_W)
    b1t = jnp.tile(b1, (1, NSA_KV)).reshape(2, 1, NSA_KV * CMP_HIDDEN)
    w2p = jnp.pad(w2, ((0, 0), (0, 0), (0, LANE - HEAD_DIM)))
    eye = jnp.eye(NSA_KV, dtype=w2.dtype)
    w2b = (eye[None, :, None, :, None] * w2p[:, None, :, None, :]).reshape(
        2, NSA_KV * CMP_HIDDEN, NSA_KV * LANE)
    rows = CMP_PAD + nu
    kd = CMP_STRIDE * KV_W
    return pl.pallas_call(
        _compress_kernel,
        grid=(2, batch),
        in_specs=[
            pl.BlockSpec((None, None, nu, kd), lambda k, b: (k, b, 0, 0)),
            pl.BlockSpec((None, None, 1, kd), lambda k, b: (k, 0, 0, 0)),
            pl.BlockSpec((None, None, 1, kd), lambda k, b: (k, 1, 0, 0)),
            pl.BlockSpec((None, kd, NSA_KV * CMP_HIDDEN), lambda k, b: (k, 0, 0)),
            pl.BlockSpec((None, kd, NSA_KV * CMP_HIDDEN), lambda k, b: (k, 0, 0)),
            pl.BlockSpec((None, 1, NSA_KV * CMP_HIDDEN), lambda k, b: (k, 0, 0)),
            pl.BlockSpec((None, NSA_KV * CMP_HIDDEN, NSA_KV * LANE), lambda k, b: (k, 0, 0)),
        ],
        out_specs=pl.BlockSpec((None, None, NSA_KV, rows, 2 * LANE), lambda k, b: (k, b, 0, 0, 0)),
        out_shape=jax.ShapeDtypeStruct((2, batch, NSA_KV, rows, 2 * LANE), F32),
        compiler_params=_params(("arbitrary", "arbitrary")),
        name="compress",
    )(raw, pe_u, pe_u, wa.astype(BF16), wb.astype(BF16), b1t, w2b.astype(BF16))


def _t5_bucket(dist):
    n = jnp.maximum(dist, 0)
    max_exact = N_BUCKETS // 2
    nf = jnp.maximum(n, 1).astype(F32)
    large = max_exact + (jnp.log(nf * (1.0 / max_exact)) / math.log(MAX_DISTANCE / max_exact)
                         * (N_BUCKETS - max_exact)).astype(jnp.int32)
    large = jnp.minimum(large, N_BUCKETS - 1)
    return jnp.where(n < max_exact, n, large)


def _tables_kernel(rb_ref, tb_ref, tc_ref, wm_ref):
    def fill(out_ref, dist):
        bk = _t5_bucket(dist)
        for h in range(NSA_HEADS):
            far = rb_ref[N_BUCKETS - 1, h]
            val = jnp.zeros(dist.shape, F32)
            for b in range(N_BUCKETS - 1):
                val = jnp.where(bk == b, rb_ref[b, h] - far, val)
            out_ref[h] = jnp.where(dist >= 0, val, MASK_F32)

    tq = lax.broadcasted_iota(jnp.int32, (Q_BLOCK, 2 * Q_BLOCK), 0)
    j = lax.broadcasted_iota(jnp.int32, (Q_BLOCK, 2 * Q_BLOCK), 1)
    fill(tb_ref, tq + Q_BLOCK - j)
    tq = lax.broadcasted_iota(jnp.int32, (Q_BLOCK, LANE), 0)
    c = lax.broadcasted_iota(jnp.int32, (Q_BLOCK, LANE), 1)
    cmp_first = Q_BLOCK // CMP_STRIDE - LANE
    fill(tc_ref, tq - CMP_STRIDE * (cmp_first + c) - (CMP_BLOCK - 1))
    wm_ref[...] = jnp.where(c > tq, 0.0, MASK_F32)


def _bias_tables(rel_bias):
    return pl.pallas_call(
        _tables_kernel,
        in_specs=[pl.BlockSpec(memory_space=pltpu.SMEM)],
        out_shape=[jax.ShapeDtypeStruct((NSA_HEADS, Q_BLOCK, 2 * Q_BLOCK), F32),
                   jax.ShapeDtypeStruct((NSA_HEADS, Q_BLOCK, LANE), F32),
                   jax.ShapeDtypeStruct((Q_BLOCK, LANE), F32)],
        name="bias_tables",
    )(rel_bias)


def _nsa_kernel(q_ref, gt_ref, kc_ref, vc_ref, ks_ref, vs_ref, kw_ref, vw_ref,
                tb_ref, tc_ref, wm_ref, o_ref, qsel_ref, s0_ref, s1_ref, m0_ref, m1_ref,
                al0_ref, al1_ref, acc_ref, *, n_sel):
    i = pl.program_id(2)
    s_ref, m_ref, al_ref = (s0_ref, s1_ref), (m0_ref, m1_ref), (al0_ref, al1_ref)
    rows = NSA_REP * Q_BLOCK
    qa = q_ref[...].reshape(rows, LANE)
    tb = tb_ref[...].reshape(rows, 2 * Q_BLOCK)
    zero_tb = jnp.zeros((rows, KV_TILE - 2 * Q_BLOCK), F32)

    c0 = pl.multiple_of(8 * i + 8 + CMP_PAD - CMP_SPAN, 8)
    s = _dot_nt(qa, kc_ref[pl.ds(c0, CMP_SPAN), :].astype(BF16))
    s = s + jnp.concatenate([jnp.zeros((rows, CMP_SPAN - LANE), F32),
                             tc_ref[...].reshape(rows, LANE)], axis=1)
    m = jnp.maximum(jnp.max(s, axis=1, keepdims=True), MAX_CLAMP)
    acc = _dot(jnp.exp(s - m).astype(BF16), vc_ref[pl.ds(c0, CMP_SPAN), :].astype(BF16))
    inv = 1.0 / jnp.maximum(acc[:, HEAD_DIM:HEAD_DIM + 1], 1e-30)
    oc = acc[:, 0:LANE] * inv
    impu = acc[:, LANE:2 * LANE] * inv
    imp = impu[0:Q_BLOCK]
    for r in range(1, NSA_REP):
        imp = imp + impu[r * Q_BLOCK:(r + 1) * Q_BLOCK]

    t0 = pl.multiple_of(Q_BLOCK * (i + 1) + KV_PAD - KV_TILE, Q_BLOCK)
    w1 = pl.multiple_of(t0 - Q_BLOCK, Q_BLOCK)
    s0 = _dot_nt(qa, kw_ref[pl.ds(t0, KV_TILE), :]) + jnp.concatenate([zero_tb, tb], axis=1)
    s1 = _dot_nt(qa, kw_ref[pl.ds(w1, Q_BLOCK), :]) + jnp.tile(wm_ref[...], (NSA_REP, 1))
    m = jnp.maximum(jnp.max(s0, axis=1, keepdims=True), jnp.max(s1, axis=1, keepdims=True))
    acc = (_dot(jnp.exp(s0 - m).astype(BF16), vw_ref[pl.ds(t0, KV_TILE), :])
           + _dot(jnp.exp(s1 - m).astype(BF16), vw_ref[pl.ds(w1, Q_BLOCK), :]))
    ow = acc * (1.0 / acc[:, HEAD_DIM:HEAD_DIM + 1])

    imp_t = imp.T
    jb = lax.broadcasted_iota(jnp.int32, (LANE, Q_BLOCK), 0)
    t = i * Q_BLOCK + lax.broadcasted_iota(jnp.int32, (LANE, Q_BLOCK), 1)
    cur = t // SEL_BLOCK
    forced = (jb == 0) | (jb == cur) | (jb == cur - 1)
    score = jnp.where(forced, FORCE_SCORE, jnp.where(jb * SEL_BLOCK <= t, imp_t, -1.0))
    notsel = jnp.ones((LANE, Q_BLOCK), F32)
    for _ in range(n_sel):
        mx = jnp.max(score, axis=0, keepdims=True)
        first = jnp.min(jnp.where(score == mx, jb, LANE), axis=0, keepdims=True)
        hit = jb == first
        notsel = jnp.where(hit, 0.0, notsel)
        score = jnp.where(hit, -2.0, score)
    notsel_q = notsel.T.astype(BF16)
    qsel_ref[:, 0:LANE] = qa
    for r in range(NSA_REP):
        qsel_ref[r * Q_BLOCK:(r + 1) * Q_BLOCK, LANE:2 * LANE] = notsel_q
    qs = qsel_ref[...]

    n_tiles = (i + 1 + KV_TILE // Q_BLOCK - 1) // (KV_TILE // Q_BLOCK)

    def bcast(col):
        return jnp.broadcast_to(col, (rows, LANE))

    def lanes(x):
        return jnp.concatenate([x] * (KV_TILE // LANE), axis=1)

    def qk_stage(k, prev, slot):
        st = pl.multiple_of(t0 - KV_TILE * k, Q_BLOCK)
        s_n = _dot_nt(qs, ks_ref[pl.ds(st, KV_TILE), :])
        s_ref[slot][...] = s_n
        m_c = m_ref[prev][...]
        m_n = jnp.maximum(m_c, bcast(jnp.max(s_n, axis=1, keepdims=True)))
        m_ref[slot][...] = m_n
        al_ref[slot][...] = jnp.exp(m_c - m_n)

    def pv_stage(k, slot):
        st = pl.multiple_of(t0 - KV_TILE * k, Q_BLOCK)
        p = jnp.exp(s_ref[slot][...] - lanes(m_ref[slot][...]))
        acc_ref[...] = (al_ref[slot][...] * acc_ref[...]
                        + _dot(p.astype(BF16), vs_ref[pl.ds(st, KV_TILE), :]))

    s = _dot_nt(qs, ks_ref[pl.ds(t0, KV_TILE), :]) + jnp.concatenate([zero_tb, tb], axis=1)
    s_ref[0][...] = s
    m_ref[0][...] = bcast(jnp.max(s, axis=1, keepdims=True))
    al_ref[0][...] = jnp.ones((rows, LANE), F32)
    acc_ref[...] = jnp.zeros((rows, LANE), F32)

    def sel_body(j, carry):
        qk_stage(2 * j + 1, 0, 1)
        pv_stage(2 * j, 0)
        qk_stage(2 * j + 2, 1, 0)
        pv_stage(2 * j + 1, 1)
        return carry

    n_pairs = (n_tiles - 1) // 2
    lax.fori_loop(0, n_pairs, sel_body, 0)
    odd_left = n_tiles - 1 - 2 * n_pairs

    @pl.when(odd_left == 1)
    def _two_left():
        qk_stage(n_tiles - 1, 0, 1)
        pv_stage(n_tiles - 2, 0)
        pv_stage(n_tiles - 1, 1)

    @pl.when(odd_left == 0)
    def _one_left():
        pv_stage(n_tiles - 1, 0)

    acc = acc_ref[...]
    osl = acc * (1.0 / acc[:, HEAD_DIM:HEAD_DIM + 1])

    gts = gt_ref[...]
    lane = lax.broadcasted_iota(jnp.int32, (Q_BLOCK, LANE), 1)
    heads = []
    for r in range(NSA_REP):
        rs = slice(r * Q_BLOCK, (r + 1) * Q_BLOCK)
        heads.append(gts[:, r:r + 1] * oc[rs]
                     + gts[:, NSA_REP + r:NSA_REP + r + 1] * osl[rs]
                     + gts[:, 2 * NSA_REP + r:2 * NSA_REP + r + 1] * ow[rs])
    for pr in range(NSA_REP // 2):
        o_ref[:, pr * LANE:(pr + 1) * LANE] = jnp.where(
            lane < HEAD_DIM, heads[2 * pr], pltpu.roll(heads[2 * pr + 1], HEAD_DIM, 1))


def _nsa(q_aug, gates, cmp_aug, ksel, vsel, kwin, vwin, tb, tc, wm, batch, seq):
    nq = seq // Q_BLOCK
    sp = seq + KV_PAD
    crow = cmp_aug.shape[3]
    n_sel = min(N_SELECT, seq // SEL_BLOCK)
    assert seq // SEL_BLOCK <= LANE and KV_PAD >= KV_TILE and KV_PAD >= WINDOW
    assert seq // CMP_STRIDE <= CMP_SPAN <= CMP_PAD and crow == CMP_PAD + seq // CMP_STRIDE
    kv_spec = lambda w: pl.BlockSpec((None, None, sp, w), lambda b, g, i: (b, g, 0, 0))
    in_specs = [
        pl.BlockSpec((None, NSA_REP, Q_BLOCK, LANE), lambda b, g, i: (b, g, i, 0)),
        pl.BlockSpec((Q_BLOCK, LANE), lambda b, g, i: (b * nq + i, g)),
        pl.BlockSpec((None, None, None, crow, LANE), lambda b, g, i: (0, b, g, 0, 0)),
        pl.BlockSpec((None, None, None, crow, 2 * LANE), lambda b, g, i: (1, b, g, 0, 0)),
        kv_spec(2 * LANE), kv_spec(LANE), kv_spec(LANE), kv_spec(LANE),
        pl.BlockSpec((NSA_REP, Q_BLOCK, 2 * Q_BLOCK), lambda b, g, i: (g, 0, 0)),
        pl.BlockSpec((NSA_REP, Q_BLOCK, LANE), lambda b, g, i: (g, 0, 0)),
        pl.BlockSpec((Q_BLOCK, LANE), lambda b, g, i: (0, 0)),
    ]
    rows = NSA_REP * Q_BLOCK
    return pl.pallas_call(
        functools.partial(_nsa_kernel, n_sel=n_sel),
        grid=(batch, NSA_KV, nq),
        in_specs=in_specs,
        out_specs=pl.BlockSpec((Q_BLOCK, NSA_REP * HEAD_DIM), lambda b, g, i: (b * nq + i, g)),
        out_shape=jax.ShapeDtypeStruct((batch * seq, NSA_WIDTH), F32),
        scratch_shapes=[pltpu.VMEM((rows, 2 * LANE), BF16),
                        pltpu.VMEM((rows, KV_TILE), F32),
                        pltpu.VMEM((rows, KV_TILE), F32),
                        pltpu.VMEM((rows, LANE), F32),
                        pltpu.VMEM((rows, LANE), F32),
                        pltpu.VMEM((rows, LANE), F32),
                        pltpu.VMEM((rows, LANE), F32),
                        pltpu.VMEM((rows, LANE), F32)],
        compiler_params=_params(("arbitrary", "arbitrary", "arbitrary")),
        name="nsa",
    )(q_aug, gates, cmp_aug, cmp_aug, ksel, vsel, kwin, vwin, tb, tc, wm)


def _memkv_kernel(mem_ref, g_ref, wk_ref, wv_ref, k_ref, v_ref):
    h = _rms(mem_ref[...], g_ref[...]).astype(BF16)
    k_ref[...] = _dot(h, wk_ref[...]).astype(BF16)
    v_ref[...] = _dot(h, wv_ref[...]).astype(BF16)


def _mem_kv(mem, g_mem, wkv):
    batch, m, _ = mem.shape
    blk = pl.BlockSpec((None, m, D_MODEL), lambda b: (b, 0, 0))
    return pl.pallas_call(
        _memkv_kernel,
        grid=(batch,),
        in_specs=[blk, _const_spec((1, D_MODEL)), _const_spec((D_MODEL, D_MODEL)),
                  _const_spec((D_MODEL, D_MODEL))],
        out_specs=[blk, blk],
        out_shape=[jax.ShapeDtypeStruct((batch, m, D_MODEL), BF16)] * 2,
        compiler_params=_params(("arbitrary",)),
        name="mem_kv",
    )(mem, g_mem.reshape(1, -1), wkv[:, :D_MODEL].astype(BF16), wkv[:, D_MODEL:].astype(BF16))


def _mixout_kernel(x_ref, ya_ref, yb_ref, gb_ref, wo_ref, post_ref, xpre_ref, wq_ref,
                   km_ref, vm_ref, wxo_ref, xpost_ref, o_ref):
    yb = _rms(yb_ref[...], gb_ref[...]).astype(BF16)
    y = _dot(ya_ref[...].astype(BF16), wo_ref[0:GM_WIDTH, :]) + _dot(yb, wo_ref[GM_WIDTH:, :])
    x1 = x_ref[...] + _rms(y, post_ref[...])
    h = _rms(x1, xpre_ref[...]).astype(BF16)
    q = _dot(h, wq_ref[...]).astype(BF16)
    outs = []
    for hd in range(XA_HEADS):
        hs = slice(hd * XA_HEAD_DIM, (hd + 1) * XA_HEAD_DIM)
        s = _dot_nt(q[:, hs], km_ref[:, hs])
        e = jnp.exp(s - jnp.max(s, axis=1, keepdims=True))
        o = _dot(e.astype(BF16), vm_ref[:, hs]) / jnp.sum(e, axis=1, keepdims=True)
        outs.append(o.astype(BF16))
    o = _dot(jnp.concatenate(outs, axis=1), wxo_ref[...])
    o_ref[...] = x1 + _rms(o, xpost_ref[...])


def _mix_out(x2d, ya, yb, batch, seq, gain_b, w_out, mix_post, xa_pre, wq, kmem, vmem, wo, xa_post):
    tm = TOKEN_TILE
    nt = seq // tm
    row = lambda w: pl.BlockSpec((tm, w), lambda b, t: (b * nt + t, 0))
    mem_spec = pl.BlockSpec((None, kmem.shape[1], D_MODEL), lambda b, t: (b, 0, 0))
    sq = _const_spec((D_MODEL, D_MODEL))
    vec = _const_spec((1, D_MODEL))
    return pl.pallas_call(
        _mixout_kernel,
        grid=(batch, nt),
        in_specs=[row(D_MODEL), row(GM_WIDTH), row(NSA_WIDTH), _const_spec((1, NSA_WIDTH)), sq, vec,
                  vec, sq, mem_spec, mem_spec, sq, vec],
        out_specs=row(D_MODEL),
        out_shape=jax.ShapeDtypeStruct(x2d.shape, F32),
        compiler_params=_params(("arbitrary", "arbitrary")),
        name="mix_out",
    )(x2d, ya, yb, gain_b.reshape(1, -1), w_out.astype(BF16), mix_post.reshape(1, -1),
      xa_pre.reshape(1, -1), (wq * XA_HEAD_DIM ** -0.5).astype(BF16), kmem, vmem,
      wo.astype(BF16), xa_post.reshape(1, -1))


def kernel(x, mem, ffn1_pre, ffn1_post, ffn1_wg, ffn1_wu, ffn1_wd, mix_pre, mix_post, w_in, gm_ln_g, gm_ln_b, gm_ws, gm_bs, ck_pe, ck_w1, ck_b1, ck_w2, cv_pe, cv_w1, cv_b1, cv_w2, rel_bias, out_gain_a, out_gain_b, w_out, xa_pre, xa_post, mem_norm, xa_wq, xa_wkv, xa_wo, ffn2_pre, ffn2_post, ffn2_wg, ffn2_wu, ffn2_wd):
    batch, seq, _ = x.shape
    depth = ffn1_pre.shape[0]
    x2d = x.reshape(batch * seq, D_MODEL)
    tb, tc, wm = _bias_tables(rel_bias)
    for l in range(depth):
        x2d = _ffn(x2d, ffn1_pre[l], ffn1_wg[l], ffn1_wu[l], ffn1_wd[l], ffn1_post[l])
        ya, q_aug, cmp_raw, ksel, vsel, kwin, vwin, gates = _mixproj(
            x2d, batch, seq, mix_pre[l], w_in[l], gm_ln_g[l], gm_ln_b[l], gm_ws[l], gm_bs[l],
            out_gain_a[l])
        cmp_aug = _compress(cmp_raw, batch, seq,
                            jnp.stack([ck_pe[l], cv_pe[l]]), jnp.stack([ck_w1[l], cv_w1[l]]),
                            jnp.stack([ck_b1[l], cv_b1[l]]), jnp.stack([ck_w2[l], cv_w2[l]]))
        yb = _nsa(q_aug, gates, cmp_aug, ksel, vsel, kwin, vwin, tb, tc, wm, batch, seq)
        kmem, vmem = _mem_kv(mem, mem_norm[l], xa_wkv[l])
        x2d = _mix_out(x2d, ya, yb, batch, seq, out_gain_b[l], w_out[l], mix_post[l], xa_pre[l],
                       xa_wq[l], kmem, vmem, xa_wo[l], xa_post[l])
        x2d = _ffn(x2d, ffn2_pre[l], ffn2_wg[l], ffn2_wu[l], ffn2_wd[l], ffn2_post[l])
    return x2d.reshape(batch, seq, D_MODEL)
```

```python
import functools
import math

import jax
import jax.numpy as jnp
from jax import lax
from jax.experimental import pallas as pl
from jax.experimental.pallas import tpu as pltpu

F32 = jnp.float32
BF16 = jnp.bfloat16

D_MODEL = 1024
MEM_LEN = 256
EPS = 1e-6
D_FF = 2816
GM_WIDTH = 512
GM_GROUPS = 4
GM_CH = GM_WIDTH // GM_GROUPS
CHUNK = 128
NSA_WIDTH = D_MODEL - GM_WIDTH
HEAD_DIM = 64
NSA_HEADS = NSA_WIDTH // HEAD_DIM
NSA_KV = 2
NSA_REP = NSA_HEADS // NSA_KV
KV_W = NSA_KV * HEAD_DIM
CMP_BLOCK = 32
CMP_STRIDE = 16
CMP_HIDDEN = 256
SEL_BLOCK = 64
N_SELECT = 16
WINDOW = 512
Q_BLOCK = 128
N_BUCKETS = 32
MAX_DISTANCE = 128
XA_HEADS = 4
XA_HEAD_DIM = D_MODEL // XA_HEADS
FORCE_SCORE = 1e4

LANE = 128
TOKEN_TILE = 512
FF_CHUNK = 256
KV_PAD = 512
KV_TILE = 512
CMP_PAD = 512
CMP_SPAN = 512
MASK_BF16 = -(2.0 ** 100)
MASK_F32 = -1e30
MAX_CLAMP = -(2.0 ** 99)
LOG2E = math.log2(math.e)
VMEM_LIMIT = 56 * 1024 * 1024

_NT = (((1,), (1,)), ((), ()))


def _rms(x, g):
    ms = jnp.mean(x * x, axis=-1, keepdims=True)
    return x * lax.rsqrt(ms + EPS) * g


def _dot(a, b):
    return jnp.dot(a, b, preferred_element_type=F32)


def _dot_nt(a, b):
    return lax.dot_general(a, b, _NT, preferred_element_type=F32)


def _params(sem, vmem=VMEM_LIMIT):
    return pltpu.CompilerParams(dimension_semantics=sem, vmem_limit_bytes=vmem)


def _const_spec(shape):
    nd = len(shape)
    return pl.BlockSpec(shape, lambda *_: (0,) * nd)


def _ffn_kernel(x_ref, pre_ref, wg_ref, wu_ref, wd_ref, post_ref, o_ref, acc_ref):
    x = x_ref[...]
    h = _rms(x, pre_ref[...]).astype(BF16)
    for c in range(D_FF // FF_CHUNK):
        sl = slice(c * FF_CHUNK, (c + 1) * FF_CHUNK)
        g = _dot(h, wg_ref[:, sl])
        u = _dot(h, wu_ref[:, sl])
        a = (jax.nn.silu(g) * u).astype(BF16)
        d = _dot(a, wd_ref[sl, :])
        if c == 0:
            acc_ref[...] = d
        else:
            acc_ref[...] += d
    o_ref[...] = x + 0.5 * _rms(acc_ref[...], post_ref[...])


def _ffn(x2d, pre, wg, wu, wd, post):
    n = x2d.shape[0]
    tm = TOKEN_TILE
    row = pl.BlockSpec((tm, D_MODEL), lambda t: (t, 0))
    return pl.pallas_call(
        _ffn_kernel,
        grid=(n // tm,),
        in_specs=[row, _const_spec((1, D_MODEL)), _const_spec((D_MODEL, D_FF)),
                  _const_spec((D_MODEL, D_FF)), _const_spec((D_FF, D_MODEL)),
                  _const_spec((1, D_MODEL))],
        out_specs=row,
        out_shape=jax.ShapeDtypeStruct((n, D_MODEL), F32),
        scratch_shapes=[pltpu.VMEM((tm, D_MODEL), F32)],
        compiler_params=_params(("arbitrary",)),
        name="ffn",
    )(x2d, pre.reshape(1, -1), wg.astype(BF16), wu.astype(BF16), wd.astype(BF16),
      post.reshape(1, -1))


def _mixproj_kernel(x_ref, pre_ref, wuv_ref, wq_ref, wc_ref, wkv_ref, wgt_ref,
                    lng_ref, lnb_ref, ws_ref, bs_ref, ga_ref,
                    ya_ref, q_ref, cmp_ref, ksel_ref, vsel_ref, kwin_ref, vwin_ref, gt_ref):
    t = pl.program_id(1)
    tm = x_ref.shape[0]
    lane = lax.broadcasted_iota(jnp.int32, (tm, LANE), 1)
    flag_col = jnp.where(lane == HEAD_DIM, MASK_BF16, 0.0).astype(BF16)

    @pl.when(t == 0)
    def _pad_rows():
        for g in range(NSA_KV):
            ksel_ref[g, :, 0:LANE] = flag_col
            ksel_ref[g, :, LANE:2 * LANE] = jnp.zeros((tm, LANE), BF16)
            kwin_ref[g] = flag_col
            vsel_ref[g] = jnp.zeros((tm, 2 * LANE), BF16)
            vwin_ref[g] = jnp.zeros((tm, 2 * LANE), BF16)

    @pl.when(t > 0)
    def _tile():
        h = _rms(x_ref[...], pre_ref[...]).astype(BF16)
        u = jax.nn.gelu(_dot(h, wuv_ref[:, 0:GM_WIDTH]))
        v = jax.nn.gelu(_dot(h, wuv_ref[:, GM_WIDTH:2 * GM_WIDTH]))
        r_i = lax.broadcasted_iota(jnp.int32, (CHUNK, CHUNK), 0)
        c_i = lax.broadcasted_iota(jnp.int32, (CHUNK, CHUNK), 1)
        for g in range(GM_GROUPS):
            gs = slice(g * GM_CH, (g + 1) * GM_CH)
            vg = v[:, gs]
            mu = jnp.mean(vg, axis=-1, keepdims=True)
            var = jnp.mean(jnp.square(vg - mu), axis=-1, keepdims=True)
            vn = ((vg - mu) * lax.rsqrt(var + EPS) * lng_ref[:, gs] + lnb_ref[:, gs]).astype(BF16)
            w = jnp.where(c_i <= r_i, ws_ref[g], 0.0).astype(BF16)
            for n in range(tm // CHUNK):
                rs = slice(n * CHUNK, (n + 1) * CHUNK)
                s = _dot(w, vn[rs]) + bs_ref[:, g:g + 1]
                ya_ref[rs, gs] = u[rs, gs] * s
        ya_ref[...] = _rms(ya_ref[...], ga_ref[...])
        one_col = jnp.where(lane == HEAD_DIM, 1.0, 0.0)
        zq = _dot(h, wq_ref[...])
        for hd in range(NSA_HEADS):
            q_ref[hd] = (zq[:, hd * LANE:(hd + 1) * LANE] + one_col).astype(BF16)
        zc = _dot(h, wc_ref[...])
        cmp_ref[0] = zc[:, 0:KV_W]
        cmp_ref[1] = zc[:, KV_W:2 * KV_W]
        zkv = _dot(h, wkv_ref[...])
        ones_hi = jnp.where(lane >= HEAD_DIM, 1.0, 0.0)
        ones_lo = 1.0 - ones_hi

        def v_aug(v):
            return jnp.concatenate([v + ones_hi, pltpu.roll(v, HEAD_DIM, 1) + ones_lo],
                                   axis=1).astype(BF16)

        pos_blk = ((t - 1) * tm + lax.broadcasted_iota(jnp.int32, (tm, LANE), 0)) // SEL_BLOCK
        blk_onehot = jnp.where(lane == pos_blk, MASK_BF16, 0.0).astype(BF16)
        for g in range(NSA_KV):
            ksel_ref[g, :, 0:LANE] = zkv[:, g * LANE:(g + 1) * LANE].astype(BF16)
            ksel_ref[g, :, LANE:2 * LANE] = blk_onehot
            vsel_ref[g] = v_aug(zkv[:, (2 + g) * LANE:(3 + g) * LANE])
            kwin_ref[g] = zkv[:, (4 + g) * LANE:(5 + g) * LANE].astype(BF16)
            vwin_ref[g] = v_aug(zkv[:, (6 + g) * LANE:(7 + g) * LANE])
        sg = jax.nn.sigmoid(_dot(h, wgt_ref[...]))
        hi = sg.astype(BF16)
        lo = (sg - hi.astype(F32)).astype(BF16)
        for g in range(NSA_KV):
            gt_ref[:, 2 * g * LANE:(2 * g + 1) * LANE] = hi[:, g * LANE:(g + 1) * LANE]
            gt_ref[:, (2 * g + 1) * LANE:(2 * g + 2) * LANE] = lo[:, g * LANE:(g + 1) * LANE]


def _mixproj(x2d, batch, seq, mix_pre, w_in, ln_g, ln_b, ws, bs, gain_a):
    tm = TOKEN_TILE
    nt = seq // tm
    assert seq % tm == 0 and KV_PAD == tm
    scale = HEAD_DIM ** -0.5 * LOG2E
    o_q = 2 * GM_WIDTH
    o_kv = o_q + NSA_WIDTH
    w_uv = w_in[:, :o_q]
    w_q = (w_in[:, o_q:o_kv] * scale).reshape(D_MODEL, NSA_HEADS, HEAD_DIM)
    w_q = jnp.pad(w_q, ((0, 0), (0, 0), (0, LANE - HEAD_DIM))).reshape(D_MODEL, NSA_HEADS * LANE)
    w_c = w_in[:, o_kv:o_kv + 2 * KV_W]
    w_kv = w_in[:, o_kv + 2 * KV_W:o_kv + 6 * KV_W].reshape(D_MODEL, 4 * NSA_KV, HEAD_DIM)
    w_kv = jnp.pad(w_kv, ((0, 0), (0, 0), (0, LANE - HEAD_DIM))).reshape(D_MODEL, 4 * NSA_KV * LANE)
    w_g = w_in[:, o_kv + 6 * KV_W:].reshape(D_MODEL, NSA_KV, NSA_REP, 3).transpose(0, 1, 3, 2)
    w_g = jnp.pad(w_g.reshape(D_MODEL, NSA_KV, 3 * NSA_REP),
                  ((0, 0), (0, 0), (0, LANE - 3 * NSA_REP))).reshape(D_MODEL, NSA_KV * LANE)

    def tile(b, t):
        return b * nt + jnp.maximum(t - 1, 0)

    n = batch * seq
    sp = seq + KV_PAD
    out_shape = [
        jax.ShapeDtypeStruct((n, GM_WIDTH), F32),
        jax.ShapeDtypeStruct((batch, NSA_HEADS, seq, LANE), BF16),
        jax.ShapeDtypeStruct((2, n, KV_W), F32),
        jax.ShapeDtypeStruct((batch, NSA_KV, sp, 2 * LANE), BF16),
        jax.ShapeDtypeStruct((batch, NSA_KV, sp, 2 * LANE), BF16),
        jax.ShapeDtypeStruct((batch, NSA_KV, sp, LANE), BF16),
        jax.ShapeDtypeStruct((batch, NSA_KV, sp, 2 * LANE), BF16),
        jax.ShapeDtypeStruct((n, 2 * NSA_KV * LANE), BF16),
    ]
    padded = lambda w: pl.BlockSpec((None, NSA_KV, tm, w), lambda b, t: (b, 0, t, 0))
    out_specs = [
        pl.BlockSpec((tm, GM_WIDTH), lambda b, t: (tile(b, t), 0)),
        pl.BlockSpec((None, NSA_HEADS, tm, LANE), lambda b, t: (b, 0, jnp.maximum(t - 1, 0), 0)),
        pl.BlockSpec((2, tm, KV_W), lambda b, t: (0, tile(b, t), 0)),
        padded(2 * LANE), padded(2 * LANE), padded(LANE), padded(2 * LANE),
        pl.BlockSpec((tm, 2 * NSA_KV * LANE), lambda b, t: (tile(b, t), 0)),
    ]
    in_specs = [
        pl.BlockSpec((tm, D_MODEL), lambda b, t: (tile(b, t), 0)),
        _const_spec((1, D_MODEL)),
        _const_spec(w_uv.shape), _const_spec(w_q.shape), _const_spec(w_c.shape),
        _const_spec(w_kv.shape), _const_spec(w_g.shape),
        _const_spec((1, GM_WIDTH)), _const_spec((1, GM_WIDTH)),
        _const_spec((GM_GROUPS, CHUNK, CHUNK)), _const_spec((CHUNK, GM_GROUPS)),
        _const_spec((1, GM_WIDTH)),
    ]
    return pl.pallas_call(
        _mixproj_kernel,
        grid=(batch, nt + 1),
        in_specs=in_specs,
        out_specs=out_specs,
        out_shape=out_shape,
        compiler_params=_params(("arbitrary", "arbitrary")),
        name="mixproj",
    )(x2d, mix_pre.reshape(1, -1), w_uv.astype(BF16), w_q.astype(BF16), w_c.astype(BF16),
      w_kv.astype(BF16), w_g.astype(BF16), ln_g.reshape(1, -1), ln_b.reshape(1, -1),
      ws, bs.T, gain_a.reshape(1, -1))


def _compress_kernel(raw_ref, pea_ref, peb_ref, wa_ref, wb_ref, b1_ref, w2_ref, o_ref):
    is_v = pl.program_id(0) == 1
    nu = raw_ref.shape[0]
    u = raw_ref[...]
    a = _dot((u + pea_ref[...]).astype(BF16), wa_ref[...])
    b = _dot((u + peb_ref[...]).astype(BF16), wb_ref[...])
    hid = jax.nn.gelu(a + pltpu.roll(b, nu - 1, 0) + b1_ref[...])
    out = _dot(hid.astype(BF16), w2_ref[...])
    lane = lax.broadcasted_iota(jnp.int32, (nu, LANE), 1)
    ci = lax.broadcasted_iota(jnp.int32, (nu, LANE), 0)
    u_c = CMP_BLOCK // CMP_STRIDE
    u_s = SEL_BLOCK // CMP_STRIDE
    overlap = jnp.maximum(jnp.minimum(ci + u_c, u_s * (lane + 1)) - jnp.maximum(ci, u_s * lane), 0)
    extra = jnp.where(is_v, overlap.astype(F32), 0.0)
    ones_hi = jnp.where(jnp.logical_and(is_v, lane >= HEAD_DIM), 1.0, 0.0)
    lane_p = lax.broadcasted_iota(jnp.int32, (CMP_PAD, LANE), 1)
    pad_lo = jnp.where(jnp.logical_and(jnp.logical_not(is_v), lane_p == HEAD_DIM), MASK_BF16, 0.0)
    ones_all = jnp.where(is_v, 1.0, 0.0) + jnp.zeros((nu, LANE), F32)
    for g in range(NSA_KV):
        o_ref[g, 0:CMP_PAD, 0:LANE] = pad_lo
        o_ref[g, 0:CMP_PAD, LANE:3 * LANE] = jnp.zeros((CMP_PAD, 2 * LANE), F32)
        o_ref[g, CMP_PAD:CMP_PAD + nu, 0:LANE] = out[:, g * LANE:(g + 1) * LANE] + ones_hi
        o_ref[g, CMP_PAD:CMP_PAD + nu, LANE:2 * LANE] = extra
        o_ref[g, CMP_PAD:CMP_PAD + nu, 2 * LANE:3 * LANE] = ones_all


def _compress(cmp_raw, batch, seq, pe, w1, b1, w2):
    nu = seq // CMP_STRIDE
    half = CMP_STRIDE * HEAD_DIM
    raw = cmp_raw.reshape(2, batch, nu, CMP_STRIDE * KV_W)

    def spread(w):
        w = w.reshape(2, CMP_STRIDE, 1, HEAD_DIM, 1, CMP_HIDDEN)
        eye = jnp.eye(NSA_KV, dtype=w.dtype).reshape(1, 1, NSA_KV, 1, NSA_KV, 1)
        return (w * eye).reshape(2, CMP_STRIDE * KV_W, NSA_KV * CMP_HIDDEN)

    wa = spread(w1[:, :half])
    wb = spread(w1[:, half:])
    pe_u = jnp.broadcast_to(pe.reshape(2, 2, CMP_STRIDE, 1, HEAD_DIM),
                            (2, 2, CMP_STRIDE, NSA_KV, HEAD_DIM)).reshape(2, 2, 1, CMP_STRIDE * KV_W)
    b1t = jnp.tile(b1, (1, NSA_KV)).reshape(2, 1, NSA_KV * CMP_HIDDEN)
    w2p = jnp.pad(w2, ((0, 0), (0, 0), (0, LANE - HEAD_DIM)))
    eye = jnp.eye(NSA_KV, dtype=w2.dtype)
    w2b = (eye[None, :, None, :, None] * w2p[:, None, :, None, :]).reshape(
        2, NSA_KV * CMP_HIDDEN, NSA_KV * LANE)
    rows = CMP_PAD + nu
    kd = CMP_STRIDE * KV_W
    return pl.pallas_call(
        _compress_kernel,
        grid=(2, batch),
        in_specs=[
            pl.BlockSpec((None, None, nu, kd), lambda k, b: (k, b, 0, 0)),
            pl.BlockSpec((None, None, 1, kd), lambda k, b: (k, 0, 0, 0)),
            pl.BlockSpec((None, None, 1, kd), lambda k, b: (k, 1, 0, 0)),
            pl.BlockSpec((None, kd, NSA_KV * CMP_HIDDEN), lambda k, b: (k, 0, 0)),
            pl.BlockSpec((None, kd, NSA_KV * CMP_HIDDEN), lambda k, b: (k, 0, 0)),
            pl.BlockSpec((None, 1, NSA_KV * CMP_HIDDEN), lambda k, b: (k, 0, 0)),
            pl.BlockSpec((None, NSA_KV * CMP_HIDDEN, NSA_KV * LANE), lambda k, b: (k, 0, 0)),
        ],
        out_specs=pl.BlockSpec((None, None, NSA_KV, rows, 3 * LANE), lambda k, b: (k, b, 0, 0, 0)),
        out_shape=jax.ShapeDtypeStruct((2, batch, NSA_KV, rows, 3 * LANE), F32),
        compiler_params=_params(("arbitrary", "arbitrary")),
        name="compress",
    )(raw, pe_u, pe_u, wa.astype(BF16), wb.astype(BF16), b1t, w2b.astype(BF16))


def _t5_bucket(dist):
    n = jnp.maximum(dist, 0)
    max_exact = N_BUCKETS // 2
    nf = jnp.maximum(n, 1).astype(F32)
    large = max_exact + (jnp.log(nf * (1.0 / max_exact)) / math.log(MAX_DISTANCE / max_exact)
                         * (N_BUCKETS - max_exact)).astype(jnp.int32)
    large = jnp.minimum(large, N_BUCKETS - 1)
    return jnp.where(n < max_exact, n, large)


def _tables_kernel(rb_ref, tb_ref, tc_ref, wm_ref):
    def fill(out_ref, dist):
        bk = _t5_bucket(dist)
        for h in range(NSA_HEADS):
            far = rb_ref[N_BUCKETS - 1, h]
            val = jnp.zeros(dist.shape, F32)
            for b in range(N_BUCKETS - 1):
                val = jnp.where(bk == b, (rb_ref[b, h] - far) * LOG2E, val)
            out_ref[h] = jnp.where(dist >= 0, val, MASK_F32)

    tq = lax.broadcasted_iota(jnp.int32, (Q_BLOCK, 2 * Q_BLOCK), 0)
    j = lax.broadcasted_iota(jnp.int32, (Q_BLOCK, 2 * Q_BLOCK), 1)
    fill(tb_ref, tq + Q_BLOCK - j)
    tq = lax.broadcasted_iota(jnp.int32, (Q_BLOCK, LANE), 0)
    c = lax.broadcasted_iota(jnp.int32, (Q_BLOCK, LANE), 1)
    cmp_first = Q_BLOCK // CMP_STRIDE - LANE
    fill(tc_ref, tq - CMP_STRIDE * (cmp_first + c) - (CMP_BLOCK - 1))
    wm_ref[...] = jnp.where(c > tq, 0.0, MASK_F32)


def _bias_tables(rel_bias):
    return pl.pallas_call(
        _tables_kernel,
        in_specs=[pl.BlockSpec(memory_space=pltpu.SMEM)],
        out_shape=[jax.ShapeDtypeStruct((NSA_HEADS, Q_BLOCK, 2 * Q_BLOCK), F32),
                   jax.ShapeDtypeStruct((NSA_HEADS, Q_BLOCK, LANE), F32),
                   jax.ShapeDtypeStruct((Q_BLOCK, LANE), F32)],
        name="bias_tables",
    )(rel_bias)


def _nsa_kernel(q_ref, gt_ref, kc_ref, vc_ref, ks_ref, vs_ref, kw_ref, vw_ref,
                tb_ref, tc_ref, wm_ref, o_ref, qsel_ref, s0_ref, s1_ref, m0_ref, m1_ref,
                al0_ref, al1_ref, acc_ref, *, n_sel):
    i = pl.program_id(2)
    s_ref, m_ref, al_ref = (s0_ref, s1_ref), (m0_ref, m1_ref), (al0_ref, al1_ref)
    rows = NSA_REP * Q_BLOCK
    qa = q_ref[...].reshape(rows, LANE)
    tb = tb_ref[...].reshape(rows, 2 * Q_BLOCK)
    zero_tb = jnp.zeros((rows, KV_TILE - 2 * Q_BLOCK), F32)

    c0 = pl.multiple_of(8 * i + 8 + CMP_PAD - CMP_SPAN, 8)
    s = _dot_nt(qa, kc_ref[pl.ds(c0, CMP_SPAN), :].astype(BF16))
    s = s + jnp.concatenate([jnp.zeros((rows, CMP_SPAN - LANE), F32),
                             tc_ref[...].reshape(rows, LANE)], axis=1)
    m = jnp.maximum(jnp.max(s, axis=1, keepdims=True), MAX_CLAMP)
    acc = _dot(jnp.exp(s - m).astype(BF16), vc_ref[pl.ds(c0, CMP_SPAN), :].astype(BF16))
    inv = 1.0 / jnp.maximum(acc[:, HEAD_DIM:HEAD_DIM + 1], 1e-30)
    oc = acc[:, 0:LANE] * inv
    impu = acc[:, LANE:2 * LANE] * inv
    imp = impu[0:Q_BLOCK]
    for r in range(1, NSA_REP):
        imp = imp + impu[r * Q_BLOCK:(r + 1) * Q_BLOCK]

    t0 = pl.multiple_of(Q_BLOCK * (i + 1) + KV_PAD - KV_TILE, Q_BLOCK)
    w1 = pl.multiple_of(t0 - Q_BLOCK, Q_BLOCK)
    s0 = _dot_nt(qa, kw_ref[pl.ds(t0, KV_TILE), :]) + jnp.concatenate([zero_tb, tb], axis=1)
    s1 = _dot_nt(qa, kw_ref[pl.ds(w1, Q_BLOCK), :]) + jnp.tile(wm_ref[...], (NSA_REP, 1))
    m = jnp.maximum(jnp.max(s0, axis=1, keepdims=True), jnp.max(s1, axis=1, keepdims=True))
    acc = (_dot(jnp.exp(s0 - m).astype(BF16), vw_ref[pl.ds(t0, KV_TILE), :])
           + _dot(jnp.exp(s1 - m).astype(BF16), vw_ref[pl.ds(w1, Q_BLOCK), :]))
    ow = acc * (1.0 / acc[:, HEAD_DIM:HEAD_DIM + 1])

    imp_t = imp.T
    jb = lax.broadcasted_iota(jnp.int32, (LANE, Q_BLOCK), 0)
    t = i * Q_BLOCK + lax.broadcasted_iota(jnp.int32, (LANE, Q_BLOCK), 1)
    cur = t // SEL_BLOCK
    forced = (jb == 0) | (jb == cur) | (jb == cur - 1)
    score = jnp.where(forced, FORCE_SCORE, jnp.where(jb * SEL_BLOCK <= t, imp_t, -1.0))
    notsel = jnp.ones((LANE, Q_BLOCK), F32)
    for _ in range(n_sel):
        mx = jnp.max(score, axis=0, keepdims=True)
        first = jnp.min(jnp.where(score == mx, jb, LANE), axis=0, keepdims=True)
        hit = jb == first
        notsel = jnp.where(hit, 0.0, notsel)
        score = jnp.where(hit, -2.0, score)
    notsel_q = notsel.T.astype(BF16)
    qsel_ref[:, 0:LANE] = qa
    for r in range(NSA_REP):
        qsel_ref[r * Q_BLOCK:(r + 1) * Q_BLOCK, LANE:2 * LANE] = notsel_q
    qs = qsel_ref[...]

    n_tiles = (i + 1 + KV_TILE // Q_BLOCK - 1) // (KV_TILE // Q_BLOCK)

    def bcast(col):
        return jnp.broadcast_to(col, (rows, LANE))

    def lanes(x):
        return jnp.concatenate([x] * (KV_TILE // LANE), axis=1)

    def qk_stage(k, prev, slot):
        st = pl.multiple_of(t0 - KV_TILE * k, Q_BLOCK)
        s_n = _dot_nt(qs, ks_ref[pl.ds(st, KV_TILE), :])
        s_ref[slot][...] = s_n
        m_c = m_ref[prev][...]
        m_n = jnp.maximum(m_c, bcast(jnp.max(s_n, axis=1, keepdims=True)))
        m_ref[slot][...] = m_n
        al_ref[slot][...] = jnp.exp(m_c - m_n)

    def pv_stage(k, slot):
        st = pl.multiple_of(t0 - KV_TILE * k, Q_BLOCK)
        p = jnp.exp(s_ref[slot][...] - lanes(m_ref[slot][...]))
        acc_ref[...] = (al_ref[slot][...] * acc_ref[...]
                        + _dot(p.astype(BF16), vs_ref[pl.ds(st, KV_TILE), :]))

    s = _dot_nt(qs, ks_ref[pl.ds(t0, KV_TILE), :]) + jnp.concatenate([zero_tb, tb], axis=1)
    s_ref[0][...] = s
    m_ref[0][...] = bcast(jnp.max(s, axis=1, keepdims=True))
    al_ref[0][...] = jnp.ones((rows, LANE), F32)
    acc_ref[...] = jnp.zeros((rows, LANE), F32)

    def sel_body(j, carry):
        qk_stage(2 * j + 1, 0, 1)
        pv_stage(2 * j, 0)
        qk_stage(2 * j + 2, 1, 0)
        pv_stage(2 * j + 1, 1)
        return carry

    n_pairs = (n_tiles - 1) // 2
    lax.fori_loop(0, n_pairs, sel_body, 0)
    odd_left = n_tiles - 1 - 2 * n_pairs

    @pl.when(odd_left == 1)
    def _two_left():
        qk_stage(n_tiles - 1, 0, 1)
        pv_stage(n_tiles - 2, 0)
        pv_stage(n_tiles - 1, 1)

    @pl.when(odd_left == 0)
    def _one_left():
        pv_stage(n_tiles - 1, 0)

    acc = acc_ref[...]
    osl = acc * (1.0 / acc[:, HEAD_DIM:HEAD_DIM + 1])

    gts = gt_ref[...]
    lane = lax.broadcasted_iota(jnp.int32, (Q_BLOCK, LANE), 1)
    heads = []
    for r in range(NSA_REP):
        rs = slice(r * Q_BLOCK, (r + 1) * Q_BLOCK)
        heads.append(gts[:, r:r + 1] * oc[rs]
                     + gts[:, NSA_REP + r:NSA_REP + r + 1] * osl[rs]
                     + gts[:, 2 * NSA_REP + r:2 * NSA_REP + r + 1] * ow[rs])
    for pr in range(NSA_REP // 2):
        o_ref[:, pr * LANE:(pr + 1) * LANE] = jnp.where(
            lane < HEAD_DIM, heads[2 * pr], pltpu.roll(heads[2 * pr + 1], HEAD_DIM, 1))


def _nsa(q_aug, gates, cmp_aug, ksel, vsel, kwin, vwin, tb, tc, wm, batch, seq):
    nq = seq // Q_BLOCK
    sp = seq + KV_PAD
    crow = cmp_aug.shape[3]
    n_sel = min(N_SELECT, seq // SEL_BLOCK)
    assert seq // SEL_BLOCK <= LANE and KV_PAD >= KV_TILE and KV_PAD >= WINDOW
    assert seq // CMP_STRIDE <= CMP_SPAN <= CMP_PAD and crow == CMP_PAD + seq // CMP_STRIDE
    kv_spec = lambda w: pl.BlockSpec((None, None, sp, w), lambda b, g, i: (b, g, 0, 0))
    in_specs = [
        pl.BlockSpec((None, NSA_REP, Q_BLOCK, LANE), lambda b, g, i: (b, g, i, 0)),
        pl.BlockSpec((Q_BLOCK, LANE), lambda b, g, i: (b * nq + i, g)),
        pl.BlockSpec((None, None, None, crow, LANE), lambda b, g, i: (0, b, g, 0, 0)),
        pl.BlockSpec((None, None, None, crow, 2 * LANE), lambda b, g, i: (1, b, g, 0, 0)),
        kv_spec(2 * LANE), kv_spec(LANE), kv_spec(LANE), kv_spec(LANE),
        pl.BlockSpec((NSA_REP, Q_BLOCK, 2 * Q_BLOCK), lambda b, g, i: (g, 0, 0)),
        pl.BlockSpec((NSA_REP, Q_BLOCK, LANE), lambda b, g, i: (g, 0, 0)),
        pl.BlockSpec((Q_BLOCK, LANE), lambda b, g, i: (0, 0)),
    ]
    rows = NSA_REP * Q_BLOCK
    return pl.pallas_call(
        functools.partial(_nsa_kernel, n_sel=n_sel),
        grid=(batch, NSA_KV, nq),
        in_specs=in_specs,
        out_specs=pl.BlockSpec((Q_BLOCK, NSA_REP * HEAD_DIM), lambda b, g, i: (b * nq + i, g)),
        out_shape=jax.ShapeDtypeStruct((batch * seq, NSA_WIDTH), F32),
        scratch_shapes=[pltpu.VMEM((rows, 2 * LANE), BF16),
                        pltpu.VMEM((rows, KV_TILE), F32),
                        pltpu.VMEM((rows, KV_TILE), F32),
                        pltpu.VMEM((rows, LANE), F32),
                        pltpu.VMEM((rows, LANE), F32),
                        pltpu.VMEM((rows, LANE), F32),
                        pltpu.VMEM((rows, LANE), F32),
                        pltpu.VMEM((rows, LANE), F32)],
        compiler_params=_params(("arbitrary", "arbitrary", "arbitrary")),
        name="nsa",
    )(q_aug, gates, cmp_aug, cmp_aug, ksel, vsel, kwin, vwin, tb, tc, wm)


_N_GROUP_SCRATCH = 8


def _nsa2_kernel(q_ref, gt_ref, kc_ref, vc_ref, ks_ref, vs_ref, kw_ref, vw_ref,
                 tb_ref, tc_ref, wm_ref, gx_ref, o_ref, *scratch, n_sel):
    i = pl.program_id(1)
    rows = NSA_REP * Q_BLOCK
    per_group = [scratch[g * _N_GROUP_SCRATCH:(g + 1) * _N_GROUP_SCRATCH] for g in range(NSA_KV)]
    qsel_ref = [s[0] for s in per_group]
    s_ref = [(s[1], s[2]) for s in per_group]
    m_ref = [(s[3], s[4]) for s in per_group]
    al_ref = [(s[5], s[6]) for s in per_group]
    acc_ref = [s[7] for s in per_group]

    zero_tb = jnp.zeros((rows, KV_TILE - 2 * Q_BLOCK), F32)
    c0 = pl.multiple_of(8 * i + 8 + CMP_PAD - CMP_SPAN, 8)
    t0 = pl.multiple_of(Q_BLOCK * (i + 1) + KV_PAD - KV_TILE, Q_BLOCK)
    w1 = pl.multiple_of(t0 - Q_BLOCK, Q_BLOCK)
    n_tiles = (i + 1 + KV_TILE // Q_BLOCK - 1) // (KV_TILE // Q_BLOCK)
    lane = lax.broadcasted_iota(jnp.int32, (Q_BLOCK, LANE), 1)

    def bcast(col):
        return jnp.broadcast_to(col, (rows, LANE))

    def lanes(x, n):
        return jnp.concatenate([x] * n, axis=1)

    def head_out(acc, r):
        rs = slice(r * Q_BLOCK, (r + 1) * Q_BLOCK)
        a, b = acc[rs, 0:LANE], acc[rs, LANE:2 * LANE]
        return a * (1.0 / b) if r % 2 == 0 else b * (1.0 / a)

    def front(g):
        hs = slice(g * NSA_REP, (g + 1) * NSA_REP)
        qa = q_ref[hs].reshape(rows, LANE)
        tile0_bias = jnp.concatenate([zero_tb, tb_ref[hs].reshape(rows, 2 * Q_BLOCK)], axis=1)

        s = _dot_nt(qa, kc_ref[g, pl.ds(c0, CMP_SPAN), :].astype(BF16))
        s = s + jnp.concatenate([jnp.zeros((rows, CMP_SPAN - LANE), F32),
                                 tc_ref[hs].reshape(rows, LANE)], axis=1)
        m = jnp.maximum(jnp.max(s, axis=1, keepdims=True), MAX_CLAMP)
        acc = _dot(jnp.exp2(s - m).astype(BF16), vc_ref[g, pl.ds(c0, CMP_SPAN), :].astype(BF16))
        inv = 1.0 / jnp.maximum(acc[:, 2 * LANE:3 * LANE], 1e-30)
        oc = acc[:, 0:LANE] * inv
        impu = acc[:, LANE:2 * LANE] * inv
        imp = impu[0:Q_BLOCK]
        for r in range(1, NSA_REP):
            imp = imp + impu[r * Q_BLOCK:(r + 1) * Q_BLOCK]

        s0 = _dot_nt(qa, kw_ref[g, pl.ds(t0, KV_TILE), :]) + tile0_bias
        s1 = _dot_nt(qa, kw_ref[g, pl.ds(w1, Q_BLOCK), :]) + jnp.tile(wm_ref[...], (NSA_REP, 1))
        m = jnp.maximum(jnp.max(s0, axis=1, keepdims=True), jnp.max(s1, axis=1, keepdims=True))
        accw = (_dot(jnp.exp2(s0 - m).astype(BF16), vw_ref[g, pl.ds(t0, KV_TILE), :])
                + _dot(jnp.exp2(s1 - m).astype(BF16), vw_ref[g, pl.ds(w1, Q_BLOCK), :]))
        ow = [head_out(accw, r) for r in range(NSA_REP)]

        imp_t = imp.T
        jb = lax.broadcasted_iota(jnp.int32, (LANE, Q_BLOCK), 0)
        t = i * Q_BLOCK + lax.broadcasted_iota(jnp.int32, (LANE, Q_BLOCK), 1)
        cur = t // SEL_BLOCK
        forced = (jb == 0) | (jb == cur) | (jb == cur - 1)
        score = jnp.where(forced, FORCE_SCORE, jnp.where(jb * SEL_BLOCK <= t, imp_t, -1.0))
        notsel = jnp.ones((LANE, Q_BLOCK), F32)
        for _ in range(n_sel):
            mx = jnp.max(score, axis=0, keepdims=True)
            first = jnp.min(jnp.where(score == mx, jb, LANE), axis=0, keepdims=True)
            hit = jb == first
            notsel = jnp.where(hit, 0.0, notsel)
            score = jnp.where(hit, -2.0, score)
        notsel_q = notsel.T.astype(BF16)
        qsel_ref[g][:, 0:LANE] = qa
        for r in range(NSA_REP):
            qsel_ref[g][r * Q_BLOCK:(r + 1) * Q_BLOCK, LANE:2 * LANE] = notsel_q

        s = _dot_nt(qsel_ref[g][...], ks_ref[g, pl.ds(t0, KV_TILE), :]) + tile0_bias
        s_ref[g][0][...] = s
        m_ref[g][0][...] = bcast(jnp.max(s, axis=1, keepdims=True))
        al_ref[g][0][...] = jnp.ones((rows, LANE), F32)
        acc_ref[g][...] = jnp.zeros((rows, 2 * LANE), F32)
        return oc, ow

    def qk_stage(g, k, prev, slot):
        st = pl.multiple_of(t0 - KV_TILE * k, Q_BLOCK)
        s_n = _dot_nt(qsel_ref[g][...], ks_ref[g, pl.ds(st, KV_TILE), :])
        s_ref[g][slot][...] = s_n
        m_c = m_ref[g][prev][...]
        m_n = jnp.maximum(m_c, bcast(jnp.max(s_n, axis=1, keepdims=True)))
        m_ref[g][slot][...] = m_n
        al_ref[g][slot][...] = jnp.exp2(m_c - m_n)

    def pv_stage(g, k, slot):
        st = pl.multiple_of(t0 - KV_TILE * k, Q_BLOCK)
        p = jnp.exp2(s_ref[g][slot][...] - lanes(m_ref[g][slot][...], KV_TILE // LANE))
        acc_ref[g][...] = (lanes(al_ref[g][slot][...], 2) * acc_ref[g][...]
                           + _dot(p.astype(BF16), vs_ref[g, pl.ds(st, KV_TILE), :]))

    groups = range(NSA_KV)
    fronts = [front(g) for g in groups]

    def sel_body(j, carry):
        for g in groups:
            qk_stage(g, 2 * j + 1, 0, 1)
        for g in groups:
            pv_stage(g, 2 * j, 0)
        for g in groups:
            qk_stage(g, 2 * j + 2, 1, 0)
        for g in groups:
            pv_stage(g, 2 * j + 1, 1)
        return carry

    n_pairs = (n_tiles - 1) // 2
    lax.fori_loop(0, n_pairs, sel_body, 0)
    odd_left = n_tiles - 1 - 2 * n_pairs

    @pl.when(odd_left == 1)
    def _two_left():
        for g in groups:
            qk_stage(g, n_tiles - 1, 0, 1)
        for g in groups:
            pv_stage(g, n_tiles - 2, 0)
        for g in groups:
            pv_stage(g, n_tiles - 1, 1)

    @pl.when(odd_left == 0)
    def _one_left():
        for g in groups:
            pv_stage(g, n_tiles - 1, 0)

    for g in groups:
        oc, ow = fronts[g]
        acc = acc_ref[g][...]
        gx = _dot(gt_ref[:, 2 * g * LANE:(2 * g + 2) * LANE], gx_ref[...])

        def gate(branch, r):
            c = branch * NSA_REP + r
            return gx[:, c * LANE:(c + 1) * LANE]

        heads = []
        for r in range(NSA_REP):
            rs = slice(r * Q_BLOCK, (r + 1) * Q_BLOCK)
            oc_r = oc[rs] if r % 2 == 0 else pltpu.roll(oc[rs], HEAD_DIM, 1)
            heads.append(gate(0, r) * oc_r + gate(1, r) * head_out(acc, r) + gate(2, r) * ow[r])
        for pr in range(NSA_REP // 2):
            col = (g * NSA_REP // 2 + pr) * LANE
            o_ref[:, col:col + LANE] = jnp.where(lane < HEAD_DIM, heads[2 * pr], heads[2 * pr + 1])


def _nsa2(q_aug, gates, cmp_aug, ksel, vsel, kwin, vwin, tb, tc, wm, batch, seq):
    nq = seq // Q_BLOCK
    sp = seq + KV_PAD
    crow = cmp_aug.shape[3]
    n_sel = min(N_SELECT, seq // SEL_BLOCK)
    assert seq // SEL_BLOCK <= LANE and KV_PAD >= KV_TILE and KV_PAD >= WINDOW
    assert seq // CMP_STRIDE <= CMP_SPAN <= CMP_PAD and crow == CMP_PAD + seq // CMP_STRIDE
    n_gate = 3 * NSA_REP
    col = jnp.arange(n_gate * LANE) // LANE
    row = jnp.arange(2 * LANE) % LANE
    gate_spread = (row[:, None] == col[None, :]).astype(BF16)
    once = pl.Buffered(1)
    kv_spec = lambda w: pl.BlockSpec((None, NSA_KV, sp, w), lambda b, i: (b, 0, 0, 0), pipeline_mode=once)
    cmp_spec = lambda k, w: pl.BlockSpec((None, None, NSA_KV, crow, w), lambda b, i: (k, b, 0, 0, 0),
                                         pipeline_mode=once)
    in_specs = [
        pl.BlockSpec((None, NSA_HEADS, Q_BLOCK, LANE), lambda b, i: (b, 0, i, 0)),
        pl.BlockSpec((Q_BLOCK, 2 * NSA_KV * LANE), lambda b, i: (b * nq + i, 0)),
        cmp_spec(0, LANE), cmp_spec(1, 3 * LANE),
        kv_spec(2 * LANE), kv_spec(2 * LANE), kv_spec(LANE), kv_spec(2 * LANE),
        _const_spec((NSA_HEADS, Q_BLOCK, 2 * Q_BLOCK)),
        _const_spec((NSA_HEADS, Q_BLOCK, LANE)),
        _const_spec((Q_BLOCK, LANE)),
        _const_spec((2 * LANE, n_gate * LANE)),
    ]
    rows = NSA_REP * Q_BLOCK
    group_scratch = [pltpu.VMEM((rows, 2 * LANE), BF16),
                     pltpu.VMEM((rows, KV_TILE), F32),
                     pltpu.VMEM((rows, KV_TILE), F32),
                     pltpu.VMEM((rows, LANE), F32),
                     pltpu.VMEM((rows, LANE), F32),
                     pltpu.VMEM((rows, LANE), F32),
                     pltpu.VMEM((rows, LANE), F32),
                     pltpu.VMEM((rows, 2 * LANE), F32)]
    assert len(group_scratch) == _N_GROUP_SCRATCH
    return pl.pallas_call(
        functools.partial(_nsa2_kernel, n_sel=n_sel),
        grid=(batch, nq),
        in_specs=in_specs,
        out_specs=pl.BlockSpec((Q_BLOCK, NSA_WIDTH), lambda b, i: (b * nq + i, 0)),
        out_shape=jax.ShapeDtypeStruct((batch * seq, NSA_WIDTH), F32),
        scratch_shapes=group_scratch * NSA_KV,
        compiler_params=_params(("arbitrary", "arbitrary")),
        name="nsa",
    )(q_aug, gates, cmp_aug, cmp_aug, ksel, vsel, kwin, vwin, tb, tc, wm, gate_spread)


def _memkv_kernel(mem_ref, g_ref, wk_ref, wv_ref, k_ref, v_ref):
    h = _rms(mem_ref[...], g_ref[...]).astype(BF16)
    k_ref[...] = _dot(h, wk_ref[...]).astype(BF16)
    v_ref[...] = _dot(h, wv_ref[...]).astype(BF16)


def _mem_kv(mem, g_mem, wkv):
    batch, m, _ = mem.shape
    blk = pl.BlockSpec((None, m, D_MODEL), lambda b: (b, 0, 0))
    return pl.pallas_call(
        _memkv_kernel,
        grid=(batch,),
        in_specs=[blk, _const_spec((1, D_MODEL)), _const_spec((D_MODEL, D_MODEL)),
                  _const_spec((D_MODEL, D_MODEL))],
        out_specs=[blk, blk],
        out_shape=[jax.ShapeDtypeStruct((batch, m, D_MODEL), BF16)] * 2,
        compiler_params=_params(("arbitrary",)),
        name="mem_kv",
    )(mem, g_mem.reshape(1, -1), wkv[:, :D_MODEL].astype(BF16), wkv[:, D_MODEL:].astype(BF16))


def _mixout_kernel(x_ref, ya_ref, yb_ref, gb_ref, wo_ref, post_ref, xpre_ref, wq_ref,
                   km_ref, vm_ref, wxo_ref, xpost_ref, o_ref):
    yb = _rms(yb_ref[...], gb_ref[...]).astype(BF16)
    y = _dot(ya_ref[...].astype(BF16), wo_ref[0:GM_WIDTH, :]) + _dot(yb, wo_ref[GM_WIDTH:, :])
    x1 = x_ref[...] + _rms(y, post_ref[...])
    h = _rms(x1, xpre_ref[...]).astype(BF16)
    q = _dot(h, wq_ref[...]).astype(BF16)
    outs = []
    for hd in range(XA_HEADS):
        hs = slice(hd * XA_HEAD_DIM, (hd + 1) * XA_HEAD_DIM)
        s = _dot_nt(q[:, hs], km_ref[:, hs])
        e = jnp.exp(s - jnp.max(s, axis=1, keepdims=True))
        o = _dot(e.astype(BF16), vm_ref[:, hs]) / jnp.sum(e, axis=1, keepdims=True)
        outs.append(o.astype(BF16))
    o = _dot(jnp.concatenate(outs, axis=1), wxo_ref[...])
    o_ref[...] = x1 + _rms(o, xpost_ref[...])


def _mix_out(x2d, ya, yb, batch, seq, gain_b, w_out, mix_post, xa_pre, wq, kmem, vmem, wo, xa_post):
    tm = TOKEN_TILE
    nt = seq // tm
    row = lambda w: pl.BlockSpec((tm, w), lambda b, t: (b * nt + t, 0))
    mem_spec = pl.BlockSpec((None, kmem.shape[1], D_MODEL), lambda b, t: (b, 0, 0))
    sq = _const_spec((D_MODEL, D_MODEL))
    vec = _const_spec((1, D_MODEL))
    return pl.pallas_call(
        _mixout_kernel,
        grid=(batch, nt),
        in_specs=[row(D_MODEL), row(GM_WIDTH), row(NSA_WIDTH), _const_spec((1, NSA_WIDTH)), sq, vec,
                  vec, sq, mem_spec, mem_spec, sq, vec],
        out_specs=row(D_MODEL),
        out_shape=jax.ShapeDtypeStruct(x2d.shape, F32),
        compiler_params=_params(("arbitrary", "arbitrary")),
        name="mix_out",
    )(x2d, ya, yb, gain_b.reshape(1, -1), w_out.astype(BF16), mix_post.reshape(1, -1),
      xa_pre.reshape(1, -1), (wq * XA_HEAD_DIM ** -0.5).astype(BF16), kmem, vmem,
      wo.astype(BF16), xa_post.reshape(1, -1))


def kernel(x, mem, ffn1_pre, ffn1_post, ffn1_wg, ffn1_wu, ffn1_wd, mix_pre, mix_post, w_in, gm_ln_g, gm_ln_b, gm_ws, gm_bs, ck_pe, ck_w1, ck_b1, ck_w2, cv_pe, cv_w1, cv_b1, cv_w2, rel_bias, out_gain_a, out_gain_b, w_out, xa_pre, xa_post, mem_norm, xa_wq, xa_wkv, xa_wo, ffn2_pre, ffn2_post, ffn2_wg, ffn2_wu, ffn2_wd):
    batch, seq, _ = x.shape
    depth = ffn1_pre.shape[0]
    x2d = x.reshape(batch * seq, D_MODEL)
    tb, tc, wm = _bias_tables(rel_bias)
    for l in range(depth):
        x2d = _ffn(x2d, ffn1_pre[l], ffn1_wg[l], ffn1_wu[l], ffn1_wd[l], ffn1_post[l])
        ya, q_aug, cmp_raw, ksel, vsel, kwin, vwin, gates = _mixproj(
            x2d, batch, seq, mix_pre[l], w_in[l], gm_ln_g[l], gm_ln_b[l], gm_ws[l], gm_bs[l],
            out_gain_a[l])
        cmp_aug = _compress(cmp_raw, batch, seq,
                            jnp.stack([ck_pe[l], cv_pe[l]]), jnp.stack([ck_w1[l], cv_w1[l]]),
                            jnp.stack([ck_b1[l], cv_b1[l]]), jnp.stack([ck_w2[l], cv_w2[l]]))
        yb = _nsa2(q_aug, gates, cmp_aug, ksel, vsel, kwin, vwin, tb, tc, wm, batch, seq)
        kmem, vmem = _mem_kv(mem, mem_norm[l], xa_wkv[l])
        x2d = _mix_out(x2d, ya, yb, batch, seq, out_gain_b[l], w_out[l], mix_post[l], xa_pre[l],
                       xa_wq[l], kmem, vmem, xa_wo[l], xa_post[l])
        x2d = _ffn(x2d, ffn2_pre[l], ffn2_wg[l], ffn2_wu[l], ffn2_wd[l], ffn2_post[l])
    return x2d.reshape(batch, seq, D_MODEL)
```

```python
import functools
import math

import jax
import jax.numpy as jnp
from jax import lax
from jax.experimental import pallas as pl
from jax.experimental.pallas import tpu as pltpu

F32 = jnp.float32
BF16 = jnp.bfloat16

D_MODEL = 1024
MEM_LEN = 256
EPS = 1e-6
D_FF = 2816
GM_WIDTH = 512
GM_GROUPS = 4
GM_CH = GM_WIDTH // GM_GROUPS
CHUNK = 128
NSA_WIDTH = D_MODEL - GM_WIDTH
HEAD_DIM = 64
NSA_HEADS = NSA_WIDTH // HEAD_DIM
NSA_KV = 2
NSA_REP = NSA_HEADS // NSA_KV
KV_W = NSA_KV * HEAD_DIM
CMP_BLOCK = 32
CMP_STRIDE = 16
CMP_HIDDEN = 256
SEL_BLOCK = 64
N_SELECT = 16
WINDOW = 512
Q_BLOCK = 128
N_BUCKETS = 32
MAX_DISTANCE = 128
XA_HEADS = 4
XA_HEAD_DIM = D_MODEL // XA_HEADS
FORCE_SCORE = 1e4

LANE = 128
TOKEN_TILE = 512
FF_CHUNK = 256
KV_PAD = 512
KV_TILE = 512
CMP_PAD = 512
CMP_SPAN = 512
MASK_BF16 = -(2.0 ** 100)
MASK_F32 = -1e30
MAX_CLAMP = -(2.0 ** 99)
LOG2E = math.log2(math.e)
VMEM_LIMIT = 56 * 1024 * 1024

_NT = (((1,), (1,)), ((), ()))


def _rms(x, g):
    ms = jnp.mean(x * x, axis=-1, keepdims=True)
    return x * lax.rsqrt(ms + EPS) * g


def _dot(a, b):
    return jnp.dot(a, b, preferred_element_type=F32)


def _dot_nt(a, b):
    return lax.dot_general(a, b, _NT, preferred_element_type=F32)


def _params(sem, vmem=VMEM_LIMIT):
    return pltpu.CompilerParams(dimension_semantics=sem, vmem_limit_bytes=vmem)


def _const_spec(shape):
    nd = len(shape)
    return pl.BlockSpec(shape, lambda *_: (0,) * nd)


def _ffn_kernel(x_ref, pre_ref, wg_ref, wu_ref, wd_ref, post_ref, o_ref, acc_ref):
    x = x_ref[...]
    h = _rms(x, pre_ref[...]).astype(BF16)
    for c in range(D_FF // FF_CHUNK):
        sl = slice(c * FF_CHUNK, (c + 1) * FF_CHUNK)
        g = _dot(h, wg_ref[:, sl])
        u = _dot(h, wu_ref[:, sl])
        a = (jax.nn.silu(g) * u).astype(BF16)
        d = _dot(a, wd_ref[sl, :])
        if c == 0:
            acc_ref[...] = d
        else:
            acc_ref[...] += d
    o_ref[...] = x + 0.5 * _rms(acc_ref[...], post_ref[...])


def _ffn(x2d, pre, wg, wu, wd, post):
    n = x2d.shape[0]
    tm = TOKEN_TILE
    row = pl.BlockSpec((tm, D_MODEL), lambda t: (t, 0))
    return pl.pallas_call(
        _ffn_kernel,
        grid=(n // tm,),
        in_specs=[row, _const_spec((1, D_MODEL)), _const_spec((D_MODEL, D_FF)),
                  _const_spec((D_MODEL, D_FF)), _const_spec((D_FF, D_MODEL)),
                  _const_spec((1, D_MODEL))],
        out_specs=row,
        out_shape=jax.ShapeDtypeStruct((n, D_MODEL), F32),
        scratch_shapes=[pltpu.VMEM((tm, D_MODEL), F32)],
        compiler_params=_params(("arbitrary",)),
        name="ffn",
    )(x2d, pre.reshape(1, -1), wg.astype(BF16), wu.astype(BF16), wd.astype(BF16),
      post.reshape(1, -1))


def _mixproj_kernel(x_ref, pre_ref, wuv_ref, wq_ref, wc_ref, wkv_ref, wgt_ref,
                    lng_ref, lnb_ref, ws_ref, bs_ref, ga_ref,
                    ya_ref, q_ref, cmp_ref, ksel_ref, vsel_ref, kwin_ref, vwin_ref, gt_ref):
    t = pl.program_id(1)
    tm = x_ref.shape[0]
    lane = lax.broadcasted_iota(jnp.int32, (tm, LANE), 1)
    flag_col = jnp.where(lane == HEAD_DIM, MASK_BF16, 0.0).astype(BF16)

    @pl.when(t == 0)
    def _pad_rows():
        for g in range(NSA_KV):
            ksel_ref[g, :, 0:LANE] = flag_col
            ksel_ref[g, :, LANE:2 * LANE] = jnp.zeros((tm, LANE), BF16)
            kwin_ref[g] = flag_col
            vsel_ref[g] = jnp.zeros((tm, 2 * LANE), BF16)
            vwin_ref[g] = jnp.zeros((tm, 2 * LANE), BF16)

    @pl.when(t > 0)
    def _tile():
        h = _rms(x_ref[...], pre_ref[...]).astype(BF16)
        u = jax.nn.gelu(_dot(h, wuv_ref[:, 0:GM_WIDTH]))
        v = jax.nn.gelu(_dot(h, wuv_ref[:, GM_WIDTH:2 * GM_WIDTH]))
        r_i = lax.broadcasted_iota(jnp.int32, (CHUNK, CHUNK), 0)
        c_i = lax.broadcasted_iota(jnp.int32, (CHUNK, CHUNK), 1)
        for g in range(GM_GROUPS):
            gs = slice(g * GM_CH, (g + 1) * GM_CH)
            vg = v[:, gs]
            mu = jnp.mean(vg, axis=-1, keepdims=True)
            var = jnp.mean(jnp.square(vg - mu), axis=-1, keepdims=True)
            vn = ((vg - mu) * lax.rsqrt(var + EPS) * lng_ref[:, gs] + lnb_ref[:, gs]).astype(BF16)
            w = jnp.where(c_i <= r_i, ws_ref[g], 0.0).astype(BF16)
            for n in range(tm // CHUNK):
                rs = slice(n * CHUNK, (n + 1) * CHUNK)
                s = _dot(w, vn[rs]) + bs_ref[:, g:g + 1]
                ya_ref[rs, gs] = u[rs, gs] * s
        ya_ref[...] = _rms(ya_ref[...], ga_ref[...])
        one_col = jnp.where(lane == HEAD_DIM, 1.0, 0.0)
        zq = _dot(h, wq_ref[...])
        for hd in range(NSA_HEADS):
            q_ref[hd] = (zq[:, hd * LANE:(hd + 1) * LANE] + one_col).astype(BF16)
        zc = _dot(h, wc_ref[...])
        cmp_ref[0] = zc[:, 0:KV_W]
        cmp_ref[1] = zc[:, KV_W:2 * KV_W]
        zkv = _dot(h, wkv_ref[...])
        ones_hi = jnp.where(lane >= HEAD_DIM, 1.0, 0.0)
        ones_lo = 1.0 - ones_hi

        def v_aug(v):
            return jnp.concatenate([v + ones_hi, pltpu.roll(v, HEAD_DIM, 1) + ones_lo],
                                   axis=1).astype(BF16)

        pos_blk = ((t - 1) * tm + lax.broadcasted_iota(jnp.int32, (tm, LANE), 0)) // SEL_BLOCK
        blk_onehot = jnp.where(lane == pos_blk, MASK_BF16, 0.0).astype(BF16)
        for g in range(NSA_KV):
            ksel_ref[g, :, 0:LANE] = zkv[:, g * LANE:(g + 1) * LANE].astype(BF16)
            ksel_ref[g, :, LANE:2 * LANE] = blk_onehot
            vsel_ref[g] = v_aug(zkv[:, (2 + g) * LANE:(3 + g) * LANE])
            kwin_ref[g] = zkv[:, (4 + g) * LANE:(5 + g) * LANE].astype(BF16)
            vwin_ref[g] = v_aug(zkv[:, (6 + g) * LANE:(7 + g) * LANE])
        sg = jax.nn.sigmoid(_dot(h, wgt_ref[...]))
        hi = sg.astype(BF16)
        lo = (sg - hi.astype(F32)).astype(BF16)
        for g in range(NSA_KV):
            gt_ref[:, 2 * g * LANE:(2 * g + 1) * LANE] = hi[:, g * LANE:(g + 1) * LANE]
            gt_ref[:, (2 * g + 1) * LANE:(2 * g + 2) * LANE] = lo[:, g * LANE:(g + 1) * LANE]


def _mixproj(x2d, batch, seq, mix_pre, w_in, ln_g, ln_b, ws, bs, gain_a):
    tm = TOKEN_TILE
    nt = seq // tm
    assert seq % tm == 0 and KV_PAD == tm
    scale = HEAD_DIM ** -0.5 * LOG2E
    o_q = 2 * GM_WIDTH
    o_kv = o_q + NSA_WIDTH
    w_uv = w_in[:, :o_q]
    w_q = (w_in[:, o_q:o_kv] * scale).reshape(D_MODEL, NSA_HEADS, HEAD_DIM)
    w_q = jnp.pad(w_q, ((0, 0), (0, 0), (0, LANE - HEAD_DIM))).reshape(D_MODEL, NSA_HEADS * LANE)
    w_c = w_in[:, o_kv:o_kv + 2 * KV_W]
    w_kv = w_in[:, o_kv + 2 * KV_W:o_kv + 6 * KV_W].reshape(D_MODEL, 4 * NSA_KV, HEAD_DIM)
    w_kv = jnp.pad(w_kv, ((0, 0), (0, 0), (0, LANE - HEAD_DIM))).reshape(D_MODEL, 4 * NSA_KV * LANE)
    w_g = w_in[:, o_kv + 6 * KV_W:].reshape(D_MODEL, NSA_KV, NSA_REP, 3).transpose(0, 1, 3, 2)
    w_g = jnp.pad(w_g.reshape(D_MODEL, NSA_KV, 3 * NSA_REP),
                  ((0, 0), (0, 0), (0, LANE - 3 * NSA_REP))).reshape(D_MODEL, NSA_KV * LANE)

    def tile(b, t):
        return b * nt + jnp.maximum(t - 1, 0)

    n = batch * seq
    sp = seq + KV_PAD
    out_shape = [
        jax.ShapeDtypeStruct((n, GM_WIDTH), F32),
        jax.ShapeDtypeStruct((batch, NSA_HEADS, seq, LANE), BF16),
        jax.ShapeDtypeStruct((2, n, KV_W), F32),
        jax.ShapeDtypeStruct((batch, NSA_KV, sp, 2 * LANE), BF16),
        jax.ShapeDtypeStruct((batch, NSA_KV, sp, 2 * LANE), BF16),
        jax.ShapeDtypeStruct((batch, NSA_KV, sp, LANE), BF16),
        jax.ShapeDtypeStruct((batch, NSA_KV, sp, 2 * LANE), BF16),
        jax.ShapeDtypeStruct((n, 2 * NSA_KV * LANE), BF16),
    ]
    padded = lambda w: pl.BlockSpec((None, NSA_KV, tm, w), lambda b, t: (b, 0, t, 0))
    out_specs = [
        pl.BlockSpec((tm, GM_WIDTH), lambda b, t: (tile(b, t), 0)),
        pl.BlockSpec((None, NSA_HEADS, tm, LANE), lambda b, t: (b, 0, jnp.maximum(t - 1, 0), 0)),
        pl.BlockSpec((2, tm, KV_W), lambda b, t: (0, tile(b, t), 0)),
        padded(2 * LANE), padded(2 * LANE), padded(LANE), padded(2 * LANE),
        pl.BlockSpec((tm, 2 * NSA_KV * LANE), lambda b, t: (tile(b, t), 0)),
    ]
    in_specs = [
        pl.BlockSpec((tm, D_MODEL), lambda b, t: (tile(b, t), 0)),
        _const_spec((1, D_MODEL)),
        _const_spec(w_uv.shape), _const_spec(w_q.shape), _const_spec(w_c.shape),
        _const_spec(w_kv.shape), _const_spec(w_g.shape),
        _const_spec((1, GM_WIDTH)), _const_spec((1, GM_WIDTH)),
        _const_spec((GM_GROUPS, CHUNK, CHUNK)), _const_spec((CHUNK, GM_GROUPS)),
        _const_spec((1, GM_WIDTH)),
    ]
    return pl.pallas_call(
        _mixproj_kernel,
        grid=(batch, nt + 1),
        in_specs=in_specs,
        out_specs=out_specs,
        out_shape=out_shape,
        compiler_params=_params(("arbitrary", "arbitrary")),
        name="mixproj",
    )(x2d, mix_pre.reshape(1, -1), w_uv.astype(BF16), w_q.astype(BF16), w_c.astype(BF16),
      w_kv.astype(BF16), w_g.astype(BF16), ln_g.reshape(1, -1), ln_b.reshape(1, -1),
      ws, bs.T, gain_a.reshape(1, -1))


def _compress_kernel(raw_ref, pea_ref, peb_ref, wa_ref, wb_ref, b1_ref, w2_ref, o_ref):
    is_v = pl.program_id(0) == 1
    nu = raw_ref.shape[0]
    u = raw_ref[...]
    a = _dot((u + pea_ref[...]).astype(BF16), wa_ref[...])
    b = _dot((u + peb_ref[...]).astype(BF16), wb_ref[...])
    hid = jax.nn.gelu(a + pltpu.roll(b, nu - 1, 0) + b1_ref[...])
    out = _dot(hid.astype(BF16), w2_ref[...])
    lane = lax.broadcasted_iota(jnp.int32, (nu, LANE), 1)
    ci = lax.broadcasted_iota(jnp.int32, (nu, LANE), 0)
    u_c = CMP_BLOCK // CMP_STRIDE
    u_s = SEL_BLOCK // CMP_STRIDE
    overlap = jnp.maximum(jnp.minimum(ci + u_c, u_s * (lane + 1)) - jnp.maximum(ci, u_s * lane), 0)
    extra = jnp.where(is_v, overlap.astype(F32), 0.0)
    ones_hi = jnp.where(jnp.logical_and(is_v, lane >= HEAD_DIM), 1.0, 0.0)
    lane_p = lax.broadcasted_iota(jnp.int32, (CMP_PAD, LANE), 1)
    pad_lo = jnp.where(jnp.logical_and(jnp.logical_not(is_v), lane_p == HEAD_DIM), MASK_BF16, 0.0)
    for g in range(NSA_KV):
        o_ref[g, 0:CMP_PAD, 0:LANE] = pad_lo
        o_ref[g, 0:CMP_PAD, LANE:2 * LANE] = jnp.zeros((CMP_PAD, LANE), F32)
        o_ref[g, CMP_PAD:CMP_PAD + nu, 0:LANE] = out[:, g * LANE:(g + 1) * LANE] + ones_hi
        o_ref[g, CMP_PAD:CMP_PAD + nu, LANE:2 * LANE] = extra


def _compress(cmp_raw, batch, seq, pe, w1, b1, w2):
    nu = seq // CMP_STRIDE
    half = CMP_STRIDE * HEAD_DIM
    raw = cmp_raw.reshape(2, batch, nu, CMP_STRIDE * KV_W)

    def spread(w):
        w = w.reshape(2, CMP_STRIDE, 1, HEAD_DIM, 1, CMP_HIDDEN)
        eye = jnp.eye(NSA_KV, dtype=w.dtype).reshape(1, 1, NSA_KV, 1, NSA_KV, 1)
        return (w * eye).reshape(2, CMP_STRIDE * KV_W, NSA_KV * CMP_HIDDEN)

    wa = spread(w1[:, :half])
    wb = spread(w1[:, half:])
    pe_u = jnp.broadcast_to(pe.reshape(2, 2, CMP_STRIDE, 1, HEAD_DIM),
                            (2, 2, CMP_STRIDE, NSA_KV, HEAD_DIM)).reshape(2, 2, 1, CMP_STRIDE * KV_W)
    b1t = jnp.tile(b1, (1, NSA_KV)).reshape(2, 1, NSA_KV * CMP_HIDDEN)
    w2p = jnp.pad(w2, ((0, 0), (0, 0), (0, LANE - HEAD_DIM)))
    eye = jnp.eye(NSA_KV, dtype=w2.dtype)
    w2b = (eye[None, :, None, :, None] * w2p[:, None, :, None, :]).reshape(
        2, NSA_KV * CMP_HIDDEN, NSA_KV * LANE)
    rows = CMP_PAD + nu
    kd = CMP_STRIDE * KV_W
    return pl.pallas_call(
        _compress_kernel,
        grid=(2, batch),
        in_specs=[
            pl.BlockSpec((None, None, nu, kd), lambda k, b: (k, b, 0, 0)),
            pl.BlockSpec((None, None, 1, kd), lambda k, b: (k, 0, 0, 0)),
            pl.BlockSpec((None, None, 1, kd), lambda k, b: (k, 1, 0, 0)),
            pl.BlockSpec((None, kd, NSA_KV * CMP_HIDDEN), lambda k, b: (k, 0, 0)),
            pl.BlockSpec((None, kd, NSA_KV * CMP_HIDDEN), lambda k, b: (k, 0, 0)),
            pl.BlockSpec((None, 1, NSA_KV * CMP_HIDDEN), lambda k, b: (k, 0, 0)),
            pl.BlockSpec((None, NSA_KV * CMP_HIDDEN, NSA_KV * LANE), lambda k, b: (k, 0, 0)),
        ],
        out_specs=pl.BlockSpec((None, None, NSA_KV, rows, 2 * LANE), lambda k, b: (k, b, 0, 0, 0)),
        out_shape=jax.ShapeDtypeStruct((2, batch, NSA_KV, rows, 2 * LANE), F32),
        compiler_params=_params(("arbitrary", "arbitrary")),
        name="compress",
    )(raw, pe_u, pe_u, wa.astype(BF16), wb.astype(BF16), b1t, w2b.astype(BF16))


def _t5_bucket(dist):
    n = jnp.maximum(dist, 0)
    max_exact = N_BUCKETS // 2
    nf = jnp.maximum(n, 1).astype(F32)
    large = max_exact + (jnp.log(nf * (1.0 / max_exact)) / math.log(MAX_DISTANCE / max_exact)
                         * (N_BUCKETS - max_exact)).astype(jnp.int32)
    large = jnp.minimum(large, N_BUCKETS - 1)
    return jnp.where(n < max_exact, n, large)


def _tables_kernel(rb_ref, tb_ref, tc_ref, wm_ref):
    def fill(out_ref, dist):
        bk = _t5_bucket(dist)
        for h in range(NSA_HEADS):
            far = rb_ref[N_BUCKETS - 1, h]
            val = jnp.zeros(dist.shape, F32)
            for b in range(N_BUCKETS - 1):
                val = jnp.where(bk == b, (rb_ref[b, h] - far) * LOG2E, val)
            out_ref[h] = jnp.where(dist >= 0, val, MASK_F32)

    tq = lax.broadcasted_iota(jnp.int32, (Q_BLOCK, 2 * Q_BLOCK), 0)
    j = lax.broadcasted_iota(jnp.int32, (Q_BLOCK, 2 * Q_BLOCK), 1)
    fill(tb_ref, tq + Q_BLOCK - j)
    tq = lax.broadcasted_iota(jnp.int32, (Q_BLOCK, LANE), 0)
    c = lax.broadcasted_iota(jnp.int32, (Q_BLOCK, LANE), 1)
    cmp_first = Q_BLOCK // CMP_STRIDE - LANE
    fill(tc_ref, tq - CMP_STRIDE * (cmp_first + c) - (CMP_BLOCK - 1))
    wm_ref[...] = jnp.where(c > tq, 0.0, MASK_F32)


def _bias_tables(rel_bias):
    return pl.pallas_call(
        _tables_kernel,
        in_specs=[pl.BlockSpec(memory_space=pltpu.SMEM)],
        out_shape=[jax.ShapeDtypeStruct((NSA_HEADS, Q_BLOCK, 2 * Q_BLOCK), F32),
                   jax.ShapeDtypeStruct((NSA_HEADS, Q_BLOCK, LANE), F32),
                   jax.ShapeDtypeStruct((Q_BLOCK, LANE), F32)],
        name="bias_tables",
    )(rel_bias)


def _nsa_kernel(q_ref, gt_ref, kc_ref, vc_ref, ks_ref, vs_ref, kw_ref, vw_ref,
                tb_ref, tc_ref, wm_ref, o_ref, qsel_ref, s0_ref, s1_ref, m0_ref, m1_ref,
                al0_ref, al1_ref, acc_ref, *, n_sel):
    i = pl.program_id(2)
    s_ref, m_ref, al_ref = (s0_ref, s1_ref), (m0_ref, m1_ref), (al0_ref, al1_ref)
    rows = NSA_REP * Q_BLOCK
    qa = q_ref[...].reshape(rows, LANE)
    tb = tb_ref[...].reshape(rows, 2 * Q_BLOCK)
    zero_tb = jnp.zeros((rows, KV_TILE - 2 * Q_BLOCK), F32)

    c0 = pl.multiple_of(8 * i + 8 + CMP_PAD - CMP_SPAN, 8)
    s = _dot_nt(qa, kc_ref[pl.ds(c0, CMP_SPAN), :].astype(BF16))
    s = s + jnp.concatenate([jnp.zeros((rows, CMP_SPAN - LANE), F32),
                             tc_ref[...].reshape(rows, LANE)], axis=1)
    m = jnp.maximum(jnp.max(s, axis=1, keepdims=True), MAX_CLAMP)
    acc = _dot(jnp.exp(s - m).astype(BF16), vc_ref[pl.ds(c0, CMP_SPAN), :].astype(BF16))
    inv = 1.0 / jnp.maximum(acc[:, HEAD_DIM:HEAD_DIM + 1], 1e-30)
    oc = acc[:, 0:LANE] * inv
    impu = acc[:, LANE:2 * LANE] * inv
    imp = impu[0:Q_BLOCK]
    for r in range(1, NSA_REP):
        imp = imp + impu[r * Q_BLOCK:(r + 1) * Q_BLOCK]

    t0 = pl.multiple_of(Q_BLOCK * (i + 1) + KV_PAD - KV_TILE, Q_BLOCK)
    w1 = pl.multiple_of(t0 - Q_BLOCK, Q_BLOCK)
    s0 = _dot_nt(qa, kw_ref[pl.ds(t0, KV_TILE), :]) + jnp.concatenate([zero_tb, tb], axis=1)
    s1 = _dot_nt(qa, kw_ref[pl.ds(w1, Q_BLOCK), :]) + jnp.tile(wm_ref[...], (NSA_REP, 1))
    m = jnp.maximum(jnp.max(s0, axis=1, keepdims=True), jnp.max(s1, axis=1, keepdims=True))
    acc = (_dot(jnp.exp(s0 - m).astype(BF16), vw_ref[pl.ds(t0, KV_TILE), :])
           + _dot(jnp.exp(s1 - m).astype(BF16), vw_ref[pl.ds(w1, Q_BLOCK), :]))
    ow = acc * (1.0 / acc[:, HEAD_DIM:HEAD_DIM + 1])

    imp_t = imp.T
    jb = lax.broadcasted_iota(jnp.int32, (LANE, Q_BLOCK), 0)
    t = i * Q_BLOCK + lax.broadcasted_iota(jnp.int32, (LANE, Q_BLOCK), 1)
    cur = t // SEL_BLOCK
    forced = (jb == 0) | (jb == cur) | (jb == cur - 1)
    score = jnp.where(forced, FORCE_SCORE, jnp.where(jb * SEL_BLOCK <= t, imp_t, -1.0))
    notsel = jnp.ones((LANE, Q_BLOCK), F32)
    for _ in range(n_sel):
        mx = jnp.max(score, axis=0, keepdims=True)
        first = jnp.min(jnp.where(score == mx, jb, LANE), axis=0, keepdims=True)
        hit = jb == first
        notsel = jnp.where(hit, 0.0, notsel)
        score = jnp.where(hit, -2.0, score)
    notsel_q = notsel.T.astype(BF16)
    qsel_ref[:, 0:LANE] = qa
    for r in range(NSA_REP):
        qsel_ref[r * Q_BLOCK:(r + 1) * Q_BLOCK, LANE:2 * LANE] = notsel_q
    qs = qsel_ref[...]

    n_tiles = (i + 1 + KV_TILE // Q_BLOCK - 1) // (KV_TILE // Q_BLOCK)

    def bcast(col):
        return jnp.broadcast_to(col, (rows, LANE))

    def lanes(x):
        return jnp.concatenate([x] * (KV_TILE // LANE), axis=1)

    def qk_stage(k, prev, slot):
        st = pl.multiple_of(t0 - KV_TILE * k, Q_BLOCK)
        s_n = _dot_nt(qs, ks_ref[pl.ds(st, KV_TILE), :])
        s_ref[slot][...] = s_n
        m_c = m_ref[prev][...]
        m_n = jnp.maximum(m_c, bcast(jnp.max(s_n, axis=1, keepdims=True)))
        m_ref[slot][...] = m_n
        al_ref[slot][...] = jnp.exp(m_c - m_n)

    def pv_stage(k, slot):
        st = pl.multiple_of(t0 - KV_TILE * k, Q_BLOCK)
        p = jnp.exp(s_ref[slot][...] - lanes(m_ref[slot][...]))
        acc_ref[...] = (al_ref[slot][...] * acc_ref[...]
                        + _dot(p.astype(BF16), vs_ref[pl.ds(st, KV_TILE), :]))

    s = _dot_nt(qs, ks_ref[pl.ds(t0, KV_TILE), :]) + jnp.concatenate([zero_tb, tb], axis=1)
    s_ref[0][...] = s
    m_ref[0][...] = bcast(jnp.max(s, axis=1, keepdims=True))
    al_ref[0][...] = jnp.ones((rows, LANE), F32)
    acc_ref[...] = jnp.zeros((rows, LANE), F32)

    def sel_body(j, carry):
        qk_stage(2 * j + 1, 0, 1)
        pv_stage(2 * j, 0)
        qk_stage(2 * j + 2, 1, 0)
        pv_stage(2 * j + 1, 1)
        return carry

    n_pairs = (n_tiles - 1) // 2
    lax.fori_loop(0, n_pairs, sel_body, 0)
    odd_left = n_tiles - 1 - 2 * n_pairs

    @pl.when(odd_left == 1)
    def _two_left():
        qk_stage(n_tiles - 1, 0, 1)
        pv_stage(n_tiles - 2, 0)
        pv_stage(n_tiles - 1, 1)

    @pl.when(odd_left == 0)
    def _one_left():
        pv_stage(n_tiles - 1, 0)

    acc = acc_ref[...]
    osl = acc * (1.0 / acc[:, HEAD_DIM:HEAD_DIM + 1])

    gts = gt_ref[...]
    lane = lax.broadcasted_iota(jnp.int32, (Q_BLOCK, LANE), 1)
    heads = []
    for r in range(NSA_REP):
        rs = slice(r * Q_BLOCK, (r + 1) * Q_BLOCK)
        heads.append(gts[:, r:r + 1] * oc[rs]
                     + gts[:, NSA_REP + r:NSA_REP + r + 1] * osl[rs]
                     + gts[:, 2 * NSA_REP + r:2 * NSA_REP + r + 1] * ow[rs])
    for pr in range(NSA_REP // 2):
        o_ref[:, pr * LANE:(pr + 1) * LANE] = jnp.where(
            lane < HEAD_DIM, heads[2 * pr], pltpu.roll(heads[2 * pr + 1], HEAD_DIM, 1))


def _nsa(q_aug, gates, cmp_aug, ksel, vsel, kwin, vwin, tb, tc, wm, batch, seq):
    nq = seq // Q_BLOCK
    sp = seq + KV_PAD
    crow = cmp_aug.shape[3]
    n_sel = min(N_SELECT, seq // SEL_BLOCK)
    assert seq // SEL_BLOCK <= LANE and KV_PAD >= KV_TILE and KV_PAD >= WINDOW
    assert seq // CMP_STRIDE <= CMP_SPAN <= CMP_PAD and crow == CMP_PAD + seq // CMP_STRIDE
    kv_spec = lambda w: pl.BlockSpec((None, None, sp, w), lambda b, g, i: (b, g, 0, 0))
    in_specs = [
        pl.BlockSpec((None, NSA_REP, Q_BLOCK, LANE), lambda b, g, i: (b, g, i, 0)),
        pl.BlockSpec((Q_BLOCK, LANE), lambda b, g, i: (b * nq + i, g)),
        pl.BlockSpec((None, None, None, crow, LANE), lambda b, g, i: (0, b, g, 0, 0)),
        pl.BlockSpec((None, None, None, crow, 2 * LANE), lambda b, g, i: (1, b, g, 0, 0)),
        kv_spec(2 * LANE), kv_spec(LANE), kv_spec(LANE), kv_spec(LANE),
        pl.BlockSpec((NSA_REP, Q_BLOCK, 2 * Q_BLOCK), lambda b, g, i: (g, 0, 0)),
        pl.BlockSpec((NSA_REP, Q_BLOCK, LANE), lambda b, g, i: (g, 0, 0)),
        pl.BlockSpec((Q_BLOCK, LANE), lambda b, g, i: (0, 0)),
    ]
    rows = NSA_REP * Q_BLOCK
    return pl.pallas_call(
        functools.partial(_nsa_kernel, n_sel=n_sel),
        grid=(batch, NSA_KV, nq),
        in_specs=in_specs,
        out_specs=pl.BlockSpec((Q_BLOCK, NSA_REP * HEAD_DIM), lambda b, g, i: (b * nq + i, g)),
        out_shape=jax.ShapeDtypeStruct((batch * seq, NSA_WIDTH), F32),
        scratch_shapes=[pltpu.VMEM((rows, 2 * LANE), BF16),
                        pltpu.VMEM((rows, KV_TILE), F32),
                        pltpu.VMEM((rows, KV_TILE), F32),
                        pltpu.VMEM((rows, LANE), F32),
                        pltpu.VMEM((rows, LANE), F32),
                        pltpu.VMEM((rows, LANE), F32),
                        pltpu.VMEM((rows, LANE), F32),
                        pltpu.VMEM((rows, LANE), F32)],
        compiler_params=_params(("arbitrary", "arbitrary", "arbitrary")),
        name="nsa",
    )(q_aug, gates, cmp_aug, cmp_aug, ksel, vsel, kwin, vwin, tb, tc, wm)


_N_GROUP_SCRATCH = 8


def _oddeven_merge_sort(n):
    def merge(lo, hi, r):
        step = 2 * r
        if step < hi - lo:
            yield from merge(lo, hi, step)
            yield from merge(lo + r, hi, step)
            yield from ((k, k + r) for k in range(lo + r, hi - r, step))
        else:
            yield (lo, lo + r)

    def sort(lo, hi):
        if hi > lo:
            mid = lo + (hi - lo) // 2
            yield from sort(lo, mid)
            yield from sort(mid + 1, hi)
            yield from merge(lo, hi, 1)

    return tuple(sort(0, n - 1))


def _bitonic_merge(n):
    pairs, k = [], n // 2
    while k >= 1:
        pairs += [(a, a + k) for a in range(n) if not a & k]
        k //= 2
    return tuple(pairs)


_SORT16 = _oddeven_merge_sort(N_SELECT)
_BITONIC16 = _bitonic_merge(N_SELECT)


def _nsa2_kernel(q_ref, gt_ref, kc_ref, vc_ref, ks_ref, vs_ref, kw_ref, vw_ref,
                 tb_ref, tc_ref, wm_ref, gx_ref, o_ref, *scratch, n_sel, select_all):
    i = pl.program_id(1)
    rows = NSA_REP * Q_BLOCK
    per_group = [scratch[g * _N_GROUP_SCRATCH:(g + 1) * _N_GROUP_SCRATCH] for g in range(NSA_KV)]
    qsel_ref = [s[0] for s in per_group]
    s_ref = [(s[1], s[2]) for s in per_group]
    m_ref = [(s[3], s[4]) for s in per_group]
    al_ref = [(s[5], s[6]) for s in per_group]
    acc_ref = [s[7] for s in per_group]

    zero_tb = jnp.zeros((rows, KV_TILE - 2 * Q_BLOCK), F32)
    c0 = pl.multiple_of(8 * i + 8 + CMP_PAD - CMP_SPAN, 8)
    t0 = pl.multiple_of(Q_BLOCK * (i + 1) + KV_PAD - KV_TILE, Q_BLOCK)
    w1 = pl.multiple_of(t0 - Q_BLOCK, Q_BLOCK)
    n_tiles = (i + 1 + KV_TILE // Q_BLOCK - 1) // (KV_TILE // Q_BLOCK)
    lane = lax.broadcasted_iota(jnp.int32, (Q_BLOCK, LANE), 1)

    def bcast(col):
        return jnp.broadcast_to(col, (rows, LANE))

    def lanes(x, n):
        return jnp.concatenate([x] * n, axis=1)

    def head_out(acc, r):
        rs = slice(r * Q_BLOCK, (r + 1) * Q_BLOCK)
        a, b = acc[rs, 0:LANE], acc[rs, LANE:2 * LANE]
        return a * (1.0 / b) if r % 2 == 0 else b * (1.0 / a)

    def heads_of(g):
        return slice(g * NSA_REP, (g + 1) * NSA_REP)

    def q_of(g):
        return q_ref[heads_of(g)].reshape(rows, LANE)

    def tile0_bias(g):
        return jnp.concatenate([zero_tb, tb_ref[heads_of(g)].reshape(rows, 2 * Q_BLOCK)], axis=1)

    def cmp_phase(g):
        hs = heads_of(g)
        s = _dot_nt(q_of(g), kc_ref[g, pl.ds(c0, CMP_SPAN), :].astype(BF16))
        s = s + jnp.concatenate([jnp.zeros((rows, CMP_SPAN - LANE), F32),
                                 tc_ref[hs].reshape(rows, LANE)], axis=1)
        m = jnp.maximum(jnp.max(s, axis=1, keepdims=True), MAX_CLAMP)
        acc = _dot(jnp.exp2(s - m).astype(BF16), vc_ref[g, pl.ds(c0, CMP_SPAN), :].astype(BF16))
        num_den = acc[:, 0:LANE]
        lane_r = lax.broadcasted_iota(jnp.int32, (rows, LANE), 1)
        den = jnp.where(lane_r < HEAD_DIM, pltpu.roll(num_den, HEAD_DIM, 1), num_den)
        inv = 1.0 / jnp.maximum(den, 1e-30)
        oc = num_den * inv
        impu = acc[:, LANE:2 * LANE] * inv
        imp = impu[0:Q_BLOCK]
        for r in range(1, NSA_REP):
            imp = imp + impu[r * Q_BLOCK:(r + 1) * Q_BLOCK]
        return oc, imp

    def win_phase(g):
        qa = q_of(g)
        s0 = _dot_nt(qa, kw_ref[g, pl.ds(t0, KV_TILE), :]) + tile0_bias(g)
        s1 = _dot_nt(qa, kw_ref[g, pl.ds(w1, Q_BLOCK), :]) + jnp.tile(wm_ref[...], (NSA_REP, 1))
        m = jnp.maximum(jnp.max(s0, axis=1, keepdims=True), jnp.max(s1, axis=1, keepdims=True))
        accw = (_dot(jnp.exp2(s0 - m).astype(BF16), vw_ref[g, pl.ds(t0, KV_TILE), :])
                + _dot(jnp.exp2(s1 - m).astype(BF16), vw_ref[g, pl.ds(w1, Q_BLOCK), :]))
        return [head_out(accw, r) for r in range(NSA_REP)]

    def not_selected(imp):
        jb = lax.broadcasted_iota(jnp.int32, (LANE, Q_BLOCK), 0)
        t = i * Q_BLOCK + lax.broadcasted_iota(jnp.int32, (LANE, Q_BLOCK), 1)
        cur = t // SEL_BLOCK
        forced = (jb == 0) | (jb == cur) | (jb == cur - 1)
        score = jnp.where(forced, FORCE_SCORE, jnp.where(jb * SEL_BLOCK <= t, imp.T, -1.0))
        if select_all:
            return jnp.zeros((Q_BLOCK, LANE), F32)
        sub = 8
        x = [score[sub * v:sub * (v + 1), :] for v in range(LANE // sub)]
        y = list(x)
        for a, b in _SORT16:
            y[a], y[b] = jnp.maximum(y[a], y[b]), jnp.minimum(y[a], y[b])
        for shift in (4, 2, 1):
            z = [jnp.maximum(y[k], pltpu.roll(y[n_sel - 1 - k], shift, 0)) for k in range(n_sel)]
            if shift > 1:
                for a, b in _BITONIC16:
                    z[a], z[b] = jnp.maximum(z[a], z[b]), jnp.minimum(z[a], z[b])
            y = z
        while len(y) > 1:
            y = [jnp.minimum(y[2 * k], y[2 * k + 1]) for k in range(len(y) // 2)]
        tau = y[0]
        gt = jnp.concatenate([jnp.where(xv > tau, 1.0, 0.0) for xv in x], axis=0)
        eq = jnp.concatenate([jnp.where(xv == tau, 1.0, 0.0) for xv in x], axis=0)
        kb = lax.broadcasted_iota(jnp.int32, (LANE, LANE), 1)
        n_above = _dot(jnp.ones((LANE, LANE), BF16), gt.astype(BF16))
        n_tied_before = _dot(jnp.where(kb < jb, 1.0, 0.0).astype(BF16), eq.astype(BF16))
        keep = (gt > 0.0) | ((eq > 0.0) & (n_above + n_tied_before < n_sel))
        return jnp.where(keep, 0.0, 1.0).T

    def sel_phase(g, imp):
        notsel_q = not_selected(imp).astype(BF16)
        qsel_ref[g][:, 0:LANE] = q_of(g)
        for r in range(NSA_REP):
            qsel_ref[g][r * Q_BLOCK:(r + 1) * Q_BLOCK, LANE:2 * LANE] = notsel_q
        s = _dot_nt(qsel_ref[g][...], ks_ref[g, pl.ds(t0, KV_TILE), :]) + tile0_bias(g)
        s_ref[g][0][...] = s
        m_ref[g][0][...] = bcast(jnp.max(s, axis=1, keepdims=True))
        al_ref[g][0][...] = jnp.ones((rows, LANE), F32)
        acc_ref[g][...] = jnp.zeros((rows, 2 * LANE), F32)

    def qk_stage(g, k, prev, slot):
        st = pl.multiple_of(t0 - KV_TILE * k, Q_BLOCK)
        s_n = _dot_nt(qsel_ref[g][...], ks_ref[g, pl.ds(st, KV_TILE), :])
        s_ref[g][slot][...] = s_n
        m_c = m_ref[g][prev][...]
        m_n = jnp.maximum(m_c, bcast(jnp.max(s_n, axis=1, keepdims=True)))
        m_ref[g][slot][...] = m_n
        al_ref[g][slot][...] = jnp.exp2(m_c - m_n)

    def pv_stage(g, k, slot):
        st = pl.multiple_of(t0 - KV_TILE * k, Q_BLOCK)
        p = jnp.exp2(s_ref[g][slot][...] - lanes(m_ref[g][slot][...], KV_TILE // LANE))
        acc_ref[g][...] = (lanes(al_ref[g][slot][...], 2) * acc_ref[g][...]
                           + _dot(p.astype(BF16), vs_ref[g, pl.ds(st, KV_TILE), :]))

    groups = range(NSA_KV)
    cmp_out = [cmp_phase(g) for g in groups]
    win_out = []
    for g in groups:
        sel_phase(g, cmp_out[g][1])
        win_out.append(win_phase(g))
    fronts = [(cmp_out[g][0], win_out[g]) for g in groups]

    def sel_body(j, carry):
        for g in groups:
            qk_stage(g, 2 * j + 1, 0, 1)
        for g in groups:
            pv_stage(g, 2 * j, 0)
        for g in groups:
            qk_stage(g, 2 * j + 2, 1, 0)
        for g in groups:
            pv_stage(g, 2 * j + 1, 1)
        return carry

    n_pairs = (n_tiles - 1) // 2
    lax.fori_loop(0, n_pairs, sel_body, 0)
    odd_left = n_tiles - 1 - 2 * n_pairs

    @pl.when(odd_left == 1)
    def _two_left():
        for g in groups:
            qk_stage(g, n_tiles - 1, 0, 1)
        for g in groups:
            pv_stage(g, n_tiles - 2, 0)
        for g in groups:
            pv_stage(g, n_tiles - 1, 1)

    @pl.when(odd_left == 0)
    def _one_left():
        for g in groups:
            pv_stage(g, n_tiles - 1, 0)

    for g in groups:
        oc, ow = fronts[g]
        acc = acc_ref[g][...]
        gx = _dot(gt_ref[:, 2 * g * LANE:(2 * g + 2) * LANE], gx_ref[...])

        def gate(branch, r):
            c = branch * NSA_REP + r
            return gx[:, c * LANE:(c + 1) * LANE]

        heads = []
        for r in range(NSA_REP):
            rs = slice(r * Q_BLOCK, (r + 1) * Q_BLOCK)
            oc_r = oc[rs] if r % 2 == 0 else pltpu.roll(oc[rs], HEAD_DIM, 1)
            heads.append(gate(0, r) * oc_r + gate(1, r) * head_out(acc, r) + gate(2, r) * ow[r])
        for pr in range(NSA_REP // 2):
            col = (g * NSA_REP // 2 + pr) * LANE
            o_ref[:, col:col + LANE] = jnp.where(lane < HEAD_DIM, heads[2 * pr], heads[2 * pr + 1])


def _nsa2(q_aug, gates, cmp_aug, ksel, vsel, kwin, vwin, tb, tc, wm, batch, seq):
    nq = seq // Q_BLOCK
    sp = seq + KV_PAD
    crow = cmp_aug.shape[3]
    n_sel = min(N_SELECT, seq // SEL_BLOCK)
    assert seq // SEL_BLOCK <= LANE and KV_PAD >= KV_TILE and KV_PAD >= WINDOW
    assert seq // CMP_STRIDE <= CMP_SPAN <= CMP_PAD and crow == CMP_PAD + seq // CMP_STRIDE
    assert seq // SEL_BLOCK <= N_SELECT or n_sel == LANE // 8
    n_gate = 3 * NSA_REP
    col =jnp.arange(n_gate * LANE) // LANE
    row = jnp.arange(2 * LANE) % LANE
    gate_spread = (row[:, None] == col[None, :]).astype(BF16)
    once = pl.Buffered(1)
    kv_spec = lambda w: pl.BlockSpec((None, NSA_KV, sp, w), lambda b, i: (b, 0, 0, 0), pipeline_mode=once)
    cmp_spec = lambda k, w: pl.BlockSpec((None, None, NSA_KV, crow, w), lambda b, i: (k, b, 0, 0, 0),
                                         pipeline_mode=once)
    in_specs = [
        pl.BlockSpec((None, NSA_HEADS, Q_BLOCK, LANE), lambda b, i: (b, 0, i, 0)),
        pl.BlockSpec((Q_BLOCK, 2 * NSA_KV * LANE), lambda b, i: (b * nq + i, 0)),
        cmp_spec(0, LANE), cmp_spec(1, 2 * LANE),
        kv_spec(2 * LANE), kv_spec(2 * LANE), kv_spec(LANE), kv_spec(2 * LANE),
        _const_spec((NSA_HEADS, Q_BLOCK, 2 * Q_BLOCK)),
        _const_spec((NSA_HEADS, Q_BLOCK, LANE)),
        _const_spec((Q_BLOCK, LANE)),
        _const_spec((2 * LANE, n_gate * LANE)),
    ]
    rows = NSA_REP * Q_BLOCK
    group_scratch = [pltpu.VMEM((rows, 2 * LANE), BF16),
                     pltpu.VMEM((rows, KV_TILE), F32),
                     pltpu.VMEM((rows, KV_TILE), F32),
                     pltpu.VMEM((rows, LANE), F32),
                     pltpu.VMEM((rows, LANE), F32),
                     pltpu.VMEM((rows, LANE), F32),
                     pltpu.VMEM((rows, LANE), F32),
                     pltpu.VMEM((rows, 2 * LANE), F32)]
    assert len(group_scratch) == _N_GROUP_SCRATCH
    return pl.pallas_call(
        functools.partial(_nsa2_kernel, n_sel=n_sel, select_all=seq // SEL_BLOCK <= N_SELECT),
        grid=(batch, nq),
        in_specs=in_specs,
        out_specs=pl.BlockSpec((Q_BLOCK, NSA_WIDTH), lambda b, i: (b * nq + i, 0)),
        out_shape=jax.ShapeDtypeStruct((batch * seq, NSA_WIDTH), F32),
        scratch_shapes=group_scratch * NSA_KV,
        compiler_params=_params(("arbitrary", "arbitrary")),
        name="nsa",
    )(q_aug, gates, cmp_aug, cmp_aug, ksel, vsel, kwin, vwin, tb, tc, wm, gate_spread)


def _memkv_kernel(mem_ref, g_ref, wk_ref, wv_ref, k_ref, v_ref):
    h = _rms(mem_ref[...], g_ref[...]).astype(BF16)
    k_ref[...] = _dot(h, wk_ref[...]).astype(BF16)
    v_ref[...] = _dot(h, wv_ref[...]).astype(BF16)


def _mem_kv(mem, g_mem, wkv):
    batch, m, _ = mem.shape
    blk = pl.BlockSpec((None, m, D_MODEL), lambda b: (b, 0, 0))
    return pl.pallas_call(
        _memkv_kernel,
        grid=(batch,),
        in_specs=[blk, _const_spec((1, D_MODEL)), _const_spec((D_MODEL, D_MODEL)),
                  _const_spec((D_MODEL, D_MODEL))],
        out_specs=[blk, blk],
        out_shape=[jax.ShapeDtypeStruct((batch, m, D_MODEL), BF16)] * 2,
        compiler_params=_params(("arbitrary",)),
        name="mem_kv",
    )(mem, g_mem.reshape(1, -1), wkv[:, :D_MODEL].astype(BF16), wkv[:, D_MODEL:].astype(BF16))


def _mixout_kernel(x_ref, ya_ref, yb_ref, gb_ref, wo_ref, post_ref, xpre_ref, wq_ref,
                   km_ref, vm_ref, wxo_ref, xpost_ref, o_ref):
    yb = _rms(yb_ref[...], gb_ref[...]).astype(BF16)
    y = _dot(ya_ref[...].astype(BF16), wo_ref[0:GM_WIDTH, :]) + _dot(yb, wo_ref[GM_WIDTH:, :])
    x1 = x_ref[...] + _rms(y, post_ref[...])
    h = _rms(x1, xpre_ref[...]).astype(BF16)
    q = _dot(h, wq_ref[...]).astype(BF16)
    outs = []
    for hd in range(XA_HEADS):
        hs = slice(hd * XA_HEAD_DIM, (hd + 1) * XA_HEAD_DIM)
        s = _dot_nt(q[:, hs], km_ref[:, hs])
        e = jnp.exp(s - jnp.max(s, axis=1, keepdims=True))
        o = _dot(e.astype(BF16), vm_ref[:, hs]) / jnp.sum(e, axis=1, keepdims=True)
        outs.append(o.astype(BF16))
    o = _dot(jnp.concatenate(outs, axis=1), wxo_ref[...])
    o_ref[...] = x1 + _rms(o, xpost_ref[...])


def _mix_out(x2d, ya, yb, batch, seq, gain_b, w_out, mix_post, xa_pre, wq, kmem, vmem, wo, xa_post):
    tm = TOKEN_TILE
    nt = seq // tm
    row = lambda w: pl.BlockSpec((tm, w), lambda b, t: (b * nt + t, 0))
    mem_spec = pl.BlockSpec((None, kmem.shape[1], D_MODEL), lambda b, t: (b, 0, 0))
    sq = _const_spec((D_MODEL, D_MODEL))
    vec = _const_spec((1, D_MODEL))
    return pl.pallas_call(
        _mixout_kernel,
        grid=(batch, nt),
        in_specs=[row(D_MODEL), row(GM_WIDTH), row(NSA_WIDTH), _const_spec((1, NSA_WIDTH)), sq, vec,
                  vec, sq, mem_spec, mem_spec, sq, vec],
        out_specs=row(D_MODEL),
        out_shape=jax.ShapeDtypeStruct(x2d.shape, F32),
        compiler_params=_params(("arbitrary", "arbitrary")),
        name="mix_out",
    )(x2d, ya, yb, gain_b.reshape(1, -1), w_out.astype(BF16), mix_post.reshape(1, -1),
      xa_pre.reshape(1, -1), (wq * XA_HEAD_DIM ** -0.5).astype(BF16), kmem, vmem,
      wo.astype(BF16), xa_post.reshape(1, -1))


def kernel(x, mem, ffn1_pre, ffn1_post, ffn1_wg, ffn1_wu, ffn1_wd, mix_pre, mix_post, w_in, gm_ln_g, gm_ln_b, gm_ws, gm_bs, ck_pe, ck_w1, ck_b1, ck_w2, cv_pe, cv_w1, cv_b1, cv_w2, rel_bias, out_gain_a, out_gain_b, w_out, xa_pre, xa_post, mem_norm, xa_wq, xa_wkv, xa_wo, ffn2_pre, ffn2_post, ffn2_wg, ffn2_wu, ffn2_wd):
    batch, seq, _ = x.shape
    depth = ffn1_pre.shape[0]
    x2d = x.reshape(batch * seq, D_MODEL)
    tb, tc, wm = _bias_tables(rel_bias)
    for l in range(depth):
        x2d = _ffn(x2d, ffn1_pre[l], ffn1_wg[l], ffn1_wu[l], ffn1_wd[l], ffn1_post[l])
        ya, q_aug, cmp_raw, ksel, vsel, kwin, vwin, gates = _mixproj(
            x2d, batch, seq, mix_pre[l], w_in[l], gm_ln_g[l], gm_ln_b[l], gm_ws[l], gm_bs[l],
            out_gain_a[l])
        cmp_aug = _compress(cmp_raw, batch, seq,
                            jnp.stack([ck_pe[l], cv_pe[l]]), jnp.stack([ck_w1[l], cv_w1[l]]),
                            jnp.stack([ck_b1[l], cv_b1[l]]), jnp.stack([ck_w2[l], cv_w2[l]]))
        yb = _nsa2(q_aug, gates, cmp_aug, ksel, vsel, kwin, vwin, tb, tc, wm, batch, seq)
        kmem, vmem = _mem_kv(mem, mem_norm[l], xa_wkv[l])
        x2d = _mix_out(x2d, ya, yb, batch, seq, out_gain_b[l], w_out[l], mix_post[l], xa_pre[l],
                       xa_wq[l], kmem, vmem, xa_wo[l], xa_post[l])
        x2d = _ffn(x2d, ffn2_pre[l], ffn2_wg[l], ffn2_wu[l], ffn2_wd[l], ffn2_post[l])
    return x2d.reshape(batch, seq, D_MODEL)
```

```python
import functools
import math

import jax
import jax.numpy as jnp
from jax import lax
from jax.experimental import pallas as pl
from jax.experimental.pallas import tpu as pltpu

F32 = jnp.float32
BF16 = jnp.bfloat16

D_MODEL = 1024
MEM_LEN = 256
EPS = 1e-6
D_FF = 2816
GM_WIDTH = 512
GM_GROUPS = 4
GM_CH = GM_WIDTH // GM_GROUPS
CHUNK = 128
NSA_WIDTH = D_MODEL - GM_WIDTH
HEAD_DIM = 64
NSA_HEADS = NSA_WIDTH // HEAD_DIM
NSA_KV = 2
NSA_REP = NSA_HEADS // NSA_KV
KV_W = NSA_KV * HEAD_DIM
CMP_BLOCK = 32
CMP_STRIDE = 16
CMP_HIDDEN = 256
SEL_BLOCK = 64
N_SELECT = 16
WINDOW = 512
Q_BLOCK = 128
N_BUCKETS = 32
MAX_DISTANCE = 128
XA_HEADS = 4
XA_HEAD_DIM = D_MODEL // XA_HEADS
FORCE_SCORE = 1e4

LANE = 128
TOKEN_TILE = 512
FF_CHUNK = 256
KV_PAD = 512
KV_TILE = 512
CMP_PAD = 512
CMP_SPAN = 512
MASK_BF16 = -(2.0 ** 100)
MASK_F32 = -1e30
MAX_CLAMP = -(2.0 ** 99)
LOG2E = math.log2(math.e)
VMEM_LIMIT = 56 * 1024 * 1024

_NT = (((1,), (1,)), ((), ()))


def _rms(x, g):
    ms = jnp.mean(x * x, axis=-1, keepdims=True)
    return x * lax.rsqrt(ms + EPS) * g


def _dot(a, b):
    return jnp.dot(a, b, preferred_element_type=F32)


def _dot_nt(a, b):
    return lax.dot_general(a, b, _NT, preferred_element_type=F32)


def _params(sem, vmem=VMEM_LIMIT):
    return pltpu.CompilerParams(dimension_semantics=sem, vmem_limit_bytes=vmem)


def _const_spec(shape):
    nd = len(shape)
    return pl.BlockSpec(shape, lambda *_: (0,) * nd)


def _ffn_kernel(x_ref, pre_ref, wg_ref, wu_ref, wd_ref, post_ref, o_ref, acc_ref):
    x = x_ref[...]
    h = _rms(x, pre_ref[...]).astype(BF16)
    for c in range(D_FF // FF_CHUNK):
        sl = slice(c * FF_CHUNK, (c + 1) * FF_CHUNK)
        g = _dot(h, wg_ref[:, sl])
        u = _dot(h, wu_ref[:, sl])
        a = (jax.nn.silu(g) * u).astype(BF16)
        d = _dot(a, wd_ref[sl, :])
        if c == 0:
            acc_ref[...] = d
        else:
            acc_ref[...] += d
    o_ref[...] = x + 0.5 * _rms(acc_ref[...], post_ref[...])


def _ffn(x2d, pre, wg, wu, wd, post):
    n = x2d.shape[0]
    tm = TOKEN_TILE
    row = pl.BlockSpec((tm, D_MODEL), lambda t: (t, 0))
    return pl.pallas_call(
        _ffn_kernel,
        grid=(n // tm,),
        in_specs=[row, _const_spec((1, D_MODEL)), _const_spec((D_MODEL, D_FF)),
                  _const_spec((D_MODEL, D_FF)), _const_spec((D_FF, D_MODEL)),
                  _const_spec((1, D_MODEL))],
        out_specs=row,
        out_shape=jax.ShapeDtypeStruct((n, D_MODEL), F32),
        scratch_shapes=[pltpu.VMEM((tm, D_MODEL), F32)],
        compiler_params=_params(("arbitrary",)),
        name="ffn",
    )(x2d, pre.reshape(1, -1), wg.astype(BF16), wu.astype(BF16), wd.astype(BF16),
      post.reshape(1, -1))


def _mixproj_kernel(x_ref, pre_ref, w_ref, wgt_ref, lng_ref, lnb_ref, ws_ref, bs_ref, ga_ref,
                    ya_ref, q_ref, cmp_ref, ksel_ref, vsel_ref, kwin_ref, vwin_ref, gt_ref, cmps_ref):
    t = pl.program_id(1)
    tm = x_ref.shape[0]
    lane = lax.broadcasted_iota(jnp.int32, (tm, LANE), 1)
    flag_col = jnp.where(lane == HEAD_DIM, MASK_BF16, 0.0).astype(BF16)

    @pl.when(t == 0)
    def _pad_rows():
        for g in range(NSA_KV):
            ksel_ref[g, :, 0:LANE] = flag_col
            ksel_ref[g, :, LANE:2 * LANE] = jnp.zeros((tm, LANE), BF16)
            kwin_ref[g] = flag_col
            vsel_ref[g] = jnp.zeros((tm, 2 * LANE), BF16)
            vwin_ref[g] = jnp.zeros((tm, 2 * LANE), BF16)

    @pl.when(t > 0)
    def _tile():
        h = _rms(x_ref[...], pre_ref[...]).astype(BF16)
        u = jax.nn.gelu(_dot(h, w_ref[:, 0:GM_WIDTH]))
        v = jax.nn.gelu(_dot(h, w_ref[:, GM_WIDTH:2 * GM_WIDTH]))
        r_i = lax.broadcasted_iota(jnp.int32, (CHUNK, CHUNK), 0)
        c_i = lax.broadcasted_iota(jnp.int32, (CHUNK, CHUNK), 1)
        for g in range(GM_GROUPS):
            gs = slice(g * GM_CH, (g + 1) * GM_CH)
            vg = v[:, gs]
            mu = jnp.mean(vg, axis=-1, keepdims=True)
            var = jnp.mean(jnp.square(vg - mu), axis=-1, keepdims=True)
            vn = ((vg - mu) * lax.rsqrt(var + EPS) * lng_ref[:, gs] + lnb_ref[:, gs]).astype(BF16)
            w = jnp.where(c_i <= r_i, ws_ref[g], 0.0).astype(BF16)
            for n in range(tm // CHUNK):
                rs = slice(n * CHUNK, (n + 1) * CHUNK)
                s = _dot(w, vn[rs]) + bs_ref[:, g:g + 1]
                ya_ref[rs, gs] = u[rs, gs] * s
        ya_ref[...] = _rms(ya_ref[...], ga_ref[...])
        lane_lo = lane < HEAD_DIM

        def halves(z):
            return z, pltpu.roll(z, HEAD_DIM, 1)

        one_col = jnp.where(lane == HEAD_DIM, 1.0, 0.0)
        zq = _dot(h, w_ref[:, _O_Q:_O_C])
        for a in range(NSA_HEADS // 2):
            for hd, qh in zip((2 * a, 2 * a + 1), halves(zq[:, a * LANE:(a + 1) * LANE])):
                q_ref[hd] = jnp.where(lane_lo, qh, one_col).astype(BF16)
        zc = _dot(h, w_ref[:, _O_C:_O_KV])
        nr = tm // CMP_STRIDE
        lo32 = lax.broadcasted_iota(jnp.int32, (nr, LANE), 1) < HEAD_DIM
        for kv in range(2):
            cmps_ref[kv] = zc[:, kv * KV_W:(kv + 1) * KV_W]
            for a in range(CMP_STRIDE // 2):
                za = cmps_ref[kv, pl.ds(2 * a, nr, stride=CMP_STRIDE), :]
                zb = cmps_ref[kv, pl.ds(2 * a + 1, nr, stride=CMP_STRIDE), :]
                cmp_ref[kv, :, a * LANE:(a + 1) * LANE] = jnp.where(
                    lo32, za, pltpu.roll(zb, HEAD_DIM, 1))
                cmp_ref[kv, :, (CMP_STRIDE // 2 + a) * LANE:(CMP_STRIDE // 2 + a + 1) * LANE] = jnp.where(
                    lo32, pltpu.roll(za, HEAD_DIM, 1), zb)
        zkv = _dot(h, w_ref[:, _O_KV:_O_GATE])
        k_sel, v_sel, k_win, v_win = (halves(zkv[:, j * LANE:(j + 1) * LANE]) for j in range(4))
        pos_blk = ((t - 1) * tm + lax.broadcasted_iota(jnp.int32, (tm, LANE), 0)) // SEL_BLOCK
        blk_onehot = jnp.where(lane == pos_blk, MASK_BF16, 0.0).astype(BF16)

        def v_aug(v, g):
            return jnp.concatenate([jnp.where(lane_lo, v[g], 1.0), jnp.where(lane_lo, 1.0, v[1 - g])],
                                   axis=1).astype(BF16)

        for g in range(NSA_KV):
            ksel_ref[g, :, 0:LANE] = jnp.where(lane_lo, k_sel[g], 0.0).astype(BF16)
            ksel_ref[g, :, LANE:2 * LANE] = blk_onehot
            vsel_ref[g] = v_aug(v_sel, g)
            kwin_ref[g] = jnp.where(lane_lo, k_win[g], 0.0).astype(BF16)
            vwin_ref[g] = v_aug(v_win, g)
        sg = jax.nn.sigmoid(_dot(h, wgt_ref[...]))
        hi = sg.astype(BF16)
        gt_ref[:, 0:LANE] = hi
        gt_ref[:, LANE:2 * LANE] = (sg - hi.astype(F32)).astype(BF16)


_O_Q = 2 * GM_WIDTH
_O_C = _O_Q + NSA_WIDTH
_O_KV = _O_C + 2 * KV_W
_O_GATE = _O_KV + 4 * KV_W


def _mixproj(x2d, batch, seq, mix_pre, w_in, ln_g, ln_b, ws, bs, gain_a):
    tm = TOKEN_TILE
    nt = seq // tm
    assert seq % tm == 0 and KV_PAD == tm
    scale = HEAD_DIM ** -0.5 * LOG2E
    w_main = jnp.concatenate([w_in[:, :_O_Q], w_in[:, _O_Q:_O_C] * scale, w_in[:, _O_C:_O_GATE]],
                             axis=1).astype(BF16)
    n_gate = 3 * NSA_HEADS
    w_g = w_in[:, _O_GATE:].reshape(D_MODEL, NSA_KV, NSA_REP, 3).transpose(0, 1, 3, 2)
    w_g = jnp.pad(w_g.reshape(D_MODEL, n_gate), ((0, 0), (0, LANE - n_gate))).astype(BF16)

    def tile(b, t):
        return b * nt + jnp.maximum(t - 1, 0)

    n = batch * seq
    sp = seq + KV_PAD
    out_shape = [
        jax.ShapeDtypeStruct((n, GM_WIDTH), F32),
        jax.ShapeDtypeStruct((batch, NSA_HEADS, seq, LANE), BF16),
        jax.ShapeDtypeStruct((2, n // CMP_STRIDE, CMP_STRIDE * KV_W), F32),
        jax.ShapeDtypeStruct((batch, NSA_KV, sp, 2 * LANE), BF16),
        jax.ShapeDtypeStruct((batch, NSA_KV, sp, 2 * LANE), BF16),
        jax.ShapeDtypeStruct((batch, NSA_KV, sp, LANE), BF16),
        jax.ShapeDtypeStruct((batch, NSA_KV, sp, 2 * LANE), BF16),
        jax.ShapeDtypeStruct((n, 2 * LANE), BF16),
    ]
    padded = lambda w: pl.BlockSpec((None, NSA_KV, tm, w), lambda b, t: (b, 0, t, 0))
    out_specs = [
        pl.BlockSpec((tm, GM_WIDTH), lambda b, t: (tile(b, t), 0)),
        pl.BlockSpec((None, NSA_HEADS, tm, LANE), lambda b, t: (b, 0, jnp.maximum(t - 1, 0), 0)),
        pl.BlockSpec((2, tm // CMP_STRIDE, CMP_STRIDE * KV_W), lambda b, t: (0, tile(b, t), 0)),
        padded(2 * LANE), padded(2 * LANE), padded(LANE), padded(2 * LANE),
        pl.BlockSpec((tm, 2 * LANE), lambda b, t: (tile(b, t), 0)),
    ]
    in_specs = [
        pl.BlockSpec((tm, D_MODEL), lambda b, t: (tile(b, t), 0)),
        _const_spec((1, D_MODEL)),
        _const_spec(w_main.shape), _const_spec(w_g.shape),
        _const_spec((1, GM_WIDTH)), _const_spec((1, GM_WIDTH)),
        _const_spec((GM_GROUPS, CHUNK, CHUNK)), _const_spec((CHUNK, GM_GROUPS)),
        _const_spec((1, GM_WIDTH)),
    ]
    return pl.pallas_call(
        _mixproj_kernel,
        grid=(batch, nt + 1),
        in_specs=in_specs,
        out_specs=out_specs,
        out_shape=out_shape,
        scratch_shapes=[pltpu.VMEM((2, tm, KV_W), F32)],
        compiler_params=_params(("arbitrary", "arbitrary")),
        name="mixproj",
    )(x2d, mix_pre.reshape(1, -1), w_main, w_g, ln_g.reshape(1, -1), ln_b.reshape(1, -1),
      ws, bs.T, gain_a.reshape(1, -1))


def _compress_kernel(raw_ref, pe_ref, w1_ref, b1_ref, w2_ref, o_ref):
    is_v = pl.program_id(0) == 1
    nu = raw_ref.shape[0]
    half = CMP_STRIDE * HEAD_DIM
    outs = []
    for g in range(NSA_KV):
        u = raw_ref[:, g * half:(g + 1) * half]
        a = _dot((u + pe_ref[0]).astype(BF16), w1_ref[0:half, :])
        b = _dot((u + pe_ref[1]).astype(BF16), w1_ref[half:2 * half, :])
        hid = jax.nn.gelu(a + pltpu.roll(b, nu - 1, 0) + b1_ref[...])
        outs.append(_dot(hid.astype(BF16), w2_ref[...]))
    lane = lax.broadcasted_iota(jnp.int32, (nu, LANE), 1)
    ci = lax.broadcasted_iota(jnp.int32, (nu, LANE), 0)
    u_c = CMP_BLOCK // CMP_STRIDE
    u_s = SEL_BLOCK // CMP_STRIDE
    overlap = jnp.maximum(jnp.minimum(ci + u_c, u_s * (lane + 1)) - jnp.maximum(ci, u_s * lane), 0)
    extra = jnp.where(is_v, overlap.astype(F32), 0.0)
    ones_hi = jnp.where(jnp.logical_and(is_v, lane >= HEAD_DIM), 1.0, 0.0)
    lane_p = lax.broadcasted_iota(jnp.int32, (CMP_PAD, LANE), 1)
    pad_lo = jnp.where(jnp.logical_and(jnp.logical_not(is_v), lane_p == HEAD_DIM), MASK_BF16, 0.0)
    for g in range(NSA_KV):
        o_ref[g, 0:CMP_PAD, 0:LANE] = pad_lo
        o_ref[g, 0:CMP_PAD, LANE:2 * LANE] = jnp.zeros((CMP_PAD, LANE), F32)
        o_ref[g, CMP_PAD:CMP_PAD + nu, 0:LANE] = outs[g] + ones_hi
        o_ref[g, CMP_PAD:CMP_PAD + nu, LANE:2 * LANE] = extra


def _compress(cmp_raw, batch, seq, pe, w1, b1, w2):
    nu = seq // CMP_STRIDE
    half = CMP_STRIDE * HEAD_DIM
    kd = NSA_KV * half
    raw = cmp_raw.reshape(2, batch, nu, kd)
    w2p = jnp.pad(w2, ((0, 0), (0, 0), (0, LANE - HEAD_DIM)))
    rows = CMP_PAD + nu
    return pl.pallas_call(
        _compress_kernel,
        grid=(2, batch),
        in_specs=[
            pl.BlockSpec((None, None, nu, kd), lambda k, b: (k, b, 0, 0)),
            pl.BlockSpec((None, 2, 1, half), lambda k, b: (k, 0, 0, 0)),
            pl.BlockSpec((None, 2 * half, CMP_HIDDEN), lambda k, b: (k, 0, 0)),
            pl.BlockSpec((None, 1, CMP_HIDDEN), lambda k, b: (k, 0, 0)),
            pl.BlockSpec((None, CMP_HIDDEN, LANE), lambda k, b: (k, 0, 0)),
        ],
        out_specs=pl.BlockSpec((None, None, NSA_KV, rows, 2 * LANE), lambda k, b: (k, b, 0, 0, 0)),
        out_shape=jax.ShapeDtypeStruct((2, batch, NSA_KV, rows, 2 * LANE), F32),
        compiler_params=_params(("arbitrary", "arbitrary")),
        name="compress",
    )(raw, pe.reshape(2, 2, 1, half), w1.astype(BF16), b1.reshape(2, 1, CMP_HIDDEN), w2p.astype(BF16))


def _t5_bucket(dist):
    n = jnp.maximum(dist, 0)
    max_exact = N_BUCKETS // 2
    nf = jnp.maximum(n, 1).astype(F32)
    large = max_exact + (jnp.log(nf * (1.0 / max_exact)) / math.log(MAX_DISTANCE / max_exact)
                         * (N_BUCKETS - max_exact)).astype(jnp.int32)
    large = jnp.minimum(large, N_BUCKETS - 1)
    return jnp.where(n < max_exact, n, large)


def _tables_kernel(rb_ref, tb_ref, tc_ref, wm_ref):
    def fill(out_ref, dist):
        bk = _t5_bucket(dist)
        for h in range(NSA_HEADS):
            far = rb_ref[N_BUCKETS - 1, h]
            val = jnp.zeros(dist.shape, F32)
            for b in range(N_BUCKETS - 1):
                val = jnp.where(bk == b, (rb_ref[b, h] - far) * LOG2E, val)
            out_ref[h] = jnp.where(dist >= 0, val, MASK_F32)

    tq = lax.broadcasted_iota(jnp.int32, (Q_BLOCK, 2 * Q_BLOCK), 0)
    j = lax.broadcasted_iota(jnp.int32, (Q_BLOCK, 2 * Q_BLOCK), 1)
    fill(tb_ref, tq + Q_BLOCK - j)
    tq = lax.broadcasted_iota(jnp.int32, (Q_BLOCK, LANE), 0)
    c = lax.broadcasted_iota(jnp.int32, (Q_BLOCK, LANE), 1)
    cmp_first = Q_BLOCK // CMP_STRIDE - LANE
    fill(tc_ref, tq - CMP_STRIDE * (cmp_first + c) - (CMP_BLOCK - 1))
    wm_ref[...] = jnp.where(c > tq, 0.0, MASK_F32)


def _bias_tables(rel_bias):
    return pl.pallas_call(
        _tables_kernel,
        in_specs=[pl.BlockSpec(memory_space=pltpu.SMEM)],
        out_shape=[jax.ShapeDtypeStruct((NSA_HEADS, Q_BLOCK, 2 * Q_BLOCK), F32),
                   jax.ShapeDtypeStruct((NSA_HEADS, Q_BLOCK, LANE), F32),
                   jax.ShapeDtypeStruct((Q_BLOCK, LANE), F32)],
        name="bias_tables",
    )(rel_bias)


def _nsa_kernel(q_ref, gt_ref, kc_ref, vc_ref, ks_ref, vs_ref, kw_ref, vw_ref,
                tb_ref, tc_ref, wm_ref, o_ref, qsel_ref, s0_ref, s1_ref, m0_ref, m1_ref,
                al0_ref, al1_ref, acc_ref, *, n_sel):
    i = pl.program_id(2)
    s_ref, m_ref, al_ref = (s0_ref, s1_ref), (m0_ref, m1_ref), (al0_ref, al1_ref)
    rows = NSA_REP * Q_BLOCK
    qa = q_ref[...].reshape(rows, LANE)
    tb = tb_ref[...].reshape(rows, 2 * Q_BLOCK)
    zero_tb = jnp.zeros((rows, KV_TILE - 2 * Q_BLOCK), F32)

    c0 = pl.multiple_of(8 * i + 8 + CMP_PAD - CMP_SPAN, 8)
    s = _dot_nt(qa, kc_ref[pl.ds(c0, CMP_SPAN), :].astype(BF16))
    s = s + jnp.concatenate([jnp.zeros((rows, CMP_SPAN - LANE), F32),
                             tc_ref[...].reshape(rows, LANE)], axis=1)
    m = jnp.maximum(jnp.max(s, axis=1, keepdims=True), MAX_CLAMP)
    acc = _dot(jnp.exp(s - m).astype(BF16), vc_ref[pl.ds(c0, CMP_SPAN), :].astype(BF16))
    inv = 1.0 / jnp.maximum(acc[:, HEAD_DIM:HEAD_DIM + 1], 1e-30)
    oc = acc[:, 0:LANE] * inv
    impu = acc[:, LANE:2 * LANE] * inv
    imp = impu[0:Q_BLOCK]
    for r in range(1, NSA_REP):
        imp = imp + impu[r * Q_BLOCK:(r + 1) * Q_BLOCK]

    t0 = pl.multiple_of(Q_BLOCK * (i + 1) + KV_PAD - KV_TILE, Q_BLOCK)
    w1 = pl.multiple_of(t0 - Q_BLOCK, Q_BLOCK)
    s0 = _dot_nt(qa, kw_ref[pl.ds(t0, KV_TILE), :]) + jnp.concatenate([zero_tb, tb], axis=1)
    s1 = _dot_nt(qa, kw_ref[pl.ds(w1, Q_BLOCK), :]) + jnp.tile(wm_ref[...], (NSA_REP, 1))
    m = jnp.maximum(jnp.max(s0, axis=1, keepdims=True), jnp.max(s1, axis=1, keepdims=True))
    acc = (_dot(jnp.exp(s0 - m).astype(BF16), vw_ref[pl.ds(t0, KV_TILE), :])
           + _dot(jnp.exp(s1 - m).astype(BF16), vw_ref[pl.ds(w1, Q_BLOCK), :]))
    ow = acc * (1.0 / acc[:, HEAD_DIM:HEAD_DIM + 1])

    imp_t = imp.T
    jb = lax.broadcasted_iota(jnp.int32, (LANE, Q_BLOCK), 0)
    t = i * Q_BLOCK + lax.broadcasted_iota(jnp.int32, (LANE, Q_BLOCK), 1)
    cur = t // SEL_BLOCK
    forced = (jb == 0) | (jb == cur) | (jb == cur - 1)
    score = jnp.where(forced, FORCE_SCORE, jnp.where(jb * SEL_BLOCK <= t, imp_t, -1.0))
    notsel = jnp.ones((LANE, Q_BLOCK), F32)
    for _ in range(n_sel):
        mx = jnp.max(score, axis=0, keepdims=True)
        first = jnp.min(jnp.where(score == mx, jb, LANE), axis=0, keepdims=True)
        hit = jb == first
        notsel = jnp.where(hit, 0.0, notsel)
        score = jnp.where(hit, -2.0, score)
    notsel_q = notsel.T.astype(BF16)
    qsel_ref[:, 0:LANE] = qa
    for r in range(NSA_REP):
        qsel_ref[r * Q_BLOCK:(r + 1) * Q_BLOCK, LANE:2 * LANE] = notsel_q
    qs = qsel_ref[...]

    n_tiles = (i + 1 + KV_TILE // Q_BLOCK - 1) // (KV_TILE // Q_BLOCK)

    def bcast(col):
        return jnp.broadcast_to(col, (rows, LANE))

    def lanes(x):
        return jnp.concatenate([x] * (KV_TILE // LANE), axis=1)

    def qk_stage(k, prev, slot):
        st = pl.multiple_of(t0 - KV_TILE * k, Q_BLOCK)
        s_n = _dot_nt(qs, ks_ref[pl.ds(st, KV_TILE), :])
        s_ref[slot][...] = s_n
        m_c = m_ref[prev][...]
        m_n = jnp.maximum(m_c, bcast(jnp.max(s_n, axis=1, keepdims=True)))
        m_ref[slot][...] = m_n
        al_ref[slot][...] = jnp.exp(m_c - m_n)

    def pv_stage(k, slot):
        st = pl.multiple_of(t0 - KV_TILE * k, Q_BLOCK)
        p = jnp.exp(s_ref[slot][...] - lanes(m_ref[slot][...]))
        acc_ref[...] = (al_ref[slot][...] * acc_ref[...]
                        + _dot(p.astype(BF16), vs_ref[pl.ds(st, KV_TILE), :]))

    s = _dot_nt(qs, ks_ref[pl.ds(t0, KV_TILE), :]) + jnp.concatenate([zero_tb, tb], axis=1)
    s_ref[0][...] = s
    m_ref[0][...] = bcast(jnp.max(s, axis=1, keepdims=True))
    al_ref[0][...] = jnp.ones((rows, LANE), F32)
    acc_ref[...] = jnp.zeros((rows, LANE), F32)

    def sel_body(j, carry):
        qk_stage(2 * j + 1, 0, 1)
        pv_stage(2 * j, 0)
        qk_stage(2 * j + 2, 1, 0)
        pv_stage(2 * j + 1, 1)
        return carry

    n_pairs = (n_tiles - 1) // 2
    lax.fori_loop(0, n_pairs, sel_body, 0)
    odd_left = n_tiles - 1 - 2 * n_pairs

    @pl.when(odd_left == 1)
    def _two_left():
        qk_stage(n_tiles - 1, 0, 1)
        pv_stage(n_tiles - 2, 0)
        pv_stage(n_tiles - 1, 1)

    @pl.when(odd_left == 0)
    def _one_left():
        pv_stage(n_tiles - 1, 0)

    acc = acc_ref[...]
    osl = acc * (1.0 / acc[:, HEAD_DIM:HEAD_DIM + 1])

    gts = gt_ref[...]
    lane = lax.broadcasted_iota(jnp.int32, (Q_BLOCK, LANE), 1)
    heads = []
    for r in range(NSA_REP):
        rs = slice(r * Q_BLOCK, (r + 1) * Q_BLOCK)
        heads.append(gts[:, r:r + 1] * oc[rs]
                     + gts[:, NSA_REP + r:NSA_REP + r + 1] * osl[rs]
                     + gts[:, 2 * NSA_REP + r:2 * NSA_REP + r + 1] * ow[rs])
    for pr in range(NSA_REP // 2):
        o_ref[:, pr * LANE:(pr + 1) * LANE] = jnp.where(
            lane < HEAD_DIM, heads[2 * pr], pltpu.roll(heads[2 * pr + 1], HEAD_DIM, 1))


def _nsa(q_aug, gates, cmp_aug, ksel, vsel, kwin, vwin, tb, tc, wm, batch, seq):
    nq = seq // Q_BLOCK
    sp = seq + KV_PAD
    crow = cmp_aug.shape[3]
    n_sel = min(N_SELECT, seq // SEL_BLOCK)
    assert seq // SEL_BLOCK <= LANE and KV_PAD >= KV_TILE and KV_PAD >= WINDOW
    assert seq // CMP_STRIDE <= CMP_SPAN <= CMP_PAD and crow == CMP_PAD + seq // CMP_STRIDE
    kv_spec = lambda w: pl.BlockSpec((None, None, sp, w), lambda b, g, i: (b, g, 0, 0))
    in_specs = [
        pl.BlockSpec((None, NSA_REP, Q_BLOCK, LANE), lambda b, g, i: (b, g, i, 0)),
        pl.BlockSpec((Q_BLOCK, LANE), lambda b, g, i: (b * nq + i, g)),
        pl.BlockSpec((None, None, None, crow, LANE), lambda b, g, i: (0, b, g, 0, 0)),
        pl.BlockSpec((None, None, None, crow, 2 * LANE), lambda b, g, i: (1, b, g, 0, 0)),
        kv_spec(2 * LANE), kv_spec(LANE), kv_spec(LANE), kv_spec(LANE),
        pl.BlockSpec((NSA_REP, Q_BLOCK, 2 * Q_BLOCK), lambda b, g, i: (g, 0, 0)),
        pl.BlockSpec((NSA_REP, Q_BLOCK, LANE), lambda b, g, i: (g, 0, 0)),
        pl.BlockSpec((Q_BLOCK, LANE), lambda b, g, i: (0, 0)),
    ]
    rows = NSA_REP * Q_BLOCK
    return pl.pallas_call(
        functools.partial(_nsa_kernel, n_sel=n_sel),
        grid=(batch, NSA_KV, nq),
        in_specs=in_specs,
        out_specs=pl.BlockSpec((Q_BLOCK, NSA_REP * HEAD_DIM), lambda b, g, i: (b * nq + i, g)),
        out_shape=jax.ShapeDtypeStruct((batch * seq, NSA_WIDTH), F32),
        scratch_shapes=[pltpu.VMEM((rows, 2 * LANE), BF16),
                        pltpu.VMEM((rows, KV_TILE), F32),
                        pltpu.VMEM((rows, KV_TILE), F32),
                        pltpu.VMEM((rows, LANE), F32),
                        pltpu.VMEM((rows, LANE), F32),
                        pltpu.VMEM((rows, LANE), F32),
                        pltpu.VMEM((rows, LANE), F32),
                        pltpu.VMEM((rows, LANE), F32)],
        compiler_params=_params(("arbitrary", "arbitrary", "arbitrary")),
        name="nsa",
    )(q_aug, gates, cmp_aug, cmp_aug, ksel, vsel, kwin, vwin, tb, tc, wm)


_N_GROUP_SCRATCH = 8


def _oddeven_merge_sort(n):
    def merge(lo, hi, r):
        step = 2 * r
        if step < hi - lo:
            yield from merge(lo, hi, step)
            yield from merge(lo + r, hi, step)
            yield from ((k, k + r) for k in range(lo + r, hi - r, step))
        else:
            yield (lo, lo + r)

    def sort(lo, hi):
        if hi > lo:
            mid = lo + (hi - lo) // 2
            yield from sort(lo, mid)
            yield from sort(mid + 1, hi)
            yield from merge(lo, hi, 1)

    return tuple(sort(0, n - 1))


def _bitonic_merge(n):
    pairs, k = [], n // 2
    while k >= 1:
        pairs += [(a, a + k) for a in range(n) if not a & k]
        k //= 2
    return tuple(pairs)


_SORT16 = _oddeven_merge_sort(N_SELECT)
_BITONIC16 = _bitonic_merge(N_SELECT)


def _nsa2_kernel(q_ref, gt_ref, kc_ref, vc_ref, ks_ref, vs_ref, kw_ref, vw_ref,
                 tb_ref, tc_ref, wm_ref, gx_ref, o_ref, *scratch, n_sel, select_all):
    i = pl.program_id(1)
    rows = NSA_REP * Q_BLOCK
    per_group = [scratch[g * _N_GROUP_SCRATCH:(g + 1) * _N_GROUP_SCRATCH] for g in range(NSA_KV)]
    qsel_ref = [s[0] for s in per_group]
    s_ref = [(s[1], s[2]) for s in per_group]
    m_ref = [(s[3], s[4]) for s in per_group]
    al_ref = [(s[5], s[6]) for s in per_group]
    acc_ref = [s[7] for s in per_group]

    zero_tb = jnp.zeros((rows, KV_TILE - 2 * Q_BLOCK), F32)
    c0 = pl.multiple_of(8 * i + 8 + CMP_PAD - CMP_SPAN, 8)
    t0 = pl.multiple_of(Q_BLOCK * (i + 1) + KV_PAD - KV_TILE, Q_BLOCK)
    w1 = pl.multiple_of(t0 - Q_BLOCK, Q_BLOCK)
    n_tiles = (i + 1 + KV_TILE // Q_BLOCK - 1) // (KV_TILE // Q_BLOCK)
    lane = lax.broadcasted_iota(jnp.int32, (Q_BLOCK, LANE), 1)

    def bcast(col):
        return jnp.broadcast_to(col, (rows, LANE))

    def lanes(x, n):
        return jnp.concatenate([x] * n, axis=1)

    def head_out(acc, r):
        rs = slice(r * Q_BLOCK, (r + 1) * Q_BLOCK)
        a, b = acc[rs, 0:LANE], acc[rs, LANE:2 * LANE]
        return a * (1.0 / b) if r % 2 == 0 else b * (1.0 / a)

    def heads_of(g):
        return slice(g * NSA_REP, (g + 1) * NSA_REP)

    def q_of(g):
        return q_ref[heads_of(g)].reshape(rows, LANE)

    def tile0_bias(g):
        return jnp.concatenate([zero_tb, tb_ref[heads_of(g)].reshape(rows, 2 * Q_BLOCK)], axis=1)

    def cmp_phase(g):
        hs = heads_of(g)
        s = _dot_nt(q_of(g), kc_ref[g, pl.ds(c0, CMP_SPAN), :].astype(BF16))
        s = s + jnp.concatenate([jnp.zeros((rows, CMP_SPAN - LANE), F32),
                                 tc_ref[hs].reshape(rows, LANE)], axis=1)
        m = jnp.maximum(jnp.max(s, axis=1, keepdims=True), MAX_CLAMP)
        acc = _dot(jnp.exp2(s - m).astype(BF16), vc_ref[g, pl.ds(c0, CMP_SPAN), :].astype(BF16))
        num_den = acc[:, 0:LANE]
        lane_r = lax.broadcasted_iota(jnp.int32, (rows, LANE), 1)
        den = jnp.where(lane_r < HEAD_DIM, pltpu.roll(num_den, HEAD_DIM, 1), num_den)
        inv = 1.0 / jnp.maximum(den, 1e-30)
        oc = num_den * inv
        impu = acc[:, LANE:2 * LANE] * inv
        imp = impu[0:Q_BLOCK]
        for r in range(1, NSA_REP):
            imp = imp + impu[r * Q_BLOCK:(r + 1) * Q_BLOCK]
        return oc, imp

    def win_phase(g):
        qa = q_of(g)
        s0 = _dot_nt(qa, kw_ref[g, pl.ds(t0, KV_TILE), :]) + tile0_bias(g)
        s1 = _dot_nt(qa, kw_ref[g, pl.ds(w1, Q_BLOCK), :]) + jnp.tile(wm_ref[...], (NSA_REP, 1))
        m = jnp.maximum(jnp.max(s0, axis=1, keepdims=True), jnp.max(s1, axis=1, keepdims=True))
        accw = (_dot(jnp.exp2(s0 - m).astype(BF16), vw_ref[g, pl.ds(t0, KV_TILE), :])
                + _dot(jnp.exp2(s1 - m).astype(BF16), vw_ref[g, pl.ds(w1, Q_BLOCK), :]))
        return [head_out(accw, r) for r in range(NSA_REP)]

    def sel_phase(g, imp):
        notsel_q = not_selected(imp).astype(BF16)
        qsel_ref[g][:, 0:LANE] = q_of(g)
        for r in range(NSA_REP):
            qsel_ref[g][r * Q_BLOCK:(r + 1) * Q_BLOCK, LANE:2 * LANE] = notsel_q
        s = _dot_nt(qsel_ref[g][...], ks_ref[g, pl.ds(t0, KV_TILE), :]) + tile0_bias(g)
        s_ref[g][0][...] = s
        m_ref[g][0][...] = bcast(jnp.max(s, axis=1, keepdims=True))
        al_ref[g][0][...] = jnp.ones((rows, LANE), F32)
        acc_ref[g][...] = jnp.zeros((rows, 2 * LANE), F32)

    def not_selected(imp):
        jb = lax.broadcasted_iota(jnp.int32, (LANE, Q_BLOCK), 0)
        t = i * Q_BLOCK + lax.broadcasted_iota(jnp.int32, (LANE, Q_BLOCK), 1)
        cur = t // SEL_BLOCK
        forced = (jb == 0) | (jb == cur) | (jb == cur - 1)
        score = jnp.where(forced, FORCE_SCORE, jnp.where(jb * SEL_BLOCK <= t, imp.T, -1.0))
        if select_all:
            return jnp.zeros((Q_BLOCK, LANE), F32)
        sub = 8
        x = [score[sub * v:sub * (v + 1), :] for v in range(LANE // sub)]
        y = list(x)
        for a, b in _SORT16:
            y[a], y[b] = jnp.maximum(y[a], y[b]), jnp.minimum(y[a], y[b])
        for shift in (4, 2, 1):
            z = [jnp.maximum(y[k], pltpu.roll(y[n_sel - 1 - k], shift, 0)) for k in range(n_sel)]
            if shift > 1:
                for a, b in _BITONIC16:
                    z[a], z[b] = jnp.maximum(z[a], z[b]), jnp.minimum(z[a], z[b])
            y = z
        while len(y) > 1:
            y = [jnp.minimum(y[2 * k], y[2 * k + 1]) for k in range(len(y) // 2)]
        tau = y[0]
        gt = jnp.concatenate([jnp.where(xv > tau, 1.0, 0.0) for xv in x], axis=0)
        eq = jnp.concatenate([jnp.where(xv == tau, 1.0, 0.0) for xv in x], axis=0)
        kb = lax.broadcasted_iota(jnp.int32, (LANE, LANE), 1)
        n_above = _dot(jnp.ones((LANE, LANE), BF16), gt.astype(BF16))
        n_tied_before = _dot(jnp.where(kb < jb, 1.0, 0.0).astype(BF16), eq.astype(BF16))
        keep = (gt > 0.0) | ((eq > 0.0) & (n_above + n_tied_before < n_sel))
        return jnp.where(keep, 0.0, 1.0).T


    def qk_stage(g, k, prev, slot):
        st = pl.multiple_of(t0 - KV_TILE * k, Q_BLOCK)
        s_n = _dot_nt(qsel_ref[g][...], ks_ref[g, pl.ds(st, KV_TILE), :])
        s_ref[g][slot][...] = s_n
        m_c = m_ref[g][prev][...]
        m_n = jnp.maximum(m_c, bcast(jnp.max(s_n, axis=1, keepdims=True)))
        m_ref[g][slot][...] = m_n
        al_ref[g][slot][...] = jnp.exp2(m_c - m_n)

    def pv_stage(g, k, slot):
        st = pl.multiple_of(t0 - KV_TILE * k, Q_BLOCK)
        p = jnp.exp2(s_ref[g][slot][...] - lanes(m_ref[g][slot][...], KV_TILE // LANE))
        acc_ref[g][...] = (lanes(al_ref[g][slot][...], 2) * acc_ref[g][...]
                           + _dot(p.astype(BF16), vs_ref[g, pl.ds(st, KV_TILE), :]))

    groups = range(NSA_KV)
    cmp_out = [cmp_phase(g) for g in groups]
    win_out = []
    for g in groups:
        sel_phase(g, cmp_out[g][1])
        win_out.append(win_phase(g))
    fronts = [(cmp_out[g][0], win_out[g]) for g in groups]

    def sel_body(j, carry):
        for g in groups:
            qk_stage(g, 2 * j + 1, 0, 1)
        for g in groups:
            pv_stage(g, 2 * j, 0)
        for g in groups:
            qk_stage(g, 2 * j + 2, 1, 0)
        for g in groups:
            pv_stage(g, 2 * j + 1, 1)
        return carry

    n_pairs = (n_tiles - 1) // 2
    lax.fori_loop(0, n_pairs, sel_body, 0)
    odd_left = n_tiles - 1 - 2 * n_pairs

    @pl.when(odd_left == 1)
    def _two_left():
        for g in groups:
            qk_stage(g, n_tiles - 1, 0, 1)
        for g in groups:
            pv_stage(g, n_tiles - 2, 0)
        for g in groups:
            pv_stage(g, n_tiles - 1, 1)

    @pl.when(odd_left == 0)
    def _one_left():
        for g in groups:
            pv_stage(g, n_tiles - 1, 0)

    for g in groups:
        oc, ow = fronts[g]
        acc = acc_ref[g][...]
        gx = _dot(gt_ref[...], gx_ref[g])

        def gate(branch, r):
            c = branch * NSA_REP + r
            return gx[:, c * LANE:(c + 1) * LANE]

        heads = []
        for r in range(NSA_REP):
            rs = slice(r * Q_BLOCK, (r + 1) * Q_BLOCK)
            oc_r = oc[rs] if r % 2 == 0 else pltpu.roll(oc[rs], HEAD_DIM, 1)
            heads.append(gate(0, r) * oc_r + gate(1, r) * head_out(acc, r) + gate(2, r) * ow[r])
        for pr in range(NSA_REP // 2):
            col = (g * NSA_REP // 2 + pr) * LANE
            o_ref[:, col:col + LANE] = jnp.where(lane < HEAD_DIM, heads[2 * pr], heads[2 * pr + 1])


def _nsa2(q_aug, gates, cmp_aug, ksel, vsel, kwin, vwin, tb, tc, wm, batch, seq):
    nq = seq // Q_BLOCK
    sp = seq + KV_PAD
    crow = cmp_aug.shape[3]
    n_sel = min(N_SELECT, seq // SEL_BLOCK)
    assert seq // SEL_BLOCK <= LANE and KV_PAD >= KV_TILE and KV_PAD >= WINDOW
    assert seq // CMP_STRIDE <= CMP_SPAN <= CMP_PAD and crow == CMP_PAD + seq // CMP_STRIDE
    assert seq // SEL_BLOCK <= N_SELECT or n_sel == LANE // 8
    n_gate = 3 * NSA_REP
    col = jnp.arange(n_gate * LANE) // LANE
    row = jnp.arange(2 * LANE) % LANE
    gate_spread = jnp.stack([(row[:, None] == g * n_gate + col[None, :]) for g in range(NSA_KV)]).astype(BF16)
    once = pl.Buffered(1)
    kv_spec = lambda w: pl.BlockSpec((None, NSA_KV, sp, w), lambda b, i: (b, 0, 0, 0), pipeline_mode=once)
    cmp_spec = lambda k, w: pl.BlockSpec((None, None, NSA_KV, crow, w), lambda b, i: (k, b, 0, 0, 0),
                                         pipeline_mode=once)
    in_specs = [
        pl.BlockSpec((None, NSA_HEADS, Q_BLOCK, LANE), lambda b, i: (b, 0, i, 0)),
        pl.BlockSpec((Q_BLOCK, 2 * LANE), lambda b, i: (b * nq + i, 0)),
        cmp_spec(0, LANE), cmp_spec(1, 2 * LANE),
        kv_spec(2 * LANE), kv_spec(2 * LANE), kv_spec(LANE), kv_spec(2 * LANE),
        _const_spec((NSA_HEADS, Q_BLOCK, 2 * Q_BLOCK)),
        _const_spec((NSA_HEADS, Q_BLOCK, LANE)),
        _const_spec((Q_BLOCK, LANE)),
        _const_spec((NSA_KV, 2 * LANE, n_gate * LANE)),
    ]
    rows = NSA_REP * Q_BLOCK
    group_scratch = [pltpu.VMEM((rows, 2 * LANE), BF16),
                     pltpu.VMEM((rows, KV_TILE), F32),
                     pltpu.VMEM((rows, KV_TILE), F32),
                     pltpu.VMEM((rows, LANE), F32),
                     pltpu.VMEM((rows, LANE), F32),
                     pltpu.VMEM((rows, LANE), F32),
                     pltpu.VMEM((rows, LANE), F32),
                     pltpu.VMEM((rows, 2 * LANE), F32)]
    assert len(group_scratch) == _N_GROUP_SCRATCH
    return pl.pallas_call(
        functools.partial(_nsa2_kernel, n_sel=n_sel, select_all=seq // SEL_BLOCK <= N_SELECT),
        grid=(batch, nq),
        in_specs=in_specs,
        out_specs=pl.BlockSpec((Q_BLOCK, NSA_WIDTH), lambda b, i: (b * nq + i, 0)),
        out_shape=jax.ShapeDtypeStruct((batch * seq, NSA_WIDTH), F32),
        scratch_shapes=group_scratch * NSA_KV,
        compiler_params=_params(("arbitrary", "arbitrary")),
        name="nsa",
    )(q_aug, gates, cmp_aug, cmp_aug, ksel, vsel, kwin, vwin, tb, tc, wm, gate_spread)


def _memkv_kernel(mem_ref, g_ref, wk_ref, wv_ref, k_ref, v_ref):
    h = _rms(mem_ref[...], g_ref[...]).astype(BF16)
    k_ref[...] = _dot(h, wk_ref[...]).astype(BF16)
    v_ref[...] = _dot(h, wv_ref[...]).astype(BF16)


def _mem_kv(mem, g_mem, wkv):
    batch, m, _ = mem.shape
    blk = pl.BlockSpec((None, m, D_MODEL), lambda b: (b, 0, 0))
    return pl.pallas_call(
        _memkv_kernel,
        grid=(batch,),
        in_specs=[blk, _const_spec((1, D_MODEL)), _const_spec((D_MODEL, D_MODEL)),
                  _const_spec((D_MODEL, D_MODEL))],
        out_specs=[blk, blk],
        out_shape=[jax.ShapeDtypeStruct((batch, m, D_MODEL), BF16)] * 2,
        compiler_params=_params(("arbitrary",)),
        name="mem_kv",
    )(mem, g_mem.reshape(1, -1), wkv[:, :D_MODEL].astype(BF16), wkv[:, D_MODEL:].astype(BF16))


def _mixout_kernel(x_ref, ya_ref, yb_ref, gb_ref, wo_ref, post_ref, xpre_ref, wq_ref,
                   km_ref, vm_ref, wxo_ref, xpost_ref, o_ref):
    yb = _rms(yb_ref[...], gb_ref[...]).astype(BF16)
    y = _dot(ya_ref[...].astype(BF16), wo_ref[0:GM_WIDTH, :]) + _dot(yb, wo_ref[GM_WIDTH:, :])
    x1 = x_ref[...] + _rms(y, post_ref[...])
    h = _rms(x1, xpre_ref[...]).astype(BF16)
    q = _dot(h, wq_ref[...]).astype(BF16)
    outs = []
    for hd in range(XA_HEADS):
        hs = slice(hd * XA_HEAD_DIM, (hd + 1) * XA_HEAD_DIM)
        s = _dot_nt(q[:, hs], km_ref[:, hs])
        e = jnp.exp(s - jnp.max(s, axis=1, keepdims=True))
        o = _dot(e.astype(BF16), vm_ref[:, hs]) / jnp.sum(e, axis=1, keepdims=True)
        outs.append(o.astype(BF16))
    o = _dot(jnp.concatenate(outs, axis=1), wxo_ref[...])
    o_ref[...] = x1 + _rms(o, xpost_ref[...])


def _mix_out(x2d, ya, yb, batch, seq, gain_b, w_out, mix_post, xa_pre, wq, kmem, vmem, wo, xa_post):
    tm = TOKEN_TILE
    nt = seq // tm
    row = lambda w: pl.BlockSpec((tm, w), lambda b, t: (b * nt + t, 0))
    mem_spec = pl.BlockSpec((None, kmem.shape[1], D_MODEL), lambda b, t: (b, 0, 0))
    sq = _const_spec((D_MODEL, D_MODEL))
    vec = _const_spec((1, D_MODEL))
    return pl.pallas_call(
        _mixout_kernel,
        grid=(batch, nt),
        in_specs=[row(D_MODEL), row(GM_WIDTH), row(NSA_WIDTH), _const_spec((1, NSA_WIDTH)), sq, vec,
                  vec, sq, mem_spec, mem_spec, sq, vec],
        out_specs=row(D_MODEL),
        out_shape=jax.ShapeDtypeStruct(x2d.shape, F32),
        compiler_params=_params(("arbitrary", "arbitrary")),
        name="mix_out",
    )(x2d, ya, yb, gain_b.reshape(1, -1), w_out.astype(BF16), mix_post.reshape(1, -1),
      xa_pre.reshape(1, -1), (wq * XA_HEAD_DIM ** -0.5).astype(BF16), kmem, vmem,
      wo.astype(BF16), xa_post.reshape(1, -1))


def kernel(x, mem, ffn1_pre, ffn1_post, ffn1_wg, ffn1_wu, ffn1_wd, mix_pre, mix_post, w_in, gm_ln_g, gm_ln_b, gm_ws, gm_bs, ck_pe, ck_w1, ck_b1, ck_w2, cv_pe, cv_w1, cv_b1, cv_w2, rel_bias, out_gain_a, out_gain_b, w_out, xa_pre, xa_post, mem_norm, xa_wq, xa_wkv, xa_wo, ffn2_pre, ffn2_post, ffn2_wg, ffn2_wu, ffn2_wd):
    batch, seq, _ = x.shape
    depth = ffn1_pre.shape[0]
    x2d = x.reshape(batch * seq, D_MODEL)
    tb, tc, wm = _bias_tables(rel_bias)
    for l in range(depth):
        x2d = _ffn(x2d, ffn1_pre[l], ffn1_wg[l], ffn1_wu[l], ffn1_wd[l], ffn1_post[l])
        ya, q_aug, cmp_raw, ksel, vsel, kwin, vwin, gates = _mixproj(
            x2d, batch, seq, mix_pre[l], w_in[l], gm_ln_g[l], gm_ln_b[l], gm_ws[l], gm_bs[l],
            out_gain_a[l])
        cmp_aug = _compress(cmp_raw, batch, seq,
                            jnp.stack([ck_pe[l], cv_pe[l]]), jnp.stack([ck_w1[l], cv_w1[l]]),
                            jnp.stack([ck_b1[l], cv_b1[l]]), jnp.stack([ck_w2[l], cv_w2[l]]))
        yb = _nsa2(q_aug, gates, cmp_aug, ksel, vsel, kwin, vwin, tb, tc, wm, batch, seq)
        kmem, vmem = _mem_kv(mem, mem_norm[l], xa_wkv[l])
        x2d = _mix_out(x2d, ya, yb, batch, seq, out_gain_b[l], w_out[l], mix_post[l], xa_pre[l],
                       xa_wq[l], kmem, vmem, xa_wo[l], xa_post[l])
        x2d = _ffn(x2d, ffn2_pre[l], ffn2_wg[l], ffn2_wu[l], ffn2_wd[l], ffn2_post[l])
    return x2d.reshape(batch, seq, D_MODEL)
```

```python
import functools
import math

import jax
import jax.numpy as jnp
from jax import lax
from jax.experimental import pallas as pl
from jax.experimental.pallas import tpu as pltpu

F32 = jnp.float32
BF16 = jnp.bfloat16

D_MODEL = 1024
MEM_LEN = 256
EPS = 1e-6
D_FF = 2816
GM_WIDTH = 512
GM_GROUPS = 4
GM_CH = GM_WIDTH // GM_GROUPS
CHUNK = 128
NSA_WIDTH = D_MODEL - GM_WIDTH
HEAD_DIM = 64
NSA_HEADS = NSA_WIDTH // HEAD_DIM
NSA_KV = 2
NSA_REP = NSA_HEADS // NSA_KV
KV_W = NSA_KV * HEAD_DIM
CMP_BLOCK = 32
CMP_STRIDE = 16
CMP_HIDDEN = 256
SEL_BLOCK = 64
N_SELECT = 16
WINDOW = 512
Q_BLOCK = 128
N_BUCKETS = 32
MAX_DISTANCE = 128
XA_HEADS = 4
XA_HEAD_DIM = D_MODEL // XA_HEADS
FORCE_SCORE = 1e4

LANE = 128
TOKEN_TILE = 512
FF_CHUNK = 256
KV_PAD = 512
KV_TILE = 512
CMP_PAD = 512
CMP_SPAN = 512
MASK_BF16 = -(2.0 ** 100)
MASK_F32 = -1e30
MAX_CLAMP = -(2.0 ** 99)
LOG2E = math.log2(math.e)
VMEM_LIMIT = 56 * 1024 * 1024

_NT = (((1,), (1,)), ((), ()))


def _rms(x, g):
    ms = jnp.mean(x * x, axis=-1, keepdims=True)
    return x * lax.rsqrt(ms + EPS) * g


def _dot(a, b):
    return jnp.dot(a, b, preferred_element_type=F32)


def _dot_nt(a, b):
    return lax.dot_general(a, b, _NT, preferred_element_type=F32)


def _params(sem, vmem=VMEM_LIMIT):
    return pltpu.CompilerParams(dimension_semantics=sem, vmem_limit_bytes=vmem)


def _const_spec(shape):
    nd = len(shape)
    return pl.BlockSpec(shape, lambda *_: (0,) * nd)


def _ffn_kernel(x_ref, pre_ref, wg_ref, wu_ref, wd_ref, post_ref, o_ref, acc_ref):
    x = x_ref[...]
    h = _rms(x, pre_ref[...]).astype(BF16)
    for c in range(D_FF // FF_CHUNK):
        sl = slice(c * FF_CHUNK, (c + 1) * FF_CHUNK)
        g = _dot(h, wg_ref[:, sl])
        u = _dot(h, wu_ref[:, sl])
        a = (jax.nn.silu(g) * u).astype(BF16)
        d = _dot(a, wd_ref[sl, :])
        if c == 0:
            acc_ref[...] = d
        else:
            acc_ref[...] += d
    o_ref[...] = x + 0.5 * _rms(acc_ref[...], post_ref[...])


def _ffn(x2d, pre, wg, wu, wd, post):
    n = x2d.shape[0]
    tm = TOKEN_TILE
    row = pl.BlockSpec((tm, D_MODEL), lambda t: (t, 0))
    return pl.pallas_call(
        _ffn_kernel,
        grid=(n // tm,),
        in_specs=[row, _const_spec((1, D_MODEL)), _const_spec((D_MODEL, D_FF)),
                  _const_spec((D_MODEL, D_FF)), _const_spec((D_FF, D_MODEL)),
                  _const_spec((1, D_MODEL))],
        out_specs=row,
        out_shape=jax.ShapeDtypeStruct((n, D_MODEL), F32),
        scratch_shapes=[pltpu.VMEM((tm, D_MODEL), F32)],
        compiler_params=_params(("arbitrary",)),
        name="ffn",
    )(x2d, pre.reshape(1, -1), wg.astype(BF16), wu.astype(BF16), wd.astype(BF16),
      post.reshape(1, -1))


def _mixproj_kernel(x_ref, pre_ref, w_ref, wgt_ref, lng_ref, lnb_ref, ws_ref, bs_ref, ga_ref,
                    ya_ref, q_ref, cmp_ref, ksel_ref, vsel_ref, kwin_ref, vwin_ref, gt_ref, cmps_ref):
    t = pl.program_id(1)
    tm = x_ref.shape[0]
    lane = lax.broadcasted_iota(jnp.int32, (tm, LANE), 1)
    flag_col = jnp.where(lane == HEAD_DIM, MASK_BF16, 0.0).astype(BF16)

    @pl.when(t == 0)
    def _pad_rows():
        for g in range(NSA_KV):
            ksel_ref[g, :, 0:LANE] = flag_col
            ksel_ref[g, :, LANE:2 * LANE] = jnp.zeros((tm, LANE), BF16)
            kwin_ref[g] = flag_col
            vsel_ref[g] = jnp.zeros((tm, 2 * LANE), BF16)
            vwin_ref[g] = jnp.zeros((tm, 2 * LANE), BF16)

    @pl.when(t > 0)
    def _tile():
        h = _rms(x_ref[...], pre_ref[...]).astype(BF16)
        u = jax.nn.gelu(_dot(h, w_ref[:, 0:GM_WIDTH]))
        v = jax.nn.gelu(_dot(h, w_ref[:, GM_WIDTH:2 * GM_WIDTH]))
        r_i = lax.broadcasted_iota(jnp.int32, (CHUNK, CHUNK), 0)
        c_i = lax.broadcasted_iota(jnp.int32, (CHUNK, CHUNK), 1)
        for g in range(GM_GROUPS):
            gs = slice(g * GM_CH, (g + 1) * GM_CH)
            vg = v[:, gs]
            mu = jnp.mean(vg, axis=-1, keepdims=True)
            var = jnp.mean(jnp.square(vg - mu), axis=-1, keepdims=True)
            vn = ((vg - mu) * lax.rsqrt(var + EPS) * lng_ref[:, gs] + lnb_ref[:, gs]).astype(BF16)
            w = jnp.where(c_i <= r_i, ws_ref[g], 0.0).astype(BF16)
            for n in range(tm // CHUNK):
                rs = slice(n * CHUNK, (n + 1) * CHUNK)
                s = _dot(w, vn[rs]) + bs_ref[:, g:g + 1]
                ya_ref[rs, gs] = u[rs, gs] * s
        ya_ref[...] = _rms(ya_ref[...], ga_ref[...])
        lane_lo = lane < HEAD_DIM

        def halves(z):
            return z, pltpu.roll(z, HEAD_DIM, 1)

        one_col = jnp.where(lane == HEAD_DIM, 1.0, 0.0)
        zq = _dot(h, w_ref[:, _O_Q:_O_C])
        for a in range(NSA_HEADS // 2):
            for hd, qh in zip((2 * a, 2 * a + 1), halves(zq[:, a * LANE:(a + 1) * LANE])):
                q_ref[hd] = jnp.where(lane_lo, qh, one_col).astype(BF16)
        zc = _dot(h, w_ref[:, _O_C:_O_KV])
        nr = tm // CMP_STRIDE
        lo32 = lax.broadcasted_iota(jnp.int32, (nr, LANE), 1) < HEAD_DIM
        for kv in range(2):
            cmps_ref[kv] = zc[:, kv * KV_W:(kv + 1) * KV_W]
            for a in range(CMP_STRIDE // 2):
                za = cmps_ref[kv, pl.ds(2 * a, nr, stride=CMP_STRIDE), :]
                zb = cmps_ref[kv, pl.ds(2 * a + 1, nr, stride=CMP_STRIDE), :]
                cmp_ref[kv, :, a * LANE:(a + 1) * LANE] = jnp.where(
                    lo32, za, pltpu.roll(zb, HEAD_DIM, 1))
                cmp_ref[kv, :, (CMP_STRIDE // 2 + a) * LANE:(CMP_STRIDE // 2 + a + 1) * LANE] = jnp.where(
                    lo32, pltpu.roll(za, HEAD_DIM, 1), zb)
        zkv = _dot(h, w_ref[:, _O_KV:_O_GATE])
        k_sel, v_sel, k_win, v_win = (halves(zkv[:, j * LANE:(j + 1) * LANE]) for j in range(4))
        pos_blk = ((t - 1) * tm + lax.broadcasted_iota(jnp.int32, (tm, LANE), 0)) // SEL_BLOCK
        blk_onehot = jnp.where(lane == pos_blk, MASK_BF16, 0.0).astype(BF16)

        def v_aug(v, g):
            return jnp.concatenate([jnp.where(lane_lo, v[g], 1.0), jnp.where(lane_lo, 1.0, v[1 - g])],
                                   axis=1).astype(BF16)

        for g in range(NSA_KV):
            ksel_ref[g, :, 0:LANE] = jnp.where(lane_lo, k_sel[g], 0.0).astype(BF16)
            ksel_ref[g, :, LANE:2 * LANE] = blk_onehot
            vsel_ref[g] = v_aug(v_sel, g)
            kwin_ref[g] = jnp.where(lane_lo, k_win[g], 0.0).astype(BF16)
            vwin_ref[g] = v_aug(v_win, g)
        sg = jax.nn.sigmoid(_dot(h, wgt_ref[...]))
        hi = sg.astype(BF16)
        gt_ref[:, 0:LANE] = hi
        gt_ref[:, LANE:2 * LANE] = (sg - hi.astype(F32)).astype(BF16)


_O_Q = 2 * GM_WIDTH
_O_C = _O_Q + NSA_WIDTH
_O_KV = _O_C + 2 * KV_W
_O_GATE = _O_KV + 4 * KV_W


def _mixproj(x2d, batch, seq, mix_pre, w_in, ln_g, ln_b, ws, bs, gain_a):
    tm = TOKEN_TILE
    nt = seq // tm
    assert seq % tm == 0 and KV_PAD == tm
    scale = HEAD_DIM ** -0.5 * LOG2E
    w_main = jnp.concatenate([w_in[:, :_O_Q], w_in[:, _O_Q:_O_C] * scale, w_in[:, _O_C:_O_GATE]],
                             axis=1).astype(BF16)
    n_gate = 3 * NSA_HEADS
    w_g = w_in[:, _O_GATE:].reshape(D_MODEL, NSA_KV, NSA_REP, 3).transpose(0, 1, 3, 2)
    w_g = jnp.pad(w_g.reshape(D_MODEL, n_gate), ((0, 0), (0, LANE - n_gate))).astype(BF16)

    def tile(b, t):
        return b * nt + jnp.maximum(t - 1, 0)

    n = batch * seq
    sp = seq + KV_PAD
    out_shape = [
        jax.ShapeDtypeStruct((n, GM_WIDTH), F32),
        jax.ShapeDtypeStruct((batch, NSA_HEADS, seq, LANE), BF16),
        jax.ShapeDtypeStruct((2, n // CMP_STRIDE, CMP_STRIDE * KV_W), F32),
        jax.ShapeDtypeStruct((batch, NSA_KV, sp, 2 * LANE), BF16),
        jax.ShapeDtypeStruct((batch, NSA_KV, sp, 2 * LANE), BF16),
        jax.ShapeDtypeStruct((batch, NSA_KV, sp, LANE), BF16),
        jax.ShapeDtypeStruct((batch, NSA_KV, sp, 2 * LANE), BF16),
        jax.ShapeDtypeStruct((n, 2 * LANE), BF16),
    ]
    padded = lambda w: pl.BlockSpec((None, NSA_KV, tm, w), lambda b, t: (b, 0, t, 0))
    out_specs = [
        pl.BlockSpec((tm, GM_WIDTH), lambda b, t: (tile(b, t), 0)),
        pl.BlockSpec((None, NSA_HEADS, tm, LANE), lambda b, t: (b, 0, jnp.maximum(t - 1, 0), 0)),
        pl.BlockSpec((2, tm // CMP_STRIDE, CMP_STRIDE * KV_W), lambda b, t: (0, tile(b, t), 0)),
        padded(2 * LANE), padded(2 * LANE), padded(LANE), padded(2 * LANE),
        pl.BlockSpec((tm, 2 * LANE), lambda b, t: (tile(b, t), 0)),
    ]
    in_specs = [
        pl.BlockSpec((tm, D_MODEL), lambda b, t: (tile(b, t), 0)),
        _const_spec((1, D_MODEL)),
        _const_spec(w_main.shape), _const_spec(w_g.shape),
        _const_spec((1, GM_WIDTH)), _const_spec((1, GM_WIDTH)),
        _const_spec((GM_GROUPS, CHUNK, CHUNK)), _const_spec((CHUNK, GM_GROUPS)),
        _const_spec((1, GM_WIDTH)),
    ]
    return pl.pallas_call(
        _mixproj_kernel,
        grid=(batch, nt + 1),
        in_specs=in_specs,
        out_specs=out_specs,
        out_shape=out_shape,
        scratch_shapes=[pltpu.VMEM((2, tm, KV_W), F32)],
        compiler_params=_params(("arbitrary", "arbitrary")),
        name="mixproj",
    )(x2d, mix_pre.reshape(1, -1), w_main, w_g, ln_g.reshape(1, -1), ln_b.reshape(1, -1),
      ws, bs.T, gain_a.reshape(1, -1))


def _compress_kernel(raw_ref, pe_ref, w1_ref, b1_ref, w2_ref, o_ref):
    is_v = pl.program_id(0) == 1
    nu = raw_ref.shape[0]
    half = CMP_STRIDE * HEAD_DIM
    outs = []
    for g in range(NSA_KV):
        u = raw_ref[:, g * half:(g + 1) * half]
        a = _dot((u + pe_ref[0]).astype(BF16), w1_ref[0:half, :])
        b = _dot((u + pe_ref[1]).astype(BF16), w1_ref[half:2 * half, :])
        hid = jax.nn.gelu(a + pltpu.roll(b, nu - 1, 0) + b1_ref[...])
        outs.append(_dot(hid.astype(BF16), w2_ref[...]))
    lane = lax.broadcasted_iota(jnp.int32, (nu, LANE), 1)
    ci = lax.broadcasted_iota(jnp.int32, (nu, LANE), 0)
    u_c = CMP_BLOCK // CMP_STRIDE
    u_s = SEL_BLOCK // CMP_STRIDE
    overlap = jnp.maximum(jnp.minimum(ci + u_c, u_s * (lane + 1)) - jnp.maximum(ci, u_s * lane), 0)
    extra = jnp.where(is_v, overlap.astype(F32), 0.0)
    ones_hi = jnp.where(jnp.logical_and(is_v, lane >= HEAD_DIM), 1.0, 0.0)
    lane_p = lax.broadcasted_iota(jnp.int32, (CMP_PAD, LANE), 1)
    pad_lo = jnp.where(jnp.logical_and(jnp.logical_not(is_v), lane_p == HEAD_DIM), MASK_BF16, 0.0)
    for g in range(NSA_KV):
        o_ref[g, 0:CMP_PAD, 0:LANE] = pad_lo
        o_ref[g, 0:CMP_PAD, LANE:2 * LANE] = jnp.zeros((CMP_PAD, LANE), F32)
        o_ref[g, CMP_PAD:CMP_PAD + nu, 0:LANE] = outs[g] + ones_hi
        o_ref[g, CMP_PAD:CMP_PAD + nu, LANE:2 * LANE] = extra


def _compress(cmp_raw, batch, seq, pe, w1, b1, w2):
    nu = seq // CMP_STRIDE
    half = CMP_STRIDE * HEAD_DIM
    kd = NSA_KV * half
    raw = cmp_raw.reshape(2, batch, nu, kd)
    w2p = jnp.pad(w2, ((0, 0), (0, 0), (0, LANE - HEAD_DIM)))
    rows = CMP_PAD + nu
    return pl.pallas_call(
        _compress_kernel,
        grid=(2, batch),
        in_specs=[
            pl.BlockSpec((None, None, nu, kd), lambda k, b: (k, b, 0, 0)),
            pl.BlockSpec((None, 2, 1, half), lambda k, b: (k, 0, 0, 0)),
            pl.BlockSpec((None, 2 * half, CMP_HIDDEN), lambda k, b: (k, 0, 0)),
            pl.BlockSpec((None, 1, CMP_HIDDEN), lambda k, b: (k, 0, 0)),
            pl.BlockSpec((None, CMP_HIDDEN, LANE), lambda k, b: (k, 0, 0)),
        ],
        out_specs=pl.BlockSpec((None, None, NSA_KV, rows, 2 * LANE), lambda k, b: (k, b, 0, 0, 0)),
        out_shape=jax.ShapeDtypeStruct((2, batch, NSA_KV, rows, 2 * LANE), F32),
        compiler_params=_params(("arbitrary", "arbitrary")),
        name="compress",
    )(raw, pe.reshape(2, 2, 1, half), w1.astype(BF16), b1.reshape(2, 1, CMP_HIDDEN), w2p.astype(BF16))


def _t5_bucket(dist):
    n = jnp.maximum(dist, 0)
    max_exact = N_BUCKETS // 2
    nf = jnp.maximum(n, 1).astype(F32)
    large = max_exact + (jnp.log(nf * (1.0 / max_exact)) / math.log(MAX_DISTANCE / max_exact)
                         * (N_BUCKETS - max_exact)).astype(jnp.int32)
    large = jnp.minimum(large, N_BUCKETS - 1)
    return jnp.where(n < max_exact, n, large)


def _tables_kernel(rb_ref, tb_ref, tc_ref, wm_ref):
    def fill(out_ref, dist):
        bk = _t5_bucket(dist)
        for h in range(NSA_HEADS):
            far = rb_ref[N_BUCKETS - 1, h]
            val = jnp.zeros(dist.shape, F32)
            for b in range(N_BUCKETS - 1):
                val = jnp.where(bk == b, (rb_ref[b, h] - far) * LOG2E, val)
            out_ref[h] = jnp.where(dist >= 0, val, MASK_F32)

    tq = lax.broadcasted_iota(jnp.int32, (Q_BLOCK, 2 * Q_BLOCK), 0)
    j = lax.broadcasted_iota(jnp.int32, (Q_BLOCK, 2 * Q_BLOCK), 1)
    fill(tb_ref, tq + Q_BLOCK - j)
    tq = lax.broadcasted_iota(jnp.int32, (Q_BLOCK, LANE), 0)
    c = lax.broadcasted_iota(jnp.int32, (Q_BLOCK, LANE), 1)
    cmp_first = Q_BLOCK // CMP_STRIDE - LANE
    fill(tc_ref, tq - CMP_STRIDE * (cmp_first + c) - (CMP_BLOCK - 1))
    wm_ref[...] = jnp.where(c > tq, 0.0, MASK_F32)


def _bias_tables(rel_bias):
    return pl.pallas_call(
        _tables_kernel,
        in_specs=[pl.BlockSpec(memory_space=pltpu.SMEM)],
        out_shape=[jax.ShapeDtypeStruct((NSA_HEADS, Q_BLOCK, 2 * Q_BLOCK), F32),
                   jax.ShapeDtypeStruct((NSA_HEADS, Q_BLOCK, LANE), F32),
                   jax.ShapeDtypeStruct((Q_BLOCK, LANE), F32)],
        name="bias_tables",
    )(rel_bias)


def _nsa_kernel(q_ref, gt_ref, kc_ref, vc_ref, ks_ref, vs_ref, kw_ref, vw_ref,
                tb_ref, tc_ref, wm_ref, o_ref, qsel_ref, s0_ref, s1_ref, m0_ref, m1_ref,
                al0_ref, al1_ref, acc_ref, *, n_sel):
    i = pl.program_id(2)
    s_ref, m_ref, al_ref = (s0_ref, s1_ref), (m0_ref, m1_ref), (al0_ref, al1_ref)
    rows = NSA_REP * Q_BLOCK
    qa = q_ref[...].reshape(rows, LANE)
    tb = tb_ref[...].reshape(rows, 2 * Q_BLOCK)
    zero_tb = jnp.zeros((rows, KV_TILE - 2 * Q_BLOCK), F32)

    c0 = pl.multiple_of(8 * i + 8 + CMP_PAD - CMP_SPAN, 8)
    s = _dot_nt(qa, kc_ref[pl.ds(c0, CMP_SPAN), :].astype(BF16))
    s = s + jnp.concatenate([jnp.zeros((rows, CMP_SPAN - LANE), F32),
                             tc_ref[...].reshape(rows, LANE)], axis=1)
    m = jnp.maximum(jnp.max(s, axis=1, keepdims=True), MAX_CLAMP)
    acc = _dot(jnp.exp(s - m).astype(BF16), vc_ref[pl.ds(c0, CMP_SPAN), :].astype(BF16))
    inv = 1.0 / jnp.maximum(acc[:, HEAD_DIM:HEAD_DIM + 1], 1e-30)
    oc = acc[:, 0:LANE] * inv
    impu = acc[:, LANE:2 * LANE] * inv
    imp = impu[0:Q_BLOCK]
    for r in range(1, NSA_REP):
        imp = imp + impu[r * Q_BLOCK:(r + 1) * Q_BLOCK]

    t0 = pl.multiple_of(Q_BLOCK * (i + 1) + KV_PAD - KV_TILE, Q_BLOCK)
    w1 = pl.multiple_of(t0 - Q_BLOCK, Q_BLOCK)
    s0 = _dot_nt(qa, kw_ref[pl.ds(t0, KV_TILE), :]) + jnp.concatenate([zero_tb, tb], axis=1)
    s1 = _dot_nt(qa, kw_ref[pl.ds(w1, Q_BLOCK), :]) + jnp.tile(wm_ref[...], (NSA_REP, 1))
    m = jnp.maximum(jnp.max(s0, axis=1, keepdims=True), jnp.max(s1, axis=1, keepdims=True))
    acc = (_dot(jnp.exp(s0 - m).astype(BF16), vw_ref[pl.ds(t0, KV_TILE), :])
           + _dot(jnp.exp(s1 - m).astype(BF16), vw_ref[pl.ds(w1, Q_BLOCK), :]))
    ow = acc * (1.0 / acc[:, HEAD_DIM:HEAD_DIM + 1])

    imp_t = imp.T
    jb = lax.broadcasted_iota(jnp.int32, (LANE, Q_BLOCK), 0)
    t = i * Q_BLOCK + lax.broadcasted_iota(jnp.int32, (LANE, Q_BLOCK), 1)
    cur = t // SEL_BLOCK
    forced = (jb == 0) | (jb == cur) | (jb == cur - 1)
    score = jnp.where(forced, FORCE_SCORE, jnp.where(jb * SEL_BLOCK <= t, imp_t, -1.0))
    notsel = jnp.ones((LANE, Q_BLOCK), F32)
    for _ in range(n_sel):
        mx = jnp.max(score, axis=0, keepdims=True)
        first = jnp.min(jnp.where(score == mx, jb, LANE), axis=0, keepdims=True)
        hit = jb == first
        notsel = jnp.where(hit, 0.0, notsel)
        score = jnp.where(hit, -2.0, score)
    notsel_q = notsel.T.astype(BF16)
    qsel_ref[:, 0:LANE] = qa
    for r in range(NSA_REP):
        qsel_ref[r * Q_BLOCK:(r + 1) * Q_BLOCK, LANE:2 * LANE] = notsel_q
    qs = qsel_ref[...]

    n_tiles = (i + 1 + KV_TILE // Q_BLOCK - 1) // (KV_TILE // Q_BLOCK)

    def bcast(col):
        return jnp.broadcast_to(col, (rows, LANE))

    def lanes(x):
        return jnp.concatenate([x] * (KV_TILE // LANE), axis=1)

    def qk_stage(k, prev, slot):
        st = pl.multiple_of(t0 - KV_TILE * k, Q_BLOCK)
        s_n = _dot_nt(qs, ks_ref[pl.ds(st, KV_TILE), :])
        s_ref[slot][...] = s_n
        m_c = m_ref[prev][...]
        m_n = jnp.maximum(m_c, bcast(jnp.max(s_n, axis=1, keepdims=True)))
        m_ref[slot][...] = m_n
        al_ref[slot][...] = jnp.exp(m_c - m_n)

    def pv_stage(k, slot):
        st = pl.multiple_of(t0 - KV_TILE * k, Q_BLOCK)
        p = jnp.exp(s_ref[slot][...] - lanes(m_ref[slot][...]))
        acc_ref[...] = (al_ref[slot][...] * acc_ref[...]
                        + _dot(p.astype(BF16), vs_ref[pl.ds(st, KV_TILE), :]))

    s = _dot_nt(qs, ks_ref[pl.ds(t0, KV_TILE), :]) + jnp.concatenate([zero_tb, tb], axis=1)
    s_ref[0][...] = s
    m_ref[0][...] = bcast(jnp.max(s, axis=1, keepdims=True))
    al_ref[0][...] = jnp.ones((rows, LANE), F32)
    acc_ref[...] = jnp.zeros((rows, LANE), F32)

    def sel_body(j, carry):
        qk_stage(2 * j + 1, 0, 1)
        pv_stage(2 * j, 0)
        qk_stage(2 * j + 2, 1, 0)
        pv_stage(2 * j + 1, 1)
        return carry

    n_pairs = (n_tiles - 1) // 2
    lax.fori_loop(0, n_pairs, sel_body, 0)
    odd_left = n_tiles - 1 - 2 * n_pairs

    @pl.when(odd_left == 1)
    def _two_left():
        qk_stage(n_tiles - 1, 0, 1)
        pv_stage(n_tiles - 2, 0)
        pv_stage(n_tiles - 1, 1)

    @pl.when(odd_left == 0)
    def _one_left():
        pv_stage(n_tiles - 1, 0)

    acc = acc_ref[...]
    osl = acc * (1.0 / acc[:, HEAD_DIM:HEAD_DIM + 1])

    gts = gt_ref[...]
    lane = lax.broadcasted_iota(jnp.int32, (Q_BLOCK, LANE), 1)
    heads = []
    for r in range(NSA_REP):
        rs = slice(r * Q_BLOCK, (r + 1) * Q_BLOCK)
        heads.append(gts[:, r:r + 1] * oc[rs]
                     + gts[:, NSA_REP + r:NSA_REP + r + 1] * osl[rs]
                     + gts[:, 2 * NSA_REP + r:2 * NSA_REP + r + 1] * ow[rs])
    for pr in range(NSA_REP // 2):
        o_ref[:, pr * LANE:(pr + 1) * LANE] = jnp.where(
            lane < HEAD_DIM, heads[2 * pr], pltpu.roll(heads[2 * pr + 1], HEAD_DIM, 1))


def _nsa(q_aug, gates, cmp_aug, ksel, vsel, kwin, vwin, tb, tc, wm, batch, seq):
    nq = seq // Q_BLOCK
    sp = seq + KV_PAD
    crow = cmp_aug.shape[3]
    n_sel = min(N_SELECT, seq // SEL_BLOCK)
    assert seq // SEL_BLOCK <= LANE and KV_PAD >= KV_TILE and KV_PAD >= WINDOW
    assert seq // CMP_STRIDE <= CMP_SPAN <= CMP_PAD and crow == CMP_PAD + seq // CMP_STRIDE
    kv_spec = lambda w: pl.BlockSpec((None, None, sp, w), lambda b, g, i: (b, g, 0, 0))
    in_specs = [
        pl.BlockSpec((None, NSA_REP, Q_BLOCK, LANE), lambda b, g, i: (b, g, i, 0)),
        pl.BlockSpec((Q_BLOCK, LANE), lambda b, g, i: (b * nq + i, g)),
        pl.BlockSpec((None, None, None, crow, LANE), lambda b, g, i: (0, b, g, 0, 0)),
        pl.BlockSpec((None, None, None, crow, 2 * LANE), lambda b, g, i: (1, b, g, 0, 0)),
        kv_spec(2 * LANE), kv_spec(LANE), kv_spec(LANE), kv_spec(LANE),
        pl.BlockSpec((NSA_REP, Q_BLOCK, 2 * Q_BLOCK), lambda b, g, i: (g, 0, 0)),
        pl.BlockSpec((NSA_REP, Q_BLOCK, LANE), lambda b, g, i: (g, 0, 0)),
        pl.BlockSpec((Q_BLOCK, LANE), lambda b, g, i: (0, 0)),
    ]
    rows = NSA_REP * Q_BLOCK
    return pl.pallas_call(
        functools.partial(_nsa_kernel, n_sel=n_sel),
        grid=(batch, NSA_KV, nq),
        in_specs=in_specs,
        out_specs=pl.BlockSpec((Q_BLOCK, NSA_REP * HEAD_DIM), lambda b, g, i: (b * nq + i, g)),
        out_shape=jax.ShapeDtypeStruct((batch * seq, NSA_WIDTH), F32),
        scratch_shapes=[pltpu.VMEM((rows, 2 * LANE), BF16),
                        pltpu.VMEM((rows, KV_TILE), F32),
                        pltpu.VMEM((rows, KV_TILE), F32),
                        pltpu.VMEM((rows, LANE), F32),
                        pltpu.VMEM((rows, LANE), F32),
                        pltpu.VMEM((rows, LANE), F32),
                        pltpu.VMEM((rows, LANE), F32),
                        pltpu.VMEM((rows, LANE), F32)],
        compiler_params=_params(("arbitrary", "arbitrary", "arbitrary")),
        name="nsa",
    )(q_aug, gates, cmp_aug, cmp_aug, ksel, vsel, kwin, vwin, tb, tc, wm)


_N_GROUP_SCRATCH = 8


def _oddeven_merge_sort(n):
    def merge(lo, hi, r):
        step = 2 * r
        if step < hi - lo:
            yield from merge(lo, hi, step)
            yield from merge(lo + r, hi, step)
            yield from ((k, k + r) for k in range(lo + r, hi - r, step))
        else:
            yield (lo, lo + r)

    def sort(lo, hi):
        if hi > lo:
            mid = lo + (hi - lo) // 2
            yield from sort(lo, mid)
            yield from sort(mid + 1, hi)
            yield from merge(lo, hi, 1)

    return tuple(sort(0, n - 1))


def _bitonic_merge(n):
    pairs, k = [], n // 2
    while k >= 1:
        pairs += [(a, a + k) for a in range(n) if not a & k]
        k //= 2
    return tuple(pairs)


_FRONT_SKEW = 1
_SORT16 = _oddeven_merge_sort(N_SELECT)
_BITONIC16 = _bitonic_merge(N_SELECT)


def _nsa2_kernel(q_ref, gt_ref, kc_ref, vc_ref, ks_ref, vs_ref, kw_ref, vw_ref,
                 tb_ref, tc_ref, wm_ref, gx_ref, o_ref, *scratch, n_sel, select_all):
    i = pl.program_id(1)
    rows = NSA_REP * Q_BLOCK
    per_group = [scratch[g * _N_GROUP_SCRATCH:(g + 1) * _N_GROUP_SCRATCH] for g in range(NSA_KV)]
    qsel_ref = [s[0] for s in per_group]
    s_ref = [(s[1], s[2]) for s in per_group]
    m_ref = [(s[3], s[4]) for s in per_group]
    al_ref = [(s[5], s[6]) for s in per_group]
    acc_ref = [s[7] for s in per_group]

    zero_tb = jnp.zeros((rows, KV_TILE - 2 * Q_BLOCK), F32)
    c0 = pl.multiple_of(8 * i + 8 + CMP_PAD - CMP_SPAN, 8)
    t0 = pl.multiple_of(Q_BLOCK * (i + 1) + KV_PAD - KV_TILE, Q_BLOCK)
    w1 = pl.multiple_of(t0 - Q_BLOCK, Q_BLOCK)
    n_tiles = (i + 1 + KV_TILE // Q_BLOCK - 1) // (KV_TILE // Q_BLOCK)
    lane = lax.broadcasted_iota(jnp.int32, (Q_BLOCK, LANE), 1)

    def bcast(col):
        return jnp.broadcast_to(col, (rows, LANE))

    def lanes(x, n):
        return jnp.concatenate([x] * n, axis=1)

    def head_out(acc, r):
        rs = slice(r * Q_BLOCK, (r + 1) * Q_BLOCK)
        a, b = acc[rs, 0:LANE], acc[rs, LANE:2 * LANE]
        return a * (1.0 / b) if r % 2 == 0 else b * (1.0 / a)

    def heads_of(g):
        return slice(g * NSA_REP, (g + 1) * NSA_REP)

    def q_of(g):
        return q_ref[heads_of(g)].reshape(rows, LANE)

    def tile0_bias(g):
        return jnp.concatenate([zero_tb, tb_ref[heads_of(g)].reshape(rows, 2 * Q_BLOCK)], axis=1)

    fronts = [None] * NSA_KV

    def front(g):
        hs = heads_of(g)
        qa = q_of(g)
        s = _dot_nt(qa, kc_ref[g, pl.ds(c0, CMP_SPAN), :].astype(BF16))
        s = s + jnp.concatenate([jnp.zeros((rows, CMP_SPAN - LANE), F32),
                                 tc_ref[hs].reshape(rows, LANE)], axis=1)
        yield
        m = jnp.maximum(jnp.max(s, axis=1, keepdims=True), MAX_CLAMP)
        p = jnp.exp2(s - m).astype(BF16)
        yield
        acc = _dot(p, vc_ref[g, pl.ds(c0, CMP_SPAN), :].astype(BF16))
        num_den = acc[:, 0:LANE]
        lane_r = lax.broadcasted_iota(jnp.int32, (rows, LANE), 1)
        den = jnp.where(lane_r < HEAD_DIM, pltpu.roll(num_den, HEAD_DIM, 1), num_den)
        inv = 1.0 / jnp.maximum(den, 1e-30)
        oc = num_den * inv
        impu = acc[:, LANE:2 * LANE] * inv
        imp = impu[0:Q_BLOCK]
        for r in range(1, NSA_REP):
            imp = imp + impu[r * Q_BLOCK:(r + 1) * Q_BLOCK]
        yield
        s0 = _dot_nt(qa, kw_ref[g, pl.ds(t0, KV_TILE), :]) + tile0_bias(g)
        s1 = _dot_nt(qa, kw_ref[g, pl.ds(w1, Q_BLOCK), :]) + jnp.tile(wm_ref[...], (NSA_REP, 1))
        yield
        m = jnp.maximum(jnp.max(s0, axis=1, keepdims=True), jnp.max(s1, axis=1, keepdims=True))
        p0 = jnp.exp2(s0 - m).astype(BF16)
        p1 = jnp.exp2(s1 - m).astype(BF16)
        yield
        accw = (_dot(p0, vw_ref[g, pl.ds(t0, KV_TILE), :]) + _dot(p1, vw_ref[g, pl.ds(w1, Q_BLOCK), :]))
        fronts[g] = (oc, [head_out(accw, r) for r in range(NSA_REP)])
        yield
        notsel_q = not_selected(imp).astype(BF16)
        qsel_ref[g][:, 0:LANE] = qa
        for r in range(NSA_REP):
            qsel_ref[g][r * Q_BLOCK:(r + 1) * Q_BLOCK, LANE:2 * LANE] = notsel_q
        yield
        s = _dot_nt(qsel_ref[g][...], ks_ref[g, pl.ds(t0, KV_TILE), :]) + tile0_bias(g)
        yield
        s_ref[g][0][...] = s
        m_ref[g][0][...] = bcast(jnp.max(s, axis=1, keepdims=True))
        al_ref[g][0][...] = jnp.ones((rows, LANE), F32)
        acc_ref[g][...] = jnp.zeros((rows, 2 * LANE), F32)

    def not_selected(imp):
        jb = lax.broadcasted_iota(jnp.int32, (LANE, Q_BLOCK), 0)
        t = i * Q_BLOCK + lax.broadcasted_iota(jnp.int32, (LANE, Q_BLOCK), 1)
        cur = t // SEL_BLOCK
        forced = (jb == 0) | (jb == cur) | (jb == cur - 1)
        score = jnp.where(forced, FORCE_SCORE, jnp.where(jb * SEL_BLOCK <= t, imp.T, -1.0))
        if select_all:
            return jnp.zeros((Q_BLOCK, LANE), F32)
        sub = 8
        x = [score[sub * v:sub * (v + 1), :] for v in range(LANE // sub)]
        y = list(x)
        for a, b in _SORT16:
            y[a], y[b] = jnp.maximum(y[a], y[b]), jnp.minimum(y[a], y[b])
        for shift in (4, 2, 1):
            z = [jnp.maximum(y[k], pltpu.roll(y[n_sel - 1 - k], shift, 0)) for k in range(n_sel)]
            if shift > 1:
                for a, b in _BITONIC16:
                    z[a], z[b] = jnp.maximum(z[a], z[b]), jnp.minimum(z[a], z[b])
            y = z
        while len(y) > 1:
            y = [jnp.minimum(y[2 * k], y[2 * k + 1]) for k in range(len(y) // 2)]
        tau = y[0]
        gt = jnp.concatenate([jnp.where(xv > tau, 1.0, 0.0) for xv in x], axis=0)
        eq = jnp.concatenate([jnp.where(xv == tau, 1.0, 0.0) for xv in x], axis=0)
        kb = lax.broadcasted_iota(jnp.int32, (LANE, LANE), 1)
        n_above = _dot(jnp.ones((LANE, LANE), BF16), gt.astype(BF16))
        n_tied_before = _dot(jnp.where(kb < jb, 1.0, 0.0).astype(BF16), eq.astype(BF16))
        keep = (gt > 0.0) | ((eq > 0.0) & (n_above + n_tied_before < n_sel))
        return jnp.where(keep, 0.0, 1.0).T


    def qk_stage(g, k, prev, slot):
        st = pl.multiple_of(t0 - KV_TILE * k, Q_BLOCK)
        s_n = _dot_nt(qsel_ref[g][...], ks_ref[g, pl.ds(st, KV_TILE), :])
        s_ref[g][slot][...] = s_n
        m_c = m_ref[g][prev][...]
        m_n = jnp.maximum(m_c, bcast(jnp.max(s_n, axis=1, keepdims=True)))
        m_ref[g][slot][...] = m_n
        al_ref[g][slot][...] = jnp.exp2(m_c - m_n)

    def pv_stage(g, k, slot):
        st = pl.multiple_of(t0 - KV_TILE * k, Q_BLOCK)
        p = jnp.exp2(s_ref[g][slot][...] - lanes(m_ref[g][slot][...], KV_TILE // LANE))
        acc_ref[g][...] = (lanes(al_ref[g][slot][...], 2) * acc_ref[g][...]
                           + _dot(p.astype(BF16), vs_ref[g, pl.ds(st, KV_TILE), :]))

    groups = range(NSA_KV)
    pending = [front(g) for g in groups]
    for _ in range(_FRONT_SKEW):
        next(pending[0])
    while pending:
        for gen in list(pending):
            if next(gen, pending) is pending:
                pending.remove(gen)

    def tile_pair(j):
        for g in groups:
            qk_stage(g, 2 * j + 1, 0, 1)
        for g in groups:
            pv_stage(g, 2 * j, 0)
        for g in groups:
            qk_stage(g, 2 * j + 2, 1, 0)
        for g in groups:
            pv_stage(g, 2 * j + 1, 1)

    def two_pairs(j, carry):
        tile_pair(2 * j)
        tile_pair(2 * j + 1)
        return carry

    def one_pair(j, carry):
        tile_pair(j)
        return carry

    n_pairs = (n_tiles - 1) // 2
    lax.fori_loop(0, n_pairs // 2, two_pairs, 0)
    lax.fori_loop(2 * (n_pairs // 2), n_pairs, one_pair, 0)
    odd_left = n_tiles - 1 - 2 * n_pairs

    @pl.when(odd_left == 1)
    def _two_left():
        for g in groups:
            qk_stage(g, n_tiles - 1, 0, 1)
        for g in groups:
            pv_stage(g, n_tiles - 2, 0)
        for g in groups:
            pv_stage(g, n_tiles - 1, 1)

    @pl.when(odd_left == 0)
    def _one_left():
        for g in groups:
            pv_stage(g, n_tiles - 1, 0)

    for g in groups:
        oc, ow = fronts[g]
        acc = acc_ref[g][...]
        gx = _dot(gt_ref[...], gx_ref[g])

        def gate(branch, r):
            c = branch * NSA_REP + r
            return gx[:, c * LANE:(c + 1) * LANE]

        heads = []
        for r in range(NSA_REP):
            rs = slice(r * Q_BLOCK, (r + 1) * Q_BLOCK)
            oc_r = oc[rs] if r % 2 == 0 else pltpu.roll(oc[rs], HEAD_DIM, 1)
            heads.append(gate(0, r) * oc_r + gate(1, r) * head_out(acc, r) + gate(2, r) * ow[r])
        for pr in range(NSA_REP // 2):
            col = (g * NSA_REP // 2 + pr) * LANE
            o_ref[:, col:col + LANE] = jnp.where(lane < HEAD_DIM, heads[2 * pr], heads[2 * pr + 1])


def _nsa2(q_aug, gates, cmp_aug, ksel, vsel, kwin, vwin, tb, tc, wm, batch, seq):
    nq = seq // Q_BLOCK
    sp = seq + KV_PAD
    crow = cmp_aug.shape[3]
    n_sel = min(N_SELECT, seq // SEL_BLOCK)
    assert seq // SEL_BLOCK <= LANE and KV_PAD >= KV_TILE and KV_PAD >= WINDOW
    assert seq // CMP_STRIDE <= CMP_SPAN <= CMP_PAD and crow == CMP_PAD + seq // CMP_STRIDE
    assert seq // SEL_BLOCK <= N_SELECT or n_sel == LANE // 8
    n_gate = 3 * NSA_REP
    col = jnp.arange(n_gate * LANE) // LANE
    row = jnp.arange(2 * LANE) % LANE
    gate_spread = jnp.stack([(row[:, None] == g * n_gate + col[None, :]) for g in range(NSA_KV)]).astype(BF16)
    once = pl.Buffered(1)
    kv_spec = lambda w: pl.BlockSpec((None, NSA_KV, sp, w), lambda b, i: (b, 0, 0, 0), pipeline_mode=once)
    cmp_spec = lambda k, w: pl.BlockSpec((None, None, NSA_KV, crow, w), lambda b, i: (k, b, 0, 0, 0),
                                         pipeline_mode=once)
    in_specs = [
        pl.BlockSpec((None, NSA_HEADS, Q_BLOCK, LANE), lambda b, i: (b, 0, i, 0)),
        pl.BlockSpec((Q_BLOCK, 2 * LANE), lambda b, i: (b * nq + i, 0)),
        cmp_spec(0, LANE), cmp_spec(1, 2 * LANE),
        kv_spec(2 * LANE), kv_spec(2 * LANE), kv_spec(LANE), kv_spec(2 * LANE),
        _const_spec((NSA_HEADS, Q_BLOCK, 2 * Q_BLOCK)),
        _const_spec((NSA_HEADS, Q_BLOCK, LANE)),
        _const_spec((Q_BLOCK, LANE)),
        _const_spec((NSA_KV, 2 * LANE, n_gate * LANE)),
    ]
    rows = NSA_REP * Q_BLOCK
    group_scratch = [pltpu.VMEM((rows, 2 * LANE), BF16),
                     pltpu.VMEM((rows, KV_TILE), F32),
                     pltpu.VMEM((rows, KV_TILE), F32),
                     pltpu.VMEM((rows, LANE), F32),
                     pltpu.VMEM((rows, LANE), F32),
                     pltpu.VMEM((rows, LANE), F32),
                     pltpu.VMEM((rows, LANE), F32),
                     pltpu.VMEM((rows, 2 * LANE), F32)]
    assert len(group_scratch) == _N_GROUP_SCRATCH
    return pl.pallas_call(
        functools.partial(_nsa2_kernel, n_sel=n_sel, select_all=seq // SEL_BLOCK <= N_SELECT),
        grid=(batch, nq),
        in_specs=in_specs,
        out_specs=pl.BlockSpec((Q_BLOCK, NSA_WIDTH), lambda b, i: (b * nq + i, 0)),
        out_shape=jax.ShapeDtypeStruct((batch * seq, NSA_WIDTH), F32),
        scratch_shapes=group_scratch * NSA_KV,
        compiler_params=_params(("arbitrary", "arbitrary")),
        name="nsa",
    )(q_aug, gates, cmp_aug, cmp_aug, ksel, vsel, kwin, vwin, tb, tc, wm, gate_spread)


def _memkv_kernel(mem_ref, g_ref, wk_ref, wv_ref, k_ref, v_ref):
    h = _rms(mem_ref[...], g_ref[...]).astype(BF16)
    k_ref[...] = _dot(h, wk_ref[...]).astype(BF16)
    v_ref[...] = _dot(h, wv_ref[...]).astype(BF16)


def _mem_kv(mem, g_mem, wkv):
    batch, m, _ = mem.shape
    blk = pl.BlockSpec((None, m, D_MODEL), lambda b: (b, 0, 0))
    return pl.pallas_call(
        _memkv_kernel,
        grid=(batch,),
        in_specs=[blk, _const_spec((1, D_MODEL)), _const_spec((D_MODEL, D_MODEL)),
                  _const_spec((D_MODEL, D_MODEL))],
        out_specs=[blk, blk],
        out_shape=[jax.ShapeDtypeStruct((batch, m, D_MODEL), BF16)] * 2,
        compiler_params=_params(("arbitrary",)),
        name="mem_kv",
    )(mem, g_mem.reshape(1, -1), wkv[:, :D_MODEL].astype(BF16), wkv[:, D_MODEL:].astype(BF16))


def _mixout_kernel(x_ref, ya_ref, yb_ref, gb_ref, wo_ref, post_ref, xpre_ref, wq_ref,
                   km_ref, vm_ref, wxo_ref, xpost_ref, o_ref):
    yb = _rms(yb_ref[...], gb_ref[...]).astype(BF16)
    y = _dot(ya_ref[...].astype(BF16), wo_ref[0:GM_WIDTH, :]) + _dot(yb, wo_ref[GM_WIDTH:, :])
    x1 = x_ref[...] + _rms(y, post_ref[...])
    h = _rms(x1, xpre_ref[...]).astype(BF16)
    q = _dot(h, wq_ref[...]).astype(BF16)
    outs = []
    for hd in range(XA_HEADS):
        hs = slice(hd * XA_HEAD_DIM, (hd + 1) * XA_HEAD_DIM)
        s = _dot_nt(q[:, hs], km_ref[:, hs])
        e = jnp.exp(s - jnp.max(s, axis=1, keepdims=True))
        o = _dot(e.astype(BF16), vm_ref[:, hs]) / jnp.sum(e, axis=1, keepdims=True)
        outs.append(o.astype(BF16))
    o = _dot(jnp.concatenate(outs, axis=1), wxo_ref[...])
    o_ref[...] = x1 + _rms(o, xpost_ref[...])


def _mix_out(x2d, ya, yb, batch, seq, gain_b, w_out, mix_post, xa_pre, wq, kmem, vmem, wo, xa_post):
    tm = TOKEN_TILE
    nt = seq // tm
    row = lambda w: pl.BlockSpec((tm, w), lambda b, t: (b * nt + t, 0))
    mem_spec = pl.BlockSpec((None, kmem.shape[1], D_MODEL), lambda b, t: (b, 0, 0))
    sq = _const_spec((D_MODEL, D_MODEL))
    vec = _const_spec((1, D_MODEL))
    return pl.pallas_call(
        _mixout_kernel,
        grid=(batch, nt),
        in_specs=[row(D_MODEL), row(GM_WIDTH), row(NSA_WIDTH), _const_spec((1, NSA_WIDTH)), sq, vec,
                  vec, sq, mem_spec, mem_spec, sq, vec],
        out_specs=row(D_MODEL),
        out_shape=jax.ShapeDtypeStruct(x2d.shape, F32),
        compiler_params=_params(("arbitrary", "arbitrary")),
        name="mix_out",
    )(x2d, ya, yb, gain_b.reshape(1, -1), w_out.astype(BF16), mix_post.reshape(1, -1),
      xa_pre.reshape(1, -1), (wq * XA_HEAD_DIM ** -0.5).astype(BF16), kmem, vmem,
      wo.astype(BF16), xa_post.reshape(1, -1))


def kernel(x, mem, ffn1_pre, ffn1_post, ffn1_wg, ffn1_wu, ffn1_wd, mix_pre, mix_post, w_in, gm_ln_g, gm_ln_b, gm_ws, gm_bs, ck_pe, ck_w1, ck_b1, ck_w2, cv_pe, cv_w1, cv_b1, cv_w2, rel_bias, out_gain_a, out_gain_b, w_out, xa_pre, xa_post, mem_norm, xa_wq, xa_wkv, xa_wo, ffn2_pre, ffn2_post, ffn2_wg, ffn2_wu, ffn2_wd):
    batch, seq, _ = x.shape
    depth = ffn1_pre.shape[0]
    x2d = x.reshape(batch * seq, D_MODEL)
    tb, tc, wm = _bias_tables(rel_bias)
    for l in range(depth):
        x2d = _ffn(x2d, ffn1_pre[l], ffn1_wg[l], ffn1_wu[l], ffn1_wd[l], ffn1_post[l])
        ya, q_aug, cmp_raw, ksel, vsel, kwin, vwin, gates = _mixproj(
            x2d, batch, seq, mix_pre[l], w_in[l], gm_ln_g[l], gm_ln_b[l], gm_ws[l], gm_bs[l],
            out_gain_a[l])
        cmp_aug = _compress(cmp_raw, batch, seq,
                            jnp.stack([ck_pe[l], cv_pe[l]]), jnp.stack([ck_w1[l], cv_w1[l]]),
                            jnp.stack([ck_b1[l], cv_b1[l]]), jnp.stack([ck_w2[l], cv_w2[l]]))
        yb = _nsa2(q_aug, gates, cmp_aug, ksel, vsel, kwin, vwin, tb, tc, wm, batch, seq)
        kmem, vmem = _mem_kv(mem, mem_norm[l], xa_wkv[l])
        x2d = _mix_out(x2d, ya, yb, batch, seq, out_gain_b[l], w_out[l], mix_post[l], xa_pre[l],
                       xa_wq[l], kmem, vmem, xa_wo[l], xa_post[l])
        x2d = _ffn(x2d, ffn2_pre[l], ffn2_wg[l], ffn2_wu[l], ffn2_wd[l], ffn2_post[l])
    return x2d.reshape(batch, seq, D_MODEL)
```

```python
import functools
import math

import jax
import jax.numpy as jnp
from jax import lax
from jax.experimental import pallas as pl
from jax.experimental.pallas import tpu as pltpu

F32 = jnp.float32
BF16 = jnp.bfloat16

D_MODEL = 1024
MEM_LEN = 256
EPS = 1e-6
D_FF = 2816
GM_WIDTH = 512
GM_GROUPS = 4
GM_CH = GM_WIDTH // GM_GROUPS
CHUNK = 128
NSA_WIDTH = D_MODEL - GM_WIDTH
HEAD_DIM = 64
NSA_HEADS = NSA_WIDTH // HEAD_DIM
NSA_KV = 2
NSA_REP = NSA_HEADS // NSA_KV
KV_W = NSA_KV * HEAD_DIM
CMP_BLOCK = 32
CMP_STRIDE = 16
CMP_HIDDEN = 256
SEL_BLOCK = 64
N_SELECT = 16
WINDOW = 512
Q_BLOCK = 128
N_BUCKETS = 32
MAX_DISTANCE = 128
XA_HEADS = 4
XA_HEAD_DIM = D_MODEL // XA_HEADS
FORCE_SCORE = 1e4

LANE = 128
TOKEN_TILE = 512
FF_CHUNK = 256
KV_PAD = 512
KV_TILE = 512
CMP_PAD = 512
CMP_SPAN = 512
MASK_BF16 = -(2.0 ** 100)
MASK_F32 = -1e30
MAX_CLAMP = -(2.0 ** 99)
LOG2E = math.log2(math.e)
_softmax_exp = jnp.exp2
VMEM_LIMIT = 56 * 1024 * 1024

_NT = (((1,), (1,)), ((), ()))


def _rms(x, g):
    ms = jnp.mean(x * x, axis=-1, keepdims=True)
    return x * lax.rsqrt(ms + EPS) * g


def _dot(a, b):
    return jnp.dot(a, b, preferred_element_type=F32)


def _dot_nt(a, b):
    return lax.dot_general(a, b, _NT, preferred_element_type=F32)


def _params(sem, vmem=VMEM_LIMIT):
    return pltpu.CompilerParams(dimension_semantics=sem, vmem_limit_bytes=vmem)


def _const_spec(shape):
    nd = len(shape)
    return pl.BlockSpec(shape, lambda *_: (0,) * nd)


def _interleave(chains, skew=1):
    pending = list(chains)
    for _ in range(skew):
        next(pending[0])
    while pending:
        for gen in list(pending):
            if next(gen, pending) is pending:
                pending.remove(gen)


def _ffn_kernel(x_ref, pre_ref, wg_ref, wu_ref, wd_ref, post_ref, o_ref, acc_ref):
    tm = x_ref.shape[0]

    def chain(rows):
        h = _rms(x_ref[rows, :], pre_ref[...]).astype(BF16)
        yield
        for c in range(D_FF // FF_CHUNK):
            sl = slice(c * FF_CHUNK, (c + 1) * FF_CHUNK)
            g = _dot(h, wg_ref[:, sl])
            u = _dot(h, wu_ref[:, sl])
            a = (jax.nn.silu(g) * u).astype(BF16)
            d = _dot(a, wd_ref[sl, :])
            if c == 0:
                acc_ref[rows, :] = d
            else:
                acc_ref[rows, :] += d
            yield
        o_ref[rows, :] = x_ref[rows, :] + 0.5 * _rms(acc_ref[rows, :], post_ref[...])

    _interleave([chain(slice(k * tm // 2, (k + 1) * tm // 2)) for k in range(2)])


def _ffn(x2d, pre, wg, wu, wd, post):
    n = x2d.shape[0]
    tm = 2 * TOKEN_TILE
    row = pl.BlockSpec((tm, D_MODEL), lambda t: (t, 0))
    once = pl.Buffered(1)
    weight = lambda shape: pl.BlockSpec(shape, lambda t: (0, 0), pipeline_mode=once)
    return pl.pallas_call(
        _ffn_kernel,
        grid=(n // tm,),
        in_specs=[row, _const_spec((1, D_MODEL)), weight((D_MODEL, D_FF)),
                  weight((D_MODEL, D_FF)), weight((D_FF, D_MODEL)),
                  _const_spec((1, D_MODEL))],
        out_specs=row,
        out_shape=jax.ShapeDtypeStruct((n, D_MODEL), F32),
        scratch_shapes=[pltpu.VMEM((tm, D_MODEL), F32)],
        compiler_params=_params(("arbitrary",)),
        name="ffn",
    )(x2d, pre.reshape(1, -1), wg.astype(BF16), wu.astype(BF16), wd.astype(BF16),
      post.reshape(1, -1))


def _mixproj_kernel(x_ref, pre_ref, w_ref, wgt_ref, lng_ref, lnb_ref, ws_ref, bs_ref, ga_ref,
                    ya_ref, q_ref, cmp_ref, ksel_ref, vsel_ref, kwin_ref, vwin_ref, gt_ref, cmps_ref):
    t = pl.program_id(1)
    tm = x_ref.shape[0]
    lane = lax.broadcasted_iota(jnp.int32, (tm, LANE), 1)
    flag_col = jnp.where(lane == HEAD_DIM, MASK_BF16, 0.0).astype(BF16)

    @pl.when(t == 0)
    def _pad_rows():
        for g in range(NSA_KV):
            ksel_ref[g, :, 0:LANE] = flag_col
            ksel_ref[g, :, LANE:2 * LANE] = jnp.zeros((tm, LANE), BF16)
            kwin_ref[g] = flag_col
            vsel_ref[g] = jnp.zeros((tm, 2 * LANE), BF16)
            vwin_ref[g] = jnp.zeros((tm, 2 * LANE), BF16)

    def chain(r0, n):
        rows = slice(r0, r0 + n)
        h = _rms(x_ref[rows, :], pre_ref[...]).astype(BF16)
        yield
        u = jax.nn.gelu(_dot(h, w_ref[:, 0:GM_WIDTH]))
        v = jax.nn.gelu(_dot(h, w_ref[:, GM_WIDTH:2 * GM_WIDTH]))
        yield
        r_i = lax.broadcasted_iota(jnp.int32, (CHUNK, CHUNK), 0)
        c_i = lax.broadcasted_iota(jnp.int32, (CHUNK, CHUNK), 1)
        for g in range(GM_GROUPS):
            gs = slice(g * GM_CH, (g + 1) * GM_CH)
            vg = v[:, gs]
            mu = jnp.mean(vg, axis=-1, keepdims=True)
            var = jnp.mean(jnp.square(vg - mu), axis=-1, keepdims=True)
            vn = ((vg - mu) * lax.rsqrt(var + EPS) * lng_ref[:, gs] + lnb_ref[:, gs]).astype(BF16)
            w = jnp.where(c_i <= r_i, ws_ref[g], 0.0).astype(BF16)
            for c in range(n // CHUNK):
                rs = slice(c * CHUNK, (c + 1) * CHUNK)
                s = _dot(w, vn[rs]) + bs_ref[:, g:g + 1]
                ya_ref[r0 + c * CHUNK:r0 + (c + 1) * CHUNK, gs] = u[rs, gs] * s
        yield
        ya_ref[rows, :] = _rms(ya_ref[rows, :], ga_ref[...])
        lane = lax.broadcasted_iota(jnp.int32, (n, LANE), 1)
        lane_lo = lane < HEAD_DIM

        def halves(z):
            return z, pltpu.roll(z, HEAD_DIM, 1)

        one_col = jnp.where(lane == HEAD_DIM, 1.0, 0.0)
        zq = _dot(h, w_ref[:, _O_Q:_O_C])
        yield
        for a in range(NSA_HEADS // 2):
            for hd, qh in zip((2 * a, 2 * a + 1), halves(zq[:, a * LANE:(a + 1) * LANE])):
                q_ref[hd, rows, :] = jnp.where(lane_lo, qh, one_col).astype(BF16)
        zc = _dot(h, w_ref[:, _O_C:_O_KV])
        yield
        nr = n // CMP_STRIDE
        urows = slice(r0 // CMP_STRIDE, r0 // CMP_STRIDE + nr)
        lo_u = lax.broadcasted_iota(jnp.int32, (nr, LANE), 1) < HEAD_DIM
        for kv in range(2):
            cmps_ref[kv, rows, :] = zc[:, kv * KV_W:(kv + 1) * KV_W]
            for a in range(CMP_STRIDE // 2):
                za = cmps_ref[kv, pl.ds(r0 + 2 * a, nr, stride=CMP_STRIDE), :]
                zb = cmps_ref[kv, pl.ds(r0 + 2 * a + 1, nr, stride=CMP_STRIDE), :]
                cmp_ref[kv, urows, a * LANE:(a + 1) * LANE] = jnp.where(
                    lo_u, za, pltpu.roll(zb, HEAD_DIM, 1))
                cmp_ref[kv, urows, (CMP_STRIDE // 2 + a) * LANE:(CMP_STRIDE // 2 + a + 1) * LANE] = (
                    jnp.where(lo_u, pltpu.roll(za, HEAD_DIM, 1), zb))
        zkv = _dot(h, w_ref[:, _O_KV:_O_GATE])
        yield
        k_sel, v_sel, k_win, v_win = (halves(zkv[:, j * LANE:(j + 1) * LANE]) for j in range(4))
        pos_blk = ((t - 1) * tm + r0 + lax.broadcasted_iota(jnp.int32, (n, LANE), 0)) // SEL_BLOCK
        blk_onehot = jnp.where(lane == pos_blk, MASK_BF16, 0.0).astype(BF16)

        def v_aug(v, g):
            return jnp.concatenate([jnp.where(lane_lo, v[g], 1.0), jnp.where(lane_lo, 1.0, v[1 - g])],
                                   axis=1).astype(BF16)

        for g in range(NSA_KV):
            ksel_ref[g, rows, 0:LANE] = jnp.where(lane_lo, k_sel[g], 0.0).astype(BF16)
            ksel_ref[g, rows, LANE:2 * LANE] = blk_onehot
            vsel_ref[g, rows, :] = v_aug(v_sel, g)
            kwin_ref[g, rows, :] = jnp.where(lane_lo, k_win[g], 0.0).astype(BF16)
            vwin_ref[g, rows, :] = v_aug(v_win, g)
        sg = jax.nn.sigmoid(_dot(h, wgt_ref[...]))
        hi = sg.astype(BF16)
        gt_ref[rows, 0:LANE] = hi
        gt_ref[rows, LANE:2 * LANE] = (sg - hi.astype(F32)).astype(BF16)

    @pl.when(t > 0)
    def _tile():
        _interleave([chain(k * tm // 2, tm // 2) for k in range(2)])


_O_Q = 2 * GM_WIDTH
_O_C = _O_Q + NSA_WIDTH
_O_KV = _O_C + 2 * KV_W
_O_GATE = _O_KV + 4 * KV_W


def _mixproj(x2d, batch, seq, mix_pre, w_in, ln_g, ln_b, ws, bs, gain_a):
    tm = TOKEN_TILE
    nt = seq // tm
    assert seq % tm == 0 and KV_PAD == tm
    scale = HEAD_DIM ** -0.5 * LOG2E
    w_main = jnp.concatenate([w_in[:, :_O_Q], w_in[:, _O_Q:_O_C] * scale, w_in[:, _O_C:_O_GATE]],
                             axis=1).astype(BF16)
    n_gate = 3 * NSA_HEADS
    w_g = w_in[:, _O_GATE:].reshape(D_MODEL, NSA_KV, NSA_REP, 3).transpose(0, 1, 3, 2)
    w_g = jnp.pad(w_g.reshape(D_MODEL, n_gate), ((0, 0), (0, LANE - n_gate))).astype(BF16)

    def tile(b, t):
        return b * nt + jnp.maximum(t - 1, 0)

    n = batch * seq
    sp = seq + KV_PAD
    out_shape = [
        jax.ShapeDtypeStruct((n, GM_WIDTH), F32),
        jax.ShapeDtypeStruct((batch, NSA_HEADS, seq, LANE), BF16),
        jax.ShapeDtypeStruct((2, n // CMP_STRIDE, CMP_STRIDE * KV_W), F32),
        jax.ShapeDtypeStruct((batch, NSA_KV, sp, 2 * LANE), BF16),
        jax.ShapeDtypeStruct((batch, NSA_KV, sp, 2 * LANE), BF16),
        jax.ShapeDtypeStruct((batch, NSA_KV, sp, LANE), BF16),
        jax.ShapeDtypeStruct((batch, NSA_KV, sp, 2 * LANE), BF16),
        jax.ShapeDtypeStruct((n, 2 * LANE), BF16),
    ]
    padded = lambda w: pl.BlockSpec((None, NSA_KV, tm, w), lambda b, t: (b, 0, t, 0))
    out_specs = [
        pl.BlockSpec((tm, GM_WIDTH), lambda b, t: (tile(b, t), 0)),
        pl.BlockSpec((None, NSA_HEADS, tm, LANE), lambda b, t: (b, 0, jnp.maximum(t - 1, 0), 0)),
        pl.BlockSpec((2, tm // CMP_STRIDE, CMP_STRIDE * KV_W), lambda b, t: (0, tile(b, t), 0)),
        padded(2 * LANE), padded(2 * LANE), padded(LANE), padded(2 * LANE),
        pl.BlockSpec((tm, 2 * LANE), lambda b, t: (tile(b, t), 0)),
    ]
    in_specs = [
        pl.BlockSpec((tm, D_MODEL), lambda b, t: (tile(b, t), 0)),
        _const_spec((1, D_MODEL)),
        _const_spec(w_main.shape), _const_spec(w_g.shape),
        _const_spec((1, GM_WIDTH)), _const_spec((1, GM_WIDTH)),
        _const_spec((GM_GROUPS, CHUNK, CHUNK)), _const_spec((CHUNK, GM_GROUPS)),
        _const_spec((1, GM_WIDTH)),
    ]
    return pl.pallas_call(
        _mixproj_kernel,
        grid=(batch, nt + 1),
        in_specs=in_specs,
        out_specs=out_specs,
        out_shape=out_shape,
        scratch_shapes=[pltpu.VMEM((2, tm, KV_W), F32)],
        compiler_params=_params(("arbitrary", "arbitrary")),
        name="mixproj",
    )(x2d, mix_pre.reshape(1, -1), w_main, w_g, ln_g.reshape(1, -1), ln_b.reshape(1, -1),
      ws, bs.T, gain_a.reshape(1, -1))


def _compress_kernel(raw_ref, pe_ref, w1_ref, b1_ref, w2_ref, o_ref):
    is_v = pl.program_id(0) == 1
    nu = raw_ref.shape[0]
    half = CMP_STRIDE * HEAD_DIM
    outs = []
    for g in range(NSA_KV):
        u = raw_ref[:, g * half:(g + 1) * half]
        a = _dot((u + pe_ref[0]).astype(BF16), w1_ref[0:half, :])
        b = _dot((u + pe_ref[1]).astype(BF16), w1_ref[half:2 * half, :])
        hid = jax.nn.gelu(a + pltpu.roll(b, nu - 1, 0) + b1_ref[...])
        outs.append(_dot(hid.astype(BF16), w2_ref[...]))
    lane = lax.broadcasted_iota(jnp.int32, (nu, LANE), 1)
    ci = lax.broadcasted_iota(jnp.int32, (nu, LANE), 0)
    u_c = CMP_BLOCK // CMP_STRIDE
    u_s = SEL_BLOCK // CMP_STRIDE
    overlap = jnp.maximum(jnp.minimum(ci + u_c, u_s * (lane + 1)) - jnp.maximum(ci, u_s * lane), 0)
    extra = jnp.where(is_v, overlap.astype(F32), 0.0)
    ones_hi = jnp.where(jnp.logical_and(is_v, lane >= HEAD_DIM), 1.0, 0.0)
    lane_p = lax.broadcasted_iota(jnp.int32, (CMP_PAD, LANE), 1)
    pad_lo = jnp.where(jnp.logical_and(jnp.logical_not(is_v), lane_p == HEAD_DIM), MASK_BF16, 0.0)
    for g in range(NSA_KV):
        o_ref[g, 0:CMP_PAD, 0:LANE] = pad_lo
        o_ref[g, 0:CMP_PAD, LANE:2 * LANE] = jnp.zeros((CMP_PAD, LANE), F32)
        o_ref[g, CMP_PAD:CMP_PAD + nu, 0:LANE] = outs[g] + ones_hi
        o_ref[g, CMP_PAD:CMP_PAD + nu, LANE:2 * LANE] = extra


def _compress(cmp_raw, batch, seq, pe, w1, b1, w2):
    nu = seq // CMP_STRIDE
    half = CMP_STRIDE * HEAD_DIM
    kd = NSA_KV * half
    raw = cmp_raw.reshape(2, batch, nu, kd)
    w2p = jnp.pad(w2, ((0, 0), (0, 0), (0, LANE - HEAD_DIM)))
    rows = CMP_PAD + nu
    return pl.pallas_call(
        _compress_kernel,
        grid=(2, batch),
        in_specs=[
            pl.BlockSpec((None, None, nu, kd), lambda k, b: (k, b, 0, 0)),
            pl.BlockSpec((None, 2, 1, half), lambda k, b: (k, 0, 0, 0)),
            pl.BlockSpec((None, 2 * half, CMP_HIDDEN), lambda k, b: (k, 0, 0)),
            pl.BlockSpec((None, 1, CMP_HIDDEN), lambda k, b: (k, 0, 0)),
            pl.BlockSpec((None, CMP_HIDDEN, LANE), lambda k, b: (k, 0, 0)),
        ],
        out_specs=pl.BlockSpec((None, None, NSA_KV, rows, 2 * LANE), lambda k, b: (k, b, 0, 0, 0)),
        out_shape=jax.ShapeDtypeStruct((2, batch, NSA_KV, rows, 2 * LANE), F32),
        compiler_params=_params(("arbitrary", "arbitrary")),
        name="compress",
    )(raw, pe.reshape(2, 2, 1, half), w1.astype(BF16), b1.reshape(2, 1, CMP_HIDDEN), w2p.astype(BF16))


def _t5_bucket(dist):
    n = jnp.maximum(dist, 0)
    max_exact = N_BUCKETS // 2
    nf = jnp.maximum(n, max_exact).astype(F32)
    large = max_exact + jnp.floor(jnp.log(nf * (1.0 / max_exact)) / math.log(MAX_DISTANCE / max_exact)
                                  * (N_BUCKETS - max_exact)).astype(jnp.int32)
    large = jnp.minimum(large, N_BUCKETS - 1)
    return jnp.where(n < max_exact, n, large)


def _tables_kernel(rb_ref, tb_ref, tc_ref, wm_ref):
    def fill(out_ref, dist):
        bk = _t5_bucket(dist)
        for h in range(NSA_HEADS):
            far = rb_ref[N_BUCKETS - 1, h]
            val = jnp.zeros(dist.shape, F32)
            for b in range(N_BUCKETS - 1):
                val = jnp.where(bk == b, (rb_ref[b, h] - far) * LOG2E, val)
            out_ref[h] = jnp.where(dist >= 0, val, MASK_F32)

    tq = lax.broadcasted_iota(jnp.int32, (Q_BLOCK, 2 * Q_BLOCK), 0)
    j = lax.broadcasted_iota(jnp.int32, (Q_BLOCK, 2 * Q_BLOCK), 1)
    fill(tb_ref, tq + Q_BLOCK - j)
    tq = lax.broadcasted_iota(jnp.int32, (Q_BLOCK, LANE), 0)
    c = lax.broadcasted_iota(jnp.int32, (Q_BLOCK, LANE), 1)
    cmp_first = Q_BLOCK // CMP_STRIDE - LANE
    fill(tc_ref, tq - CMP_STRIDE * (cmp_first + c) - (CMP_BLOCK - 1))
    wm_ref[...] = jnp.where(c > tq, 0.0, MASK_F32)


def _bias_tables(rel_bias):
    return pl.pallas_call(
        _tables_kernel,
        in_specs=[pl.BlockSpec(memory_space=pltpu.SMEM)],
        out_shape=[jax.ShapeDtypeStruct((NSA_HEADS, Q_BLOCK, 2 * Q_BLOCK), F32),
                   jax.ShapeDtypeStruct((NSA_HEADS, Q_BLOCK, LANE), F32),
                   jax.ShapeDtypeStruct((Q_BLOCK, LANE), F32)],
        name="bias_tables",
    )(rel_bias)


def _nsa_kernel(q_ref, gt_ref, kc_ref, vc_ref, ks_ref, vs_ref, kw_ref, vw_ref,
                tb_ref, tc_ref, wm_ref, o_ref, qsel_ref, s0_ref, s1_ref, m0_ref, m1_ref,
                al0_ref, al1_ref, acc_ref, *, n_sel):
    i = pl.program_id(2)
    s_ref, m_ref, al_ref = (s0_ref, s1_ref), (m0_ref, m1_ref), (al0_ref, al1_ref)
    rows = NSA_REP * Q_BLOCK
    qa = q_ref[...].reshape(rows, LANE)
    tb = tb_ref[...].reshape(rows, 2 * Q_BLOCK)
    zero_tb = jnp.zeros((rows, KV_TILE - 2 * Q_BLOCK), F32)

    c0 = pl.multiple_of(8 * i + 8 + CMP_PAD - CMP_SPAN, 8)
    s = _dot_nt(qa, kc_ref[pl.ds(c0, CMP_SPAN), :].astype(BF16))
    s = s + jnp.concatenate([jnp.zeros((rows, CMP_SPAN - LANE), F32),
                             tc_ref[...].reshape(rows, LANE)], axis=1)
    m = jnp.maximum(jnp.max(s, axis=1, keepdims=True), MAX_CLAMP)
    acc = _dot(jnp.exp(s - m).astype(BF16), vc_ref[pl.ds(c0, CMP_SPAN), :].astype(BF16))
    inv = 1.0 / jnp.maximum(acc[:, HEAD_DIM:HEAD_DIM + 1], 1e-30)
    oc = acc[:, 0:LANE] * inv
    impu = acc[:, LANE:2 * LANE] * inv
    imp = impu[0:Q_BLOCK]
    for r in range(1, NSA_REP):
        imp = imp + impu[r * Q_BLOCK:(r + 1) * Q_BLOCK]

    t0 = pl.multiple_of(Q_BLOCK * (i + 1) + KV_PAD - KV_TILE, Q_BLOCK)
    w1 = pl.multiple_of(t0 - Q_BLOCK, Q_BLOCK)
    s0 = _dot_nt(qa, kw_ref[pl.ds(t0, KV_TILE), :]) + jnp.concatenate([zero_tb, tb], axis=1)
    s1 = _dot_nt(qa, kw_ref[pl.ds(w1, Q_BLOCK), :]) + jnp.tile(wm_ref[...], (NSA_REP, 1))
    m = jnp.maximum(jnp.max(s0, axis=1, keepdims=True), jnp.max(s1, axis=1, keepdims=True))
    acc = (_dot(jnp.exp(s0 - m).astype(BF16), vw_ref[pl.ds(t0, KV_TILE), :])
           + _dot(jnp.exp(s1 - m).astype(BF16), vw_ref[pl.ds(w1, Q_BLOCK), :]))
    ow = acc * (1.0 / acc[:, HEAD_DIM:HEAD_DIM + 1])

    imp_t = imp.T
    jb = lax.broadcasted_iota(jnp.int32, (LANE, Q_BLOCK), 0)
    t = i * Q_BLOCK + lax.broadcasted_iota(jnp.int32, (LANE, Q_BLOCK), 1)
    cur = t // SEL_BLOCK
    forced = (jb == 0) | (jb == cur) | (jb == cur - 1)
    score = jnp.where(forced, FORCE_SCORE, jnp.where(jb * SEL_BLOCK <= t, imp_t, -1.0))
    notsel = jnp.ones((LANE, Q_BLOCK), F32)
    for _ in range(n_sel):
        mx = jnp.max(score, axis=0, keepdims=True)
        first = jnp.min(jnp.where(score == mx, jb, LANE), axis=0, keepdims=True)
        hit = jb == first
        notsel = jnp.where(hit, 0.0, notsel)
        score = jnp.where(hit, -2.0, score)
    notsel_q = notsel.T.astype(BF16)
    qsel_ref[:, 0:LANE] = qa
    for r in range(NSA_REP):
        qsel_ref[r * Q_BLOCK:(r + 1) * Q_BLOCK, LANE:2 * LANE] = notsel_q
    qs = qsel_ref[...]

    n_tiles = (i + 1 + KV_TILE // Q_BLOCK - 1) // (KV_TILE // Q_BLOCK)

    def bcast(col):
        return jnp.broadcast_to(col, (rows, LANE))

    def lanes(x):
        return jnp.concatenate([x] * (KV_TILE // LANE), axis=1)

    def qk_stage(k, prev, slot):
        st = pl.multiple_of(t0 - KV_TILE * k, Q_BLOCK)
        s_n = _dot_nt(qs, ks_ref[pl.ds(st, KV_TILE), :])
        s_ref[slot][...] = s_n
        m_c = m_ref[prev][...]
        m_n = jnp.maximum(m_c, bcast(jnp.max(s_n, axis=1, keepdims=True)))
        m_ref[slot][...] = m_n
        al_ref[slot][...] = jnp.exp(m_c - m_n)

    def pv_stage(k, slot):
        st = pl.multiple_of(t0 - KV_TILE * k, Q_BLOCK)
        p = jnp.exp(s_ref[slot][...] - lanes(m_ref[slot][...]))
        acc_ref[...] = (al_ref[slot][...] * acc_ref[...]
                        + _dot(p.astype(BF16), vs_ref[pl.ds(st, KV_TILE), :]))

    s = _dot_nt(qs, ks_ref[pl.ds(t0, KV_TILE), :]) + jnp.concatenate([zero_tb, tb], axis=1)
    s_ref[0][...] = s
    m_ref[0][...] = bcast(jnp.max(s, axis=1, keepdims=True))
    al_ref[0][...] = jnp.ones((rows, LANE), F32)
    acc_ref[...] = jnp.zeros((rows, LANE), F32)

    def sel_body(j, carry):
        qk_stage(2 * j + 1, 0, 1)
        pv_stage(2 * j, 0)
        qk_stage(2 * j + 2, 1, 0)
        pv_stage(2 * j + 1, 1)
        return carry

    n_pairs = (n_tiles - 1) // 2
    lax.fori_loop(0, n_pairs, sel_body, 0)
    odd_left = n_tiles - 1 - 2 * n_pairs

    @pl.when(odd_left == 1)
    def _two_left():
        qk_stage(n_tiles - 1, 0, 1)
        pv_stage(n_tiles - 2, 0)
        pv_stage(n_tiles - 1, 1)

    @pl.when(odd_left == 0)
    def _one_left():
        pv_stage(n_tiles - 1, 0)

    acc = acc_ref[...]
    osl = acc * (1.0 / acc[:, HEAD_DIM:HEAD_DIM + 1])

    gts = gt_ref[...]
    lane = lax.broadcasted_iota(jnp.int32, (Q_BLOCK, LANE), 1)
    heads = []
    for r in range(NSA_REP):
        rs = slice(r * Q_BLOCK, (r + 1) * Q_BLOCK)
        heads.append(gts[:, r:r + 1] * oc[rs]
                     + gts[:, NSA_REP + r:NSA_REP + r + 1] * osl[rs]
                     + gts[:, 2 * NSA_REP + r:2 * NSA_REP + r + 1] * ow[rs])
    for pr in range(NSA_REP // 2):
        o_ref[:, pr * LANE:(pr + 1) * LANE] = jnp.where(
            lane < HEAD_DIM, heads[2 * pr], pltpu.roll(heads[2 * pr + 1], HEAD_DIM, 1))


def _nsa(q_aug, gates, cmp_aug, ksel, vsel, kwin, vwin, tb, tc, wm, batch, seq):
    nq = seq // Q_BLOCK
    sp = seq + KV_PAD
    crow = cmp_aug.shape[3]
    n_sel = min(N_SELECT, seq // SEL_BLOCK)
    assert seq // SEL_BLOCK <= LANE and KV_PAD >= KV_TILE and KV_PAD >= WINDOW
    assert seq // CMP_STRIDE <= CMP_SPAN <= CMP_PAD and crow == CMP_PAD + seq // CMP_STRIDE
    kv_spec = lambda w: pl.BlockSpec((None, None, sp, w), lambda b, g, i: (b, g, 0, 0))
    in_specs = [
        pl.BlockSpec((None, NSA_REP, Q_BLOCK, LANE), lambda b, g, i: (b, g, i, 0)),
        pl.BlockSpec((Q_BLOCK, LANE), lambda b, g, i: (b * nq + i, g)),
        pl.BlockSpec((None, None, None, crow, LANE), lambda b, g, i: (0, b, g, 0, 0)),
        pl.BlockSpec((None, None, None, crow, 2 * LANE), lambda b, g, i: (1, b, g, 0, 0)),
        kv_spec(2 * LANE), kv_spec(LANE), kv_spec(LANE), kv_spec(LANE),
        pl.BlockSpec((NSA_REP, Q_BLOCK, 2 * Q_BLOCK), lambda b, g, i: (g, 0, 0)),
        pl.BlockSpec((NSA_REP, Q_BLOCK, LANE), lambda b, g, i: (g, 0, 0)),
        pl.BlockSpec((Q_BLOCK, LANE), lambda b, g, i: (0, 0)),
    ]
    rows = NSA_REP * Q_BLOCK
    return pl.pallas_call(
        functools.partial(_nsa_kernel, n_sel=n_sel),
        grid=(batch, NSA_KV, nq),
        in_specs=in_specs,
        out_specs=pl.BlockSpec((Q_BLOCK, NSA_REP * HEAD_DIM), lambda b, g, i: (b * nq + i, g)),
        out_shape=jax.ShapeDtypeStruct((batch * seq, NSA_WIDTH), F32),
        scratch_shapes=[pltpu.VMEM((rows, 2 * LANE), BF16),
                        pltpu.VMEM((rows, KV_TILE), F32),
                        pltpu.VMEM((rows, KV_TILE), F32),
                        pltpu.VMEM((rows, LANE), F32),
                        pltpu.VMEM((rows, LANE), F32),
                        pltpu.VMEM((rows, LANE), F32),
                        pltpu.VMEM((rows, LANE), F32),
                        pltpu.VMEM((rows, LANE), F32)],
        compiler_params=_params(("arbitrary", "arbitrary", "arbitrary")),
        name="nsa",
    )(q_aug, gates, cmp_aug, cmp_aug, ksel, vsel, kwin, vwin, tb, tc, wm)


_N_GROUP_SCRATCH = 8


def _oddeven_merge_sort(n):
    def merge(lo, hi, r):
        step = 2 * r
        if step < hi - lo:
            yield from merge(lo, hi, step)
            yield from merge(lo + r, hi, step)
            yield from ((k, k + r) for k in range(lo + r, hi - r, step))
        else:
            yield (lo, lo + r)

    def sort(lo, hi):
        if hi > lo:
            mid = lo + (hi - lo) // 2
            yield from sort(lo, mid)
            yield from sort(mid + 1, hi)
            yield from merge(lo, hi, 1)

    return tuple(sort(0, n - 1))


def _bitonic_merge(n):
    pairs, k = [], n // 2
    while k >= 1:
        pairs += [(a, a + k) for a in range(n) if not a & k]
        k //= 2
    return tuple(pairs)


_FRONT_SKEW = 1
_SORT16 = _oddeven_merge_sort(N_SELECT)
_BITONIC16 = _bitonic_merge(N_SELECT)


def _nsa2_kernel(q_ref, gt_ref, kc_ref, vc_ref, ks_ref, vs_ref, kw_ref, vw_ref,
                 tb_ref, tc_ref, wm_ref, gx_ref, o_ref, *scratch, n_sel, select_all):
    i = pl.program_id(1)
    rows = NSA_REP * Q_BLOCK
    per_group = [scratch[g * _N_GROUP_SCRATCH:(g + 1) * _N_GROUP_SCRATCH] for g in range(NSA_KV)]
    qsel_ref = [s[0] for s in per_group]
    s_ref = [(s[1], s[2]) for s in per_group]
    m_ref = [(s[3], s[4]) for s in per_group]
    al_ref = [(s[5], s[6]) for s in per_group]
    acc_ref = [s[7] for s in per_group]

    zero_tb = jnp.zeros((rows, KV_TILE - 2 * Q_BLOCK), F32)
    c0 = pl.multiple_of(8 * i + 8 + CMP_PAD - CMP_SPAN, 8)
    t0 = pl.multiple_of(Q_BLOCK * (i + 1) + KV_PAD - KV_TILE, Q_BLOCK)
    w1 = pl.multiple_of(t0 - Q_BLOCK, Q_BLOCK)
    n_tiles = (i + 1 + KV_TILE // Q_BLOCK - 1) // (KV_TILE // Q_BLOCK)
    lane = lax.broadcasted_iota(jnp.int32, (Q_BLOCK, LANE), 1)

    def bcast(col):
        return jnp.broadcast_to(col, (rows, LANE))

    def lanes(x, n):
        return jnp.concatenate([x] * n, axis=1)

    def head_out(acc, r):
        rs = slice(r * Q_BLOCK, (r + 1) * Q_BLOCK)
        a, b = acc[rs, 0:LANE], acc[rs, LANE:2 * LANE]
        return a * (1.0 / b) if r % 2 == 0 else b * (1.0 / a)

    def heads_of(g):
        return slice(g * NSA_REP, (g + 1) * NSA_REP)

    def q_of(g):
        return q_ref[heads_of(g)].reshape(rows, LANE)

    def tile0_bias(g):
        return jnp.concatenate([zero_tb, tb_ref[heads_of(g)].reshape(rows, 2 * Q_BLOCK)], axis=1)

    fronts = [None] * NSA_KV

    def front(g):
        hs = heads_of(g)
        qa = q_of(g)
        s = _dot_nt(qa, kc_ref[g, pl.ds(c0, CMP_SPAN), :].astype(BF16))
        s = s + jnp.concatenate([jnp.zeros((rows, CMP_SPAN - LANE), F32),
                                 tc_ref[hs].reshape(rows, LANE)], axis=1)
        yield
        m = jnp.maximum(jnp.max(s, axis=1, keepdims=True), MAX_CLAMP)
        p = _softmax_exp(s - m).astype(BF16)
        yield
        acc = _dot(p, vc_ref[g, pl.ds(c0, CMP_SPAN), :].astype(BF16))
        num_den = acc[:, 0:LANE]
        lane_r = lax.broadcasted_iota(jnp.int32, (rows, LANE), 1)
        den = jnp.where(lane_r < HEAD_DIM, pltpu.roll(num_den, HEAD_DIM, 1), num_den)
        inv = 1.0 / jnp.maximum(den, 1e-30)
        oc = num_den * inv
        impu = acc[:, LANE:2 * LANE] * inv
        imp = impu[0:Q_BLOCK]
        for r in range(1, NSA_REP):
            imp = imp + impu[r * Q_BLOCK:(r + 1) * Q_BLOCK]
        yield
        s0 = _dot_nt(qa, kw_ref[g, pl.ds(t0, KV_TILE), :]) + tile0_bias(g)
        s1 = _dot_nt(qa, kw_ref[g, pl.ds(w1, Q_BLOCK), :]) + jnp.tile(wm_ref[...], (NSA_REP, 1))
        yield
        m = jnp.maximum(jnp.max(s0, axis=1, keepdims=True), jnp.max(s1, axis=1, keepdims=True))
        p0 = _softmax_exp(s0 - m).astype(BF16)
        p1 = _softmax_exp(s1 - m).astype(BF16)
        yield
        accw = (_dot(p0, vw_ref[g, pl.ds(t0, KV_TILE), :]) + _dot(p1, vw_ref[g, pl.ds(w1, Q_BLOCK), :]))
        fronts[g] = (oc, [head_out(accw, r) for r in range(NSA_REP)])
        yield
        notsel_q = not_selected(imp).astype(BF16)
        qsel_ref[g][:, 0:LANE] = qa
        for r in range(NSA_REP):
            qsel_ref[g][r * Q_BLOCK:(r + 1) * Q_BLOCK, LANE:2 * LANE] = notsel_q
        yield
        s = _dot_nt(qsel_ref[g][...], ks_ref[g, pl.ds(t0, KV_TILE), :]) + tile0_bias(g)
        yield
        s_ref[g][0][...] = s
        m_ref[g][0][...] = bcast(jnp.max(s, axis=1, keepdims=True))
        al_ref[g][0][...] = jnp.ones((rows, LANE), F32)
        acc_ref[g][...] = jnp.zeros((rows, 2 * LANE), F32)

    def not_selected(imp):
        jb = lax.broadcasted_iota(jnp.int32, (LANE, Q_BLOCK), 0)
        t = i * Q_BLOCK + lax.broadcasted_iota(jnp.int32, (LANE, Q_BLOCK), 1)
        cur = t // SEL_BLOCK
        forced = (jb == 0) | (jb == cur) | (jb == cur - 1)
        score = jnp.where(forced, FORCE_SCORE, jnp.where(jb * SEL_BLOCK <= t, imp.T, -1.0))
        if select_all:
            return jnp.zeros((Q_BLOCK, LANE), F32)
        sub = 8
        x = [score[sub * v:sub * (v + 1), :] for v in range(LANE // sub)]
        y = list(x)
        for a, b in _SORT16:
            y[a], y[b] = jnp.maximum(y[a], y[b]), jnp.minimum(y[a], y[b])
        for shift in (4, 2, 1):
            z = [jnp.maximum(y[k], pltpu.roll(y[n_sel - 1 - k], shift, 0)) for k in range(n_sel)]
            if shift > 1:
                for a, b in _BITONIC16:
                    z[a], z[b] = jnp.maximum(z[a], z[b]), jnp.minimum(z[a], z[b])
            y = z
        while len(y) > 1:
            y = [jnp.minimum(y[2 * k], y[2 * k + 1]) for k in range(len(y) // 2)]
        tau = y[0]
        gt = jnp.concatenate([jnp.where(xv > tau, 1.0, 0.0) for xv in x], axis=0)
        eq = jnp.concatenate([jnp.where(xv == tau, 1.0, 0.0) for xv in x], axis=0)
        kb = lax.broadcasted_iota(jnp.int32, (LANE, LANE), 1)
        n_above = _dot(jnp.ones((LANE, LANE), BF16), gt.astype(BF16))
        n_tied_before = _dot(jnp.where(kb < jb, 1.0, 0.0).astype(BF16), eq.astype(BF16))
        keep = (gt > 0.0) | ((eq > 0.0) & (n_above + n_tied_before < n_sel))
        return jnp.where(keep, 0.0, 1.0).T


    def qk_stage(g, k, prev, slot):
        st = pl.multiple_of(t0 - KV_TILE * k, Q_BLOCK)
        s_n = _dot_nt(qsel_ref[g][...], ks_ref[g, pl.ds(st, KV_TILE), :])
        s_ref[g][slot][...] = s_n
        m_c = m_ref[g][prev][...]
        m_n = jnp.maximum(m_c, bcast(jnp.max(s_n, axis=1, keepdims=True)))
        m_ref[g][slot][...] = m_n
        al_ref[g][slot][...] = _softmax_exp(m_c - m_n)

    def pv_stage(g, k, slot):
        st = pl.multiple_of(t0 - KV_TILE * k, Q_BLOCK)
        p = _softmax_exp(s_ref[g][slot][...] - lanes(m_ref[g][slot][...], KV_TILE // LANE))
        acc_ref[g][...] = (lanes(al_ref[g][slot][...], 2) * acc_ref[g][...]
                           + _dot(p.astype(BF16), vs_ref[g, pl.ds(st, KV_TILE), :]))

    groups = range(NSA_KV)
    pending = [front(g) for g in groups]
    for _ in range(_FRONT_SKEW):
        next(pending[0])
    while pending:
        for gen in list(pending):
            if next(gen, pending) is pending:
                pending.remove(gen)

    def tile_pair(j):
        for g in groups:
            qk_stage(g, 2 * j + 1, 0, 1)
        for g in groups:
            pv_stage(g, 2 * j, 0)
        for g in groups:
            qk_stage(g, 2 * j + 2, 1, 0)
        for g in groups:
            pv_stage(g, 2 * j + 1, 1)

    def two_pairs(j, carry):
        tile_pair(2 * j)
        tile_pair(2 * j + 1)
        return carry

    def one_pair(j, carry):
        tile_pair(j)
        return carry

    n_pairs = (n_tiles - 1) // 2
    lax.fori_loop(0, n_pairs // 2, two_pairs, 0)
    lax.fori_loop(2 * (n_pairs // 2), n_pairs, one_pair, 0)
    odd_left = n_tiles - 1 - 2 * n_pairs

    @pl.when(odd_left == 1)
    def _two_left():
        for g in groups:
            qk_stage(g, n_tiles - 1, 0, 1)
        for g in groups:
            pv_stage(g, n_tiles - 2, 0)
        for g in groups:
            pv_stage(g, n_tiles - 1, 1)

    @pl.when(odd_left == 0)
    def _one_left():
        for g in groups:
            pv_stage(g, n_tiles - 1, 0)

    for g in groups:
        oc, ow = fronts[g]
        acc = acc_ref[g][...]
        gx = _dot(gt_ref[...], gx_ref[g])

        def gate(branch, r):
            c = branch * NSA_REP + r
            return gx[:, c * LANE:(c + 1) * LANE]

        heads = []
        for r in range(NSA_REP):
            rs = slice(r * Q_BLOCK, (r + 1) * Q_BLOCK)
            oc_r = oc[rs] if r % 2 == 0 else pltpu.roll(oc[rs], HEAD_DIM, 1)
            heads.append(gate(0, r) * oc_r + gate(1, r) * head_out(acc, r) + gate(2, r) * ow[r])
        for pr in range(NSA_REP // 2):
            col = (g * NSA_REP // 2 + pr) * LANE
            o_ref[:, col:col + LANE] = jnp.where(lane < HEAD_DIM, heads[2 * pr], heads[2 * pr + 1])


def _nsa2(q_aug, gates, cmp_aug, ksel, vsel, kwin, vwin, tb, tc, wm, batch, seq):
    nq = seq // Q_BLOCK
    sp = seq + KV_PAD
    crow = cmp_aug.shape[3]
    n_sel = min(N_SELECT, seq // SEL_BLOCK)
    assert seq // SEL_BLOCK <= LANE and KV_PAD >= KV_TILE and KV_PAD >= WINDOW
    assert seq // CMP_STRIDE <= CMP_SPAN <= CMP_PAD and crow == CMP_PAD + seq // CMP_STRIDE
    assert seq // SEL_BLOCK <= N_SELECT or n_sel == LANE // 8
    n_gate = 3 * NSA_REP
    col = jnp.arange(n_gate * LANE) // LANE
    row = jnp.arange(2 * LANE) % LANE
    gate_spread = jnp.stack([(row[:, None] == g * n_gate + col[None, :]) for g in range(NSA_KV)]).astype(BF16)
    once = pl.Buffered(1)
    kv_spec = lambda w: pl.BlockSpec((None, NSA_KV, sp, w), lambda b, i: (b, 0, 0, 0), pipeline_mode=once)
    cmp_spec = lambda k, w: pl.BlockSpec((None, None, NSA_KV, crow, w), lambda b, i: (k, b, 0, 0, 0),
                                         pipeline_mode=once)
    in_specs = [
        pl.BlockSpec((None, NSA_HEADS, Q_BLOCK, LANE), lambda b, i: (b, 0, i, 0)),
        pl.BlockSpec((Q_BLOCK, 2 * LANE), lambda b, i: (b * nq + i, 0)),
        cmp_spec(0, LANE), cmp_spec(1, 2 * LANE),
        kv_spec(2 * LANE), kv_spec(2 * LANE), kv_spec(LANE), kv_spec(2 * LANE),
        _const_spec((NSA_HEADS, Q_BLOCK, 2 * Q_BLOCK)),
        _const_spec((NSA_HEADS, Q_BLOCK, LANE)),
        _const_spec((Q_BLOCK, LANE)),
        _const_spec((NSA_KV, 2 * LANE, n_gate * LANE)),
    ]
    rows = NSA_REP * Q_BLOCK
    group_scratch = [pltpu.VMEM((rows, 2 * LANE), BF16),
                     pltpu.VMEM((rows, KV_TILE), F32),
                     pltpu.VMEM((rows, KV_TILE), F32),
                     pltpu.VMEM((rows, LANE), F32),
                     pltpu.VMEM((rows, LANE), F32),
                     pltpu.VMEM((rows, LANE), F32),
                     pltpu.VMEM((rows, LANE), F32),
                     pltpu.VMEM((rows, 2 * LANE), F32)]
    assert len(group_scratch) == _N_GROUP_SCRATCH
    return pl.pallas_call(
        functools.partial(_nsa2_kernel, n_sel=n_sel, select_all=seq // SEL_BLOCK <= N_SELECT),
        grid=(batch, nq),
        in_specs=in_specs,
        out_specs=pl.BlockSpec((Q_BLOCK, NSA_WIDTH), lambda b, i: (b * nq + i, 0)),
        out_shape=jax.ShapeDtypeStruct((batch * seq, NSA_WIDTH), F32),
        scratch_shapes=group_scratch * NSA_KV,
        compiler_params=_params(("arbitrary", "arbitrary")),
        name="nsa",
    )(q_aug, gates, cmp_aug, cmp_aug, ksel, vsel, kwin, vwin, tb, tc, wm, gate_spread)


def _memkv_kernel(mem_ref, g_ref, wk_ref, wv_ref, k_ref, v_ref):
    h = _rms(mem_ref[...], g_ref[...]).astype(BF16)
    k_ref[...] = _dot(h, wk_ref[...]).astype(BF16)
    v_ref[...] = _dot(h, wv_ref[...]).astype(BF16)


def _mem_kv(mem, g_mem, wkv):
    batch, m, _ = mem.shape
    blk = pl.BlockSpec((None, m, D_MODEL), lambda b: (b, 0, 0))
    return pl.pallas_call(
        _memkv_kernel,
        grid=(batch,),
        in_specs=[blk, _const_spec((1, D_MODEL)), _const_spec((D_MODEL, D_MODEL)),
                  _const_spec((D_MODEL, D_MODEL))],
        out_specs=[blk, blk],
        out_shape=[jax.ShapeDtypeStruct((batch, m, D_MODEL), BF16)] * 2,
        compiler_params=_params(("arbitrary",)),
        name="mem_kv",
    )(mem, g_mem.reshape(1, -1), wkv[:, :D_MODEL].astype(BF16), wkv[:, D_MODEL:].astype(BF16))


def _mixout_kernel(x_ref, ya_ref, yb_ref, gb_ref, wo_ref, post_ref, xpre_ref, wq_ref,
                   km_ref, vm_ref, wxo_ref, xpost_ref, o_ref):
    tm = x_ref.shape[0]

    def chain(rows):
        yb = _rms(yb_ref[rows, :], gb_ref[...]).astype(BF16)
        ya = ya_ref[rows, :].astype(BF16)
        yield
        y = _dot(ya, wo_ref[0:GM_WIDTH, :]) + _dot(yb, wo_ref[GM_WIDTH:, :])
        yield
        x1 = x_ref[rows, :] + _rms(y, post_ref[...])
        h = _rms(x1, xpre_ref[...]).astype(BF16)
        yield
        q = _dot(h, wq_ref[...]).astype(BF16)
        yield
        outs = []
        for hd in range(XA_HEADS):
            hs = slice(hd * XA_HEAD_DIM, (hd + 1) * XA_HEAD_DIM)
            s = _dot_nt(q[:, hs], km_ref[:, hs])
            yield
            e = _softmax_exp(s - jnp.max(s, axis=1, keepdims=True))
            den = jnp.sum(e, axis=1, keepdims=True)
            e = e.astype(BF16)
            yield
            outs.append((_dot(e, vm_ref[:, hs]) / den).astype(BF16))
        o = _dot(jnp.concatenate(outs, axis=1), wxo_ref[...])
        yield
        o_ref[rows, :] = x1 + _rms(o, xpost_ref[...])

    n_slices = 2
    pending = [chain(slice(k * tm // n_slices, (k + 1) * tm // n_slices)) for k in range(n_slices)]
    next(pending[0])
    while pending:
        for gen in list(pending):
            if next(gen, pending) is pending:
                pending.remove(gen)


def _mix_out(x2d, ya, yb, batch, seq, gain_b, w_out, mix_post, xa_pre, wq, kmem, vmem, wo, xa_post):
    tm = 2 * TOKEN_TILE if seq % (2 * TOKEN_TILE) == 0 else TOKEN_TILE
    nt = seq // tm
    row = lambda w: pl.BlockSpec((tm, w), lambda b, t: (b * nt + t, 0))
    mem_spec = pl.BlockSpec((None, kmem.shape[1], D_MODEL), lambda b, t: (b, 0, 0))
    sq = _const_spec((D_MODEL, D_MODEL))
    vec = _const_spec((1, D_MODEL))
    return pl.pallas_call(
        _mixout_kernel,
        grid=(batch, nt),
        in_specs=[row(D_MODEL), row(GM_WIDTH), row(NSA_WIDTH), _const_spec((1, NSA_WIDTH)), sq, vec,
                  vec, sq, mem_spec, mem_spec, sq, vec],
        out_specs=row(D_MODEL),
        out_shape=jax.ShapeDtypeStruct(x2d.shape, F32),
        compiler_params=_params(("arbitrary", "arbitrary")),
        name="mix_out",
    )(x2d, ya, yb, gain_b.reshape(1, -1), w_out.astype(BF16), mix_post.reshape(1, -1),
      xa_pre.reshape(1, -1), (wq * (XA_HEAD_DIM ** -0.5 * LOG2E)).astype(BF16), kmem, vmem,
      wo.astype(BF16), xa_post.reshape(1, -1))


def kernel(x, mem, ffn1_pre, ffn1_post, ffn1_wg, ffn1_wu, ffn1_wd, mix_pre, mix_post, w_in, gm_ln_g, gm_ln_b, gm_ws, gm_bs, ck_pe, ck_w1, ck_b1, ck_w2, cv_pe, cv_w1, cv_b1, cv_w2, rel_bias, out_gain_a, out_gain_b, w_out, xa_pre, xa_post, mem_norm, xa_wq, xa_wkv, xa_wo, ffn2_pre, ffn2_post, ffn2_wg, ffn2_wu, ffn2_wd):
    batch, seq, _ = x.shape
    depth = ffn1_pre.shape[0]
    x2d = x.reshape(batch * seq, D_MODEL)
    tb, tc, wm = _bias_tables(rel_bias)
    for l in range(depth):
        x2d = _ffn(x2d, ffn1_pre[l], ffn1_wg[l], ffn1_wu[l], ffn1_wd[l], ffn1_post[l])
        ya, q_aug, cmp_raw, ksel, vsel, kwin, vwin, gates = _mixproj(
            x2d, batch, seq, mix_pre[l], w_in[l], gm_ln_g[l], gm_ln_b[l], gm_ws[l], gm_bs[l],
            out_gain_a[l])
        cmp_aug = _compress(cmp_raw, batch, seq,
                            jnp.stack([ck_pe[l], cv_pe[l]]), jnp.stack([ck_w1[l], cv_w1[l]]),
                            jnp.stack([ck_b1[l], cv_b1[l]]), jnp.stack([ck_w2[l], cv_w2[l]]))
        yb = _nsa2(q_aug, gates, cmp_aug, ksel, vsel, kwin, vwin, tb, tc, wm, batch, seq)
        kmem, vmem = _mem_kv(mem, mem_norm[l], xa_wkv[l])
        x2d = _mix_out(x2d, ya, yb, batch, seq, out_gain_b[l], w_out[l], mix_post[l], xa_pre[l],
                       xa_wq[l], kmem, vmem, xa_wo[l], xa_post[l])
        x2d = _ffn(x2d, ffn2_pre[l], ffn2_wg[l], ffn2_wu[l], ffn2_wd[l], ffn2_post[l])
    return x2d.reshape(batch, seq, D_MODEL)
```

```python
import functools
import math

import jax
import jax.numpy as jnp
from jax import lax
from jax.experimental import pallas as pl
from jax.experimental.pallas import tpu as pltpu

F32 = jnp.float32
BF16 = jnp.bfloat16

D_MODEL = 1024
MEM_LEN = 256
EPS = 1e-6
D_FF = 2816
GM_WIDTH = 512
GM_GROUPS = 4
GM_CH = GM_WIDTH // GM_GROUPS
CHUNK = 128
NSA_WIDTH = D_MODEL - GM_WIDTH
HEAD_DIM = 64
NSA_HEADS = NSA_WIDTH // HEAD_DIM
NSA_KV = 2
NSA_REP = NSA_HEADS // NSA_KV
KV_W = NSA_KV * HEAD_DIM
CMP_BLOCK = 32
CMP_STRIDE = 16
CMP_HIDDEN = 256
SEL_BLOCK = 64
N_SELECT = 16
WINDOW = 512
Q_BLOCK = 128
N_BUCKETS = 32
MAX_DISTANCE = 128
XA_HEADS = 4
XA_HEAD_DIM = D_MODEL // XA_HEADS
FORCE_SCORE = 1e4

LANE = 128
TOKEN_TILE = 512
FF_CHUNK = 256
KV_PAD = 512
KV_TILE = 512
CMP_PAD = 512
CMP_SPAN = 512
MASK_BF16 = -(2.0 ** 100)
MASK_F32 = -1e30
MAX_CLAMP = -(2.0 ** 99)
LOG2E = math.log2(math.e)
_softmax_exp = jnp.exp2
VMEM_LIMIT = 56 * 1024 * 1024

_NT = (((1,), (1,)), ((), ()))


def _rms(x, g):
    ms = jnp.mean(x * x, axis=-1, keepdims=True)
    return x * lax.rsqrt(ms + EPS) * g


def _dot(a, b):
    return jnp.dot(a, b, preferred_element_type=F32)


def _dot_nt(a, b):
    return lax.dot_general(a, b, _NT, preferred_element_type=F32)


def _params(sem, vmem=VMEM_LIMIT):
    return pltpu.CompilerParams(dimension_semantics=sem, vmem_limit_bytes=vmem)


def _const_spec(shape):
    nd = len(shape)
    return pl.BlockSpec(shape, lambda *_: (0,) * nd)


def _interleave(chains, skew=1):
    pending = list(chains)
    for _ in range(skew):
        next(pending[0])
    while pending:
        for gen in list(pending):
            if next(gen, pending) is pending:
                pending.remove(gen)


def _ffn_kernel(x_ref, pre_ref, wg_ref, wu_ref, wd_ref, post_ref, o_ref, acc_ref):
    tm = x_ref.shape[0]

    def chain(rows):
        h = _rms(x_ref[rows, :], pre_ref[...]).astype(BF16)
        yield
        for c in range(D_FF // FF_CHUNK):
            sl = slice(c * FF_CHUNK, (c + 1) * FF_CHUNK)
            g = _dot(h, wg_ref[:, sl])
            u = _dot(h, wu_ref[:, sl])
            a = (jax.nn.silu(g) * u).astype(BF16)
            d = _dot(a, wd_ref[sl, :])
            if c == 0:
                acc_ref[rows, :] = d
            else:
                acc_ref[rows, :] += d
            yield
        o_ref[rows, :] = x_ref[rows, :] + 0.5 * _rms(acc_ref[rows, :], post_ref[...])

    _interleave([chain(slice(k * tm // 2, (k + 1) * tm // 2)) for k in range(2)])


def _ffn(x2d, pre, wg, wu, wd, post):
    n = x2d.shape[0]
    tm = 2 * TOKEN_TILE if n % (2 * TOKEN_TILE) == 0 else TOKEN_TILE
    assert n % tm == 0
    row = pl.BlockSpec((tm, D_MODEL), lambda t: (t, 0))
    once = pl.Buffered(1)
    weight = lambda shape: pl.BlockSpec(shape, lambda t: (0, 0), pipeline_mode=once)
    return pl.pallas_call(
        _ffn_kernel,
        grid=(n // tm,),
        in_specs=[row, _const_spec((1, D_MODEL)), weight((D_MODEL, D_FF)),
                  weight((D_MODEL, D_FF)), weight((D_FF, D_MODEL)),
                  _const_spec((1, D_MODEL))],
        out_specs=row,
        out_shape=jax.ShapeDtypeStruct((n, D_MODEL), F32),
        scratch_shapes=[pltpu.VMEM((tm, D_MODEL), F32)],
        compiler_params=_params(("arbitrary",)),
        name="ffn",
    )(x2d, pre.reshape(1, -1), wg.astype(BF16), wu.astype(BF16), wd.astype(BF16),
      post.reshape(1, -1))


def _mixproj_kernel(x_ref, pre_ref, w_ref, wgt_ref, lng_ref, lnb_ref, ws_ref, bs_ref, ga_ref,
                    ya_ref, q_ref, cmp_ref, ksel_ref, vsel_ref, kwin_ref, vwin_ref, gt_ref, cmps_ref):
    t = pl.program_id(1)
    tm = x_ref.shape[0]
    lane = lax.broadcasted_iota(jnp.int32, (tm, LANE), 1)
    flag_col = jnp.where(lane == HEAD_DIM, MASK_BF16, 0.0).astype(BF16)

    @pl.when(t == 0)
    def _pad_rows():
        for g in range(NSA_KV):
            ksel_ref[g, :, 0:LANE] = flag_col
            ksel_ref[g, :, LANE:2 * LANE] = jnp.zeros((tm, LANE), BF16)
            kwin_ref[g] = flag_col
            vsel_ref[g] = jnp.zeros((tm, 2 * LANE), BF16)
            vwin_ref[g] = jnp.zeros((tm, 2 * LANE), BF16)

    def chain(r0, n):
        rows = slice(r0, r0 + n)
        h = _rms(x_ref[rows, :], pre_ref[...]).astype(BF16)
        yield
        u = jax.nn.gelu(_dot(h, w_ref[:, 0:GM_WIDTH]))
        v = jax.nn.gelu(_dot(h, w_ref[:, GM_WIDTH:2 * GM_WIDTH]))
        yield
        r_i = lax.broadcasted_iota(jnp.int32, (CHUNK, CHUNK), 0)
        c_i = lax.broadcasted_iota(jnp.int32, (CHUNK, CHUNK), 1)
        for g in range(GM_GROUPS):
            gs = slice(g * GM_CH, (g + 1) * GM_CH)
            vg = v[:, gs]
            mu = jnp.mean(vg, axis=-1, keepdims=True)
            var = jnp.mean(jnp.square(vg - mu), axis=-1, keepdims=True)
            vn = ((vg - mu) * lax.rsqrt(var + EPS) * lng_ref[:, gs] + lnb_ref[:, gs]).astype(BF16)
            w = jnp.where(c_i <= r_i, ws_ref[g], 0.0).astype(BF16)
            for c in range(n // CHUNK):
                rs = slice(c * CHUNK, (c + 1) * CHUNK)
                s = _dot(w, vn[rs]) + bs_ref[:, g:g + 1]
                ya_ref[r0 + c * CHUNK:r0 + (c + 1) * CHUNK, gs] = u[rs, gs] * s
        yield
        ya_ref[rows, :] = _rms(ya_ref[rows, :], ga_ref[...])
        lane = lax.broadcasted_iota(jnp.int32, (n, LANE), 1)
        lane_lo = lane < HEAD_DIM

        def halves(z):
            return z, pltpu.roll(z, HEAD_DIM, 1)

        one_col = jnp.where(lane == HEAD_DIM, 1.0, 0.0)
        zq = _dot(h, w_ref[:, _O_Q:_O_C])
        yield
        for a in range(NSA_HEADS // 2):
            for hd, qh in zip((2 * a, 2 * a + 1), halves(zq[:, a * LANE:(a + 1) * LANE])):
                q_ref[hd, rows, :] = jnp.where(lane_lo, qh, one_col).astype(BF16)
        zc = _dot(h, w_ref[:, _O_C:_O_KV])
        yield
        nr = n // CMP_STRIDE
        urows = slice(r0 // CMP_STRIDE, r0 // CMP_STRIDE + nr)
        lo_u = lax.broadcasted_iota(jnp.int32, (nr, LANE), 1) < HEAD_DIM
        for kv in range(2):
            cmps_ref[kv, rows, :] = zc[:, kv * KV_W:(kv + 1) * KV_W]
            for a in range(CMP_STRIDE // 2):
                za = cmps_ref[kv, pl.ds(r0 + 2 * a, nr, stride=CMP_STRIDE), :]
                zb = cmps_ref[kv, pl.ds(r0 + 2 * a + 1, nr, stride=CMP_STRIDE), :]
                cmp_ref[kv, urows, a * LANE:(a + 1) * LANE] = jnp.where(
                    lo_u, za, pltpu.roll(zb, HEAD_DIM, 1))
                cmp_ref[kv, urows, (CMP_STRIDE // 2 + a) * LANE:(CMP_STRIDE // 2 + a + 1) * LANE] = (
                    jnp.where(lo_u, pltpu.roll(za, HEAD_DIM, 1), zb))
        zkv = _dot(h, w_ref[:, _O_KV:_O_GATE])
        yield
        k_sel, v_sel, k_win, v_win = (halves(zkv[:, j * LANE:(j + 1) * LANE]) for j in range(4))
        pos_blk = ((t - 1) * tm + r0 + lax.broadcasted_iota(jnp.int32, (n, LANE), 0)) // SEL_BLOCK
        blk_onehot = jnp.where(lane == pos_blk, MASK_BF16, 0.0).astype(BF16)

        def v_aug(v, g):
            return jnp.concatenate([jnp.where(lane_lo, v[g], 1.0), jnp.where(lane_lo, 1.0, v[1 - g])],
                                   axis=1).astype(BF16)

        for g in range(NSA_KV):
            ksel_ref[g, rows, 0:LANE] = jnp.where(lane_lo, k_sel[g], 0.0).astype(BF16)
            ksel_ref[g, rows, LANE:2 * LANE] = blk_onehot
            vsel_ref[g, rows, :] = v_aug(v_sel, g)
            kwin_ref[g, rows, :] = jnp.where(lane_lo, k_win[g], 0.0).astype(BF16)
            vwin_ref[g, rows, :] = v_aug(v_win, g)
        sg = jax.nn.sigmoid(_dot(h, wgt_ref[...]))
        hi = sg.astype(BF16)
        gt_ref[rows, 0:LANE] = hi
        gt_ref[rows, LANE:2 * LANE] = (sg - hi.astype(F32)).astype(BF16)

    @pl.when(t > 0)
    def _tile():
        _interleave([chain(k * tm // 2, tm // 2) for k in range(2)])


_O_Q = 2 * GM_WIDTH
_O_C = _O_Q + NSA_WIDTH
_O_KV = _O_C + 2 * KV_W
_O_GATE = _O_KV + 4 * KV_W


def _mixproj(x2d, batch, seq, mix_pre, w_in, ln_g, ln_b, ws, bs, gain_a):
    tm = TOKEN_TILE
    nt = seq // tm
    assert seq % tm == 0 and KV_PAD == tm
    scale = HEAD_DIM ** -0.5 * LOG2E
    w_main = jnp.concatenate([w_in[:, :_O_Q], w_in[:, _O_Q:_O_C] * scale, w_in[:, _O_C:_O_GATE]],
                             axis=1).astype(BF16)
    n_gate = 3 * NSA_HEADS
    w_g = w_in[:, _O_GATE:].reshape(D_MODEL, NSA_KV, NSA_REP, 3).transpose(0, 1, 3, 2)
    w_g = jnp.pad(w_g.reshape(D_MODEL, n_gate), ((0, 0), (0, LANE - n_gate))).astype(BF16)

    def tile(b, t):
        return b * nt + jnp.maximum(t - 1, 0)

    n = batch * seq
    sp = seq + KV_PAD
    out_shape = [
        jax.ShapeDtypeStruct((n, GM_WIDTH), F32),
        jax.ShapeDtypeStruct((batch, NSA_HEADS, seq, LANE), BF16),
        jax.ShapeDtypeStruct((2, n // CMP_STRIDE, CMP_STRIDE * KV_W), F32),
        jax.ShapeDtypeStruct((batch, NSA_KV, sp, 2 * LANE), BF16),
        jax.ShapeDtypeStruct((batch, NSA_KV, sp, 2 * LANE), BF16),
        jax.ShapeDtypeStruct((batch, NSA_KV, sp, LANE), BF16),
        jax.ShapeDtypeStruct((batch, NSA_KV, sp, 2 * LANE), BF16),
        jax.ShapeDtypeStruct((n, 2 * LANE), BF16),
    ]
    padded = lambda w: pl.BlockSpec((None, NSA_KV, tm, w), lambda b, t: (b, 0, t, 0))
    out_specs = [
        pl.BlockSpec((tm, GM_WIDTH), lambda b, t: (tile(b, t), 0)),
        pl.BlockSpec((None, NSA_HEADS, tm, LANE), lambda b, t: (b, 0, jnp.maximum(t - 1, 0), 0)),
        pl.BlockSpec((2, tm // CMP_STRIDE, CMP_STRIDE * KV_W), lambda b, t: (0, tile(b, t), 0)),
        padded(2 * LANE), padded(2 * LANE), padded(LANE), padded(2 * LANE),
        pl.BlockSpec((tm, 2 * LANE), lambda b, t: (tile(b, t), 0)),
    ]
    in_specs = [
        pl.BlockSpec((tm, D_MODEL), lambda b, t: (tile(b, t), 0)),
        _const_spec((1, D_MODEL)),
        _const_spec(w_main.shape), _const_spec(w_g.shape),
        _const_spec((1, GM_WIDTH)), _const_spec((1, GM_WIDTH)),
        _const_spec((GM_GROUPS, CHUNK, CHUNK)), _const_spec((CHUNK, GM_GROUPS)),
        _const_spec((1, GM_WIDTH)),
    ]
    return pl.pallas_call(
        _mixproj_kernel,
        grid=(batch, nt + 1),
        in_specs=in_specs,
        out_specs=out_specs,
        out_shape=out_shape,
        scratch_shapes=[pltpu.VMEM((2, tm, KV_W), F32)],
        compiler_params=_params(("arbitrary", "arbitrary")),
        name="mixproj",
    )(x2d, mix_pre.reshape(1, -1), w_main, w_g, ln_g.reshape(1, -1), ln_b.reshape(1, -1),
      ws, bs.T, gain_a.reshape(1, -1))


def _compress_kernel(raw_ref, pe_ref, w1_ref, b1_ref, w2_ref, o_ref):
    is_v = pl.program_id(0) == 1
    nu = raw_ref.shape[0]
    half = CMP_STRIDE * HEAD_DIM
    outs = []
    for g in range(NSA_KV):
        u = raw_ref[:, g * half:(g + 1) * half]
        a = _dot((u + pe_ref[0]).astype(BF16), w1_ref[0:half, :])
        b = _dot((u + pe_ref[1]).astype(BF16), w1_ref[half:2 * half, :])
        hid = jax.nn.gelu(a + pltpu.roll(b, nu - 1, 0) + b1_ref[...])
        outs.append(_dot(hid.astype(BF16), w2_ref[...]))
    lane = lax.broadcasted_iota(jnp.int32, (nu, LANE), 1)
    ci = lax.broadcasted_iota(jnp.int32, (nu, LANE), 0)
    u_c = CMP_BLOCK // CMP_STRIDE
    u_s = SEL_BLOCK // CMP_STRIDE
    overlap = jnp.maximum(jnp.minimum(ci + u_c, u_s * (lane + 1)) - jnp.maximum(ci, u_s * lane), 0)
    extra = jnp.where(is_v, overlap.astype(F32), 0.0)
    ones_hi = jnp.where(jnp.logical_and(is_v, lane >= HEAD_DIM), 1.0, 0.0)
    lane_p = lax.broadcasted_iota(jnp.int32, (CMP_PAD, LANE), 1)
    pad_lo = jnp.where(jnp.logical_and(jnp.logical_not(is_v), lane_p == HEAD_DIM), MASK_BF16, 0.0)
    for g in range(NSA_KV):
        o_ref[g, 0:CMP_PAD, 0:LANE] = pad_lo
        o_ref[g, 0:CMP_PAD, LANE:2 * LANE] = jnp.zeros((CMP_PAD, LANE), F32)
        o_ref[g, CMP_PAD:CMP_PAD + nu, 0:LANE] = outs[g] + ones_hi
        o_ref[g, CMP_PAD:CMP_PAD + nu, LANE:2 * LANE] = extra


def _compress(cmp_raw, batch, seq, pe, w1, b1, w2):
    nu = seq // CMP_STRIDE
    half = CMP_STRIDE * HEAD_DIM
    kd = NSA_KV * half
    raw = cmp_raw.reshape(2, batch, nu, kd)
    w2p = jnp.pad(w2, ((0, 0), (0, 0), (0, LANE - HEAD_DIM)))
    rows = CMP_PAD + nu
    return pl.pallas_call(
        _compress_kernel,
        grid=(2, batch),
        in_specs=[
            pl.BlockSpec((None, None, nu, kd), lambda k, b: (k, b, 0, 0)),
            pl.BlockSpec((None, 2, 1, half), lambda k, b: (k, 0, 0, 0)),
            pl.BlockSpec((None, 2 * half, CMP_HIDDEN), lambda k, b: (k, 0, 0)),
            pl.BlockSpec((None, 1, CMP_HIDDEN), lambda k, b: (k, 0, 0)),
            pl.BlockSpec((None, CMP_HIDDEN, LANE), lambda k, b: (k, 0, 0)),
        ],
        out_specs=pl.BlockSpec((None, None, NSA_KV, rows, 2 * LANE), lambda k, b: (k, b, 0, 0, 0)),
        out_shape=jax.ShapeDtypeStruct((2, batch, NSA_KV, rows, 2 * LANE), F32),
        compiler_params=_params(("arbitrary", "arbitrary")),
        name="compress",
    )(raw, pe.reshape(2, 2, 1, half), w1.astype(BF16), b1.reshape(2, 1, CMP_HIDDEN), w2p.astype(BF16))


def _t5_bucket(dist):
    n = jnp.maximum(dist, 0)
    max_exact = N_BUCKETS // 2
    nf = jnp.maximum(n, max_exact).astype(F32)
    large = max_exact + jnp.floor(jnp.log(nf * (1.0 / max_exact)) / math.log(MAX_DISTANCE / max_exact)
                                  * (N_BUCKETS - max_exact)).astype(jnp.int32)
    large = jnp.minimum(large, N_BUCKETS - 1)
    return jnp.where(n < max_exact, n, large)


def _tables_kernel(rb_ref, tb_ref, tc_ref, wm_ref):
    def fill(out_ref, dist):
        bk = _t5_bucket(dist)
        for h in range(NSA_HEADS):
            far = rb_ref[N_BUCKETS - 1, h]
            val = jnp.zeros(dist.shape, F32)
            for b in range(N_BUCKETS - 1):
                val = jnp.where(bk == b, (rb_ref[b, h] - far) * LOG2E, val)
            out_ref[h] = jnp.where(dist >= 0, val, MASK_F32)

    tq = lax.broadcasted_iota(jnp.int32, (Q_BLOCK, 2 * Q_BLOCK), 0)
    j = lax.broadcasted_iota(jnp.int32, (Q_BLOCK, 2 * Q_BLOCK), 1)
    fill(tb_ref, tq + Q_BLOCK - j)
    tq = lax.broadcasted_iota(jnp.int32, (Q_BLOCK, LANE), 0)
    c = lax.broadcasted_iota(jnp.int32, (Q_BLOCK, LANE), 1)
    cmp_first = Q_BLOCK // CMP_STRIDE - LANE
    fill(tc_ref, tq - CMP_STRIDE * (cmp_first + c) - (CMP_BLOCK - 1))
    wm_ref[...] = jnp.where(c > tq, 0.0, MASK_F32)


def _bias_tables(rel_bias):
    return pl.pallas_call(
        _tables_kernel,
        in_specs=[pl.BlockSpec(memory_space=pltpu.SMEM)],
        out_shape=[jax.ShapeDtypeStruct((NSA_HEADS, Q_BLOCK, 2 * Q_BLOCK), F32),
                   jax.ShapeDtypeStruct((NSA_HEADS, Q_BLOCK, LANE), F32),
                   jax.ShapeDtypeStruct((Q_BLOCK, LANE), F32)],
        name="bias_tables",
    )(rel_bias)


_N_GROUP_SCRATCH = 8


def _oddeven_merge_sort(n):
    def merge(lo, hi, r):
        step = 2 * r
        if step < hi - lo:
            yield from merge(lo, hi, step)
            yield from merge(lo + r, hi, step)
            yield from ((k, k + r) for k in range(lo + r, hi - r, step))
        else:
            yield (lo, lo + r)

    def sort(lo, hi):
        if hi > lo:
            mid = lo + (hi - lo) // 2
            yield from sort(lo, mid)
            yield from sort(mid + 1, hi)
            yield from merge(lo, hi, 1)

    return tuple(sort(0, n - 1))


def _bitonic_merge(n):
    pairs, k = [], n // 2
    while k >= 1:
        pairs += [(a, a + k) for a in range(n) if not a & k]
        k //= 2
    return tuple(pairs)


_SORT16 = _oddeven_merge_sort(N_SELECT)
_BITONIC16 = _bitonic_merge(N_SELECT)


def _nsa_kernel(q_ref, gt_ref, kc_ref, vc_ref, ks_ref, vs_ref, kw_ref, vw_ref,
                 tb_ref, tc_ref, wm_ref, gx_ref, o_ref, *scratch, n_sel, select_all):
    i = pl.program_id(1)
    rows = NSA_REP * Q_BLOCK
    per_group = [scratch[g * _N_GROUP_SCRATCH:(g + 1) * _N_GROUP_SCRATCH] for g in range(NSA_KV)]
    qsel_ref = [s[0] for s in per_group]
    s_ref = [(s[1], s[2]) for s in per_group]
    m_ref = [(s[3], s[4]) for s in per_group]
    al_ref = [(s[5], s[6]) for s in per_group]
    acc_ref = [s[7] for s in per_group]

    zero_tb = jnp.zeros((rows, KV_TILE - 2 * Q_BLOCK), F32)
    c0 = pl.multiple_of(8 * i + 8 + CMP_PAD - CMP_SPAN, 8)
    t0 = pl.multiple_of(Q_BLOCK * (i + 1) + KV_PAD - KV_TILE, Q_BLOCK)
    w1 = pl.multiple_of(t0 - Q_BLOCK, Q_BLOCK)
    n_tiles = (i + 1 + KV_TILE // Q_BLOCK - 1) // (KV_TILE // Q_BLOCK)
    lane = lax.broadcasted_iota(jnp.int32, (Q_BLOCK, LANE), 1)

    def bcast(col):
        return jnp.broadcast_to(col, (rows, LANE))

    def lanes(x, n):
        return jnp.concatenate([x] * n, axis=1)

    def head_out(acc, r):
        rs = slice(r * Q_BLOCK, (r + 1) * Q_BLOCK)
        a, b = acc[rs, 0:LANE], acc[rs, LANE:2 * LANE]
        return a * (1.0 / b) if r % 2 == 0 else b * (1.0 / a)

    def heads_of(g):
        return slice(g * NSA_REP, (g + 1) * NSA_REP)

    def q_of(g):
        return q_ref[heads_of(g)].reshape(rows, LANE)

    def tile0_bias(g):
        return jnp.concatenate([zero_tb, tb_ref[heads_of(g)].reshape(rows, 2 * Q_BLOCK)], axis=1)

    fronts = [None] * NSA_KV

    def front(g):
        hs = heads_of(g)
        qa = q_of(g)
        s = _dot_nt(qa, kc_ref[g, pl.ds(c0, CMP_SPAN), :].astype(BF16))
        s = s + jnp.concatenate([jnp.zeros((rows, CMP_SPAN - LANE), F32),
                                 tc_ref[hs].reshape(rows, LANE)], axis=1)
        yield
        m = jnp.maximum(jnp.max(s, axis=1, keepdims=True), MAX_CLAMP)
        p = _softmax_exp(s - m).astype(BF16)
        yield
        acc = _dot(p, vc_ref[g, pl.ds(c0, CMP_SPAN), :].astype(BF16))
        num_den = acc[:, 0:LANE]
        lane_r = lax.broadcasted_iota(jnp.int32, (rows, LANE), 1)
        den = jnp.where(lane_r < HEAD_DIM, pltpu.roll(num_den, HEAD_DIM, 1), num_den)
        inv = 1.0 / jnp.maximum(den, 1e-30)
        oc = num_den * inv
        impu = acc[:, LANE:2 * LANE] * inv
        imp = impu[0:Q_BLOCK]
        for r in range(1, NSA_REP):
            imp = imp + impu[r * Q_BLOCK:(r + 1) * Q_BLOCK]
        yield
        s0 = _dot_nt(qa, kw_ref[g, pl.ds(t0, KV_TILE), :]) + tile0_bias(g)
        s1 = _dot_nt(qa, kw_ref[g, pl.ds(w1, Q_BLOCK), :]) + jnp.tile(wm_ref[...], (NSA_REP, 1))
        yield
        m = jnp.maximum(jnp.max(s0, axis=1, keepdims=True), jnp.max(s1, axis=1, keepdims=True))
        p0 = _softmax_exp(s0 - m).astype(BF16)
        p1 = _softmax_exp(s1 - m).astype(BF16)
        yield
        accw = (_dot(p0, vw_ref[g, pl.ds(t0, KV_TILE), :]) + _dot(p1, vw_ref[g, pl.ds(w1, Q_BLOCK), :]))
        gx = _dot(gt_ref[...], gx_ref[g])

        def gate(branch, r):
            c = branch * NSA_REP + r
            return gx[:, c * LANE:(c + 1) * LANE]

        partial = []
        for r in range(NSA_REP):
            rs = slice(r * Q_BLOCK, (r + 1) * Q_BLOCK)
            oc_r = oc[rs] if r % 2 == 0 else pltpu.roll(oc[rs], HEAD_DIM, 1)
            partial.append(gate(0, r) * oc_r + gate(2, r) * head_out(accw, r))
        fronts[g] = (partial, [gate(1, r) for r in range(NSA_REP)])
        yield
        notsel_q = not_selected(imp).astype(BF16)
        qsel_ref[g][:, 0:LANE] = qa
        for r in range(NSA_REP):
            qsel_ref[g][r * Q_BLOCK:(r + 1) * Q_BLOCK, LANE:2 * LANE] = notsel_q
        yield
        s = _dot_nt(qsel_ref[g][...], ks_ref[g, pl.ds(t0, KV_TILE), :]) + tile0_bias(g)
        yield
        s_ref[g][0][...] = s
        m_ref[g][0][...] = bcast(jnp.max(s, axis=1, keepdims=True))
        al_ref[g][0][...] = jnp.ones((rows, LANE), F32)
        acc_ref[g][...] = jnp.zeros((rows, 2 * LANE), F32)

    def not_selected(imp):
        jb = lax.broadcasted_iota(jnp.int32, (LANE, Q_BLOCK), 0)
        t = i * Q_BLOCK + lax.broadcasted_iota(jnp.int32, (LANE, Q_BLOCK), 1)
        cur = t // SEL_BLOCK
        forced = (jb == 0) | (jb == cur) | (jb == cur - 1)
        score = jnp.where(forced, FORCE_SCORE, jnp.where(jb * SEL_BLOCK <= t, imp.T, -1.0))
        if select_all:
            return jnp.zeros((Q_BLOCK, LANE), F32)
        sub = 8
        x = [score[sub * v:sub * (v + 1), :] for v in range(LANE // sub)]
        y = list(x)
        for a, b in _SORT16:
            y[a], y[b] = jnp.maximum(y[a], y[b]), jnp.minimum(y[a], y[b])
        for shift in (4, 2, 1):
            z = [jnp.maximum(y[k], pltpu.roll(y[n_sel - 1 - k], shift, 0)) for k in range(n_sel)]
            if shift > 1:
                for a, b in _BITONIC16:
                    z[a], z[b] = jnp.maximum(z[a], z[b]), jnp.minimum(z[a], z[b])
            y = z
        while len(y) > 1:
            y = [jnp.minimum(y[2 * k], y[2 * k + 1]) for k in range(len(y) // 2)]
        tau = y[0]
        gt = jnp.concatenate([jnp.where(xv > tau, 1.0, 0.0) for xv in x], axis=0)
        eq = jnp.concatenate([jnp.where(xv == tau, 1.0, 0.0) for xv in x], axis=0)
        kb = lax.broadcasted_iota(jnp.int32, (LANE, LANE), 1)
        n_above = _dot(jnp.ones((LANE, LANE), BF16), gt.astype(BF16))
        n_tied_before = _dot(jnp.where(kb < jb, 1.0, 0.0).astype(BF16), eq.astype(BF16))
        keep = (gt > 0.0) | ((eq > 0.0) & (n_above + n_tied_before < n_sel))
        return jnp.where(keep, 0.0, 1.0).T


    def qk_stage(g, k, prev, slot):
        st = pl.multiple_of(t0 - KV_TILE * k, Q_BLOCK)
        s_n = _dot_nt(qsel_ref[g][...], ks_ref[g, pl.ds(st, KV_TILE), :])
        s_ref[g][slot][...] = s_n
        m_c = m_ref[g][prev][...]
        m_n = jnp.maximum(m_c, bcast(jnp.max(s_n, axis=1, keepdims=True)))
        m_ref[g][slot][...] = m_n
        al_ref[g][slot][...] = _softmax_exp(m_c - m_n)

    def pv_stage(g, k, slot):
        st = pl.multiple_of(t0 - KV_TILE * k, Q_BLOCK)
        p = _softmax_exp(s_ref[g][slot][...] - lanes(m_ref[g][slot][...], KV_TILE // LANE))
        acc_ref[g][...] = (lanes(al_ref[g][slot][...], 2) * acc_ref[g][...]
                           + _dot(p.astype(BF16), vs_ref[g, pl.ds(st, KV_TILE), :]))

    groups = range(NSA_KV)
    _interleave([front(g) for g in groups])

    def tile_pair(j):
        for g in groups:
            qk_stage(g, 2 * j + 1, 0, 1)
        for g in groups:
            pv_stage(g, 2 * j, 0)
        for g in groups:
            qk_stage(g, 2 * j + 2, 1, 0)
        for g in groups:
            pv_stage(g, 2 * j + 1, 1)

    def two_pairs(j, carry):
        tile_pair(2 * j)
        tile_pair(2 * j + 1)
        return carry

    def one_pair(j, carry):
        tile_pair(j)
        return carry

    n_pairs = (n_tiles - 1) // 2
    lax.fori_loop(0, n_pairs // 2, two_pairs, 0)
    lax.fori_loop(2 * (n_pairs // 2), n_pairs, one_pair, 0)
    odd_left = n_tiles - 1 - 2 * n_pairs

    @pl.when(odd_left == 1)
    def _two_left():
        for g in groups:
            qk_stage(g, n_tiles - 1, 0, 1)
        for g in groups:
            pv_stage(g, n_tiles - 2, 0)
        for g in groups:
            pv_stage(g, n_tiles - 1, 1)

    @pl.when(odd_left == 0)
    def _one_left():
        for g in groups:
            pv_stage(g, n_tiles - 1, 0)

    for g in groups:
        partial, sel_gate = fronts[g]
        acc = acc_ref[g][...]
        heads = [partial[r] + sel_gate[r] * head_out(acc, r) for r in range(NSA_REP)]
        for pr in range(NSA_REP // 2):
            col = (g * NSA_REP // 2 + pr) * LANE
            o_ref[:, col:col + LANE] = jnp.where(lane < HEAD_DIM, heads[2 * pr], heads[2 * pr + 1])


def _nsa(q_aug, gates, cmp_aug, ksel, vsel, kwin, vwin, tb, tc, wm, batch, seq):
    nq = seq // Q_BLOCK
    sp = seq + KV_PAD
    crow = cmp_aug.shape[3]
    n_sel = min(N_SELECT, seq // SEL_BLOCK)
    assert seq // SEL_BLOCK <= LANE and KV_PAD >= KV_TILE and KV_PAD >= WINDOW
    assert seq // CMP_STRIDE <= CMP_SPAN <= CMP_PAD and crow == CMP_PAD + seq // CMP_STRIDE
    assert seq // SEL_BLOCK <= N_SELECT or n_sel == LANE // 8
    n_gate = 3 * NSA_REP
    col = jnp.arange(n_gate * LANE) // LANE
    row = jnp.arange(2 * LANE) % LANE
    gate_spread = jnp.stack([(row[:, None] == g * n_gate + col[None, :]) for g in range(NSA_KV)]).astype(BF16)
    once = pl.Buffered(1)
    kv_spec = lambda w: pl.BlockSpec((None, NSA_KV, sp, w), lambda b, i: (b, 0, 0, 0), pipeline_mode=once)
    cmp_spec = lambda k, w: pl.BlockSpec((None, None, NSA_KV, crow, w), lambda b, i: (k, b, 0, 0, 0),
                                         pipeline_mode=once)
    in_specs = [
        pl.BlockSpec((None, NSA_HEADS, Q_BLOCK, LANE), lambda b, i: (b, 0, i, 0)),
        pl.BlockSpec((Q_BLOCK, 2 * LANE), lambda b, i: (b * nq + i, 0)),
        cmp_spec(0, LANE), cmp_spec(1, 2 * LANE),
        kv_spec(2 * LANE), kv_spec(2 * LANE), kv_spec(LANE), kv_spec(2 * LANE),
        _const_spec((NSA_HEADS, Q_BLOCK, 2 * Q_BLOCK)),
        _const_spec((NSA_HEADS, Q_BLOCK, LANE)),
        _const_spec((Q_BLOCK, LANE)),
        _const_spec((NSA_KV, 2 * LANE, n_gate * LANE)),
    ]
    rows = NSA_REP * Q_BLOCK
    group_scratch = [pltpu.VMEM((rows, 2 * LANE), BF16),
                     pltpu.VMEM((rows, KV_TILE), F32),
                     pltpu.VMEM((rows, KV_TILE), F32),
                     pltpu.VMEM((rows, LANE), F32),
                     pltpu.VMEM((rows, LANE), F32),
                     pltpu.VMEM((rows, LANE), F32),
                     pltpu.VMEM((rows, LANE), F32),
                     pltpu.VMEM((rows, 2 * LANE), F32)]
    assert len(group_scratch) == _N_GROUP_SCRATCH
    return pl.pallas_call(
        functools.partial(_nsa_kernel, n_sel=n_sel, select_all=seq // SEL_BLOCK <= N_SELECT),
        grid=(batch, nq),
        in_specs=in_specs,
        out_specs=pl.BlockSpec((Q_BLOCK, NSA_WIDTH), lambda b, i: (b * nq + i, 0)),
        out_shape=jax.ShapeDtypeStruct((batch * seq, NSA_WIDTH), F32),
        scratch_shapes=group_scratch * NSA_KV,
        compiler_params=_params(("arbitrary", "arbitrary")),
        name="nsa",
    )(q_aug, gates, cmp_aug, cmp_aug, ksel, vsel, kwin, vwin, tb, tc, wm, gate_spread)


def _memkv_kernel(mem_ref, g_ref, wk_ref, wv_ref, k_ref, v_ref):
    h = _rms(mem_ref[...], g_ref[...]).astype(BF16)
    k_ref[...] = _dot(h, wk_ref[...]).astype(BF16)
    v_ref[...] = _dot(h, wv_ref[...]).astype(BF16)


def _mem_kv(mem, g_mem, wkv):
    batch, m, _ = mem.shape
    blk = pl.BlockSpec((None, m, D_MODEL), lambda b: (b, 0, 0))
    return pl.pallas_call(
        _memkv_kernel,
        grid=(batch,),
        in_specs=[blk, _const_spec((1, D_MODEL)), _const_spec((D_MODEL, D_MODEL)),
                  _const_spec((D_MODEL, D_MODEL))],
        out_specs=[blk, blk],
        out_shape=[jax.ShapeDtypeStruct((batch, m, D_MODEL), BF16)] * 2,
        compiler_params=_params(("arbitrary",)),
        name="mem_kv",
    )(mem, g_mem.reshape(1, -1), wkv[:, :D_MODEL].astype(BF16), wkv[:, D_MODEL:].astype(BF16))


def _mixout_kernel(x_ref, ya_ref, yb_ref, gb_ref, wo_ref, post_ref, xpre_ref, wq_ref,
                   km_ref, vm_ref, wxo_ref, xpost_ref, o_ref):
    tm = x_ref.shape[0]

    def chain(rows):
        yb = _rms(yb_ref[rows, :], gb_ref[...]).astype(BF16)
        ya = ya_ref[rows, :].astype(BF16)
        yield
        y = _dot(ya, wo_ref[0:GM_WIDTH, :]) + _dot(yb, wo_ref[GM_WIDTH:, :])
        yield
        x1 = x_ref[rows, :] + _rms(y, post_ref[...])
        h = _rms(x1, xpre_ref[...]).astype(BF16)
        yield
        q = _dot(h, wq_ref[...]).astype(BF16)
        yield
        outs = []
        for hd in range(XA_HEADS):
            hs = slice(hd * XA_HEAD_DIM, (hd + 1) * XA_HEAD_DIM)
            s = _dot_nt(q[:, hs], km_ref[:, hs])
            yield
            e = _softmax_exp(s - jnp.max(s, axis=1, keepdims=True))
            den = jnp.sum(e, axis=1, keepdims=True)
            e = e.astype(BF16)
            yield
            outs.append((_dot(e, vm_ref[:, hs]) / den).astype(BF16))
        o = _dot(jnp.concatenate(outs, axis=1), wxo_ref[...])
        yield
        o_ref[rows, :] = x1 + _rms(o, xpost_ref[...])

    _interleave([chain(slice(k * tm // 2, (k + 1) * tm // 2)) for k in range(2)])


def _mix_out(x2d, ya, yb, batch, seq, gain_b, w_out, mix_post, xa_pre, wq, kmem, vmem, wo, xa_post):
    tm = 2 * TOKEN_TILE if seq % (2 * TOKEN_TILE) == 0 else TOKEN_TILE
    nt = seq // tm
    row = lambda w: pl.BlockSpec((tm, w), lambda b, t: (b * nt + t, 0))
    mem_spec = pl.BlockSpec((None, kmem.shape[1], D_MODEL), lambda b, t: (b, 0, 0))
    sq = _const_spec((D_MODEL, D_MODEL))
    vec = _const_spec((1, D_MODEL))
    return pl.pallas_call(
        _mixout_kernel,
        grid=(batch, nt),
        in_specs=[row(D_MODEL), row(GM_WIDTH), row(NSA_WIDTH), _const_spec((1, NSA_WIDTH)), sq, vec,
                  vec, sq, mem_spec, mem_spec, sq, vec],
        out_specs=row(D_MODEL),
        out_shape=jax.ShapeDtypeStruct(x2d.shape, F32),
        compiler_params=_params(("arbitrary", "arbitrary")),
        name="mix_out",
    )(x2d, ya, yb, gain_b.reshape(1, -1), w_out.astype(BF16), mix_post.reshape(1, -1),
      xa_pre.reshape(1, -1), (wq * (XA_HEAD_DIM ** -0.5 * LOG2E)).astype(BF16), kmem, vmem,
      wo.astype(BF16), xa_post.reshape(1, -1))


def kernel(x, mem, ffn1_pre, ffn1_post, ffn1_wg, ffn1_wu, ffn1_wd, mix_pre, mix_post, w_in, gm_ln_g, gm_ln_b, gm_ws, gm_bs, ck_pe, ck_w1, ck_b1, ck_w2, cv_pe, cv_w1, cv_b1, cv_w2, rel_bias, out_gain_a, out_gain_b, w_out, xa_pre, xa_post, mem_norm, xa_wq, xa_wkv, xa_wo, ffn2_pre, ffn2_post, ffn2_wg, ffn2_wu, ffn2_wd):
    batch, seq, _ = x.shape
    depth = ffn1_pre.shape[0]
    x2d = x.reshape(batch * seq, D_MODEL)
    tb, tc, wm = _bias_tables(rel_bias)
    for l in range(depth):
        x2d = _ffn(x2d, ffn1_pre[l], ffn1_wg[l], ffn1_wu[l], ffn1_wd[l], ffn1_post[l])
        ya, q_aug, cmp_raw, ksel, vsel, kwin, vwin, gates = _mixproj(
            x2d, batch, seq, mix_pre[l], w_in[l], gm_ln_g[l], gm_ln_b[l], gm_ws[l], gm_bs[l],
            out_gain_a[l])
        cmp_aug = _compress(cmp_raw, batch, seq,
                            jnp.stack([ck_pe[l], cv_pe[l]]), jnp.stack([ck_w1[l], cv_w1[l]]),
                            jnp.stack([ck_b1[l], cv_b1[l]]), jnp.stack([ck_w2[l], cv_w2[l]]))
        yb = _nsa(q_aug, gates, cmp_aug, ksel, vsel, kwin, vwin, tb, tc, wm, batch, seq)
        kmem, vmem = _mem_kv(mem, mem_norm[l], xa_wkv[l])
        x2d = _mix_out(x2d, ya, yb, batch, seq, out_gain_b[l], w_out[l], mix_post[l], xa_pre[l],
                       xa_wq[l], kmem, vmem, xa_wo[l], xa_post[l])
        x2d = _ffn(x2d, ffn2_pre[l], ffn2_wg[l], ffn2_wu[l], ffn2_wd[l], ffn2_post[l])
    return x2d.reshape(batch, seq, D_MODEL)
```

```python
import functools
import math

import jax
import jax.numpy as jnp
from jax import lax
from jax.experimental import pallas as pl
from jax.experimental.pallas import tpu as pltpu

F32 = jnp.float32
BF16 = jnp.bfloat16

D_MODEL = 1024
MEM_LEN = 256
EPS = 1e-6
D_FF = 2816
GM_WIDTH = 512
GM_GROUPS = 4
GM_CH = GM_WIDTH // GM_GROUPS
CHUNK = 128
NSA_WIDTH = D_MODEL - GM_WIDTH
HEAD_DIM = 64
NSA_HEADS = NSA_WIDTH // HEAD_DIM
NSA_KV = 2
NSA_REP = NSA_HEADS // NSA_KV
KV_W = NSA_KV * HEAD_DIM
CMP_BLOCK = 32
CMP_STRIDE = 16
CMP_HIDDEN = 256
SEL_BLOCK = 64
N_SELECT = 16
WINDOW = 512
Q_BLOCK = 128
N_BUCKETS = 32
MAX_DISTANCE = 128
XA_HEADS = 4
XA_HEAD_DIM = D_MODEL // XA_HEADS
FORCE_SCORE = 1e4

LANE = 128
TOKEN_TILE = 512
FF_CHUNK = 256
KV_PAD = 512
KV_TILE = 512
CMP_PAD = 512
CMP_SPAN = 512
MASK_BF16 = -(2.0 ** 100)
MASK_F32 = -1e30
MAX_CLAMP = -(2.0 ** 99)
LOG2E = math.log2(math.e)
_softmax_exp = jnp.exp2
VMEM_LIMIT = 56 * 1024 * 1024

_NT = (((1,), (1,)), ((), ()))


def _rms(x, g):
    ms = jnp.mean(x * x, axis=-1, keepdims=True)
    return x * lax.rsqrt(ms + EPS) * g


def _dot(a, b):
    return jnp.dot(a, b, preferred_element_type=F32)


def _dot_nt(a, b):
    return lax.dot_general(a, b, _NT, preferred_element_type=F32)


def _params(sem, vmem=VMEM_LIMIT):
    return pltpu.CompilerParams(dimension_semantics=sem, vmem_limit_bytes=vmem)


def _const_spec(shape):
    nd = len(shape)
    return pl.BlockSpec(shape, lambda *_: (0,) * nd)


def _interleave(chains, skew=1):
    pending = list(chains)
    for _ in range(skew):
        next(pending[0])
    while pending:
        for gen in list(pending):
            if next(gen, pending) is pending:
                pending.remove(gen)


def _ffn_kernel(x_ref, pre_ref, wg_ref, wu_ref, wd_ref, post_ref, o_ref, acc_ref):
    tm = x_ref.shape[0]

    def chain(rows):
        h = _rms(x_ref[rows, :], pre_ref[...]).astype(BF16)
        yield
        for c in range(D_FF // FF_CHUNK):
            sl = slice(c * FF_CHUNK, (c + 1) * FF_CHUNK)
            g = _dot(h, wg_ref[:, sl])
            u = _dot(h, wu_ref[:, sl])
            a = (jax.nn.silu(g) * u).astype(BF16)
            d = _dot(a, wd_ref[sl, :])
            if c == 0:
                acc_ref[rows, :] = d
            else:
                acc_ref[rows, :] += d
            yield
        o_ref[rows, :] = x_ref[rows, :] + 0.5 * _rms(acc_ref[rows, :], post_ref[...])

    _interleave([chain(slice(k * tm // 2, (k + 1) * tm // 2)) for k in range(2)])


def _ffn(x2d, pre, wg, wu, wd, post):
    n = x2d.shape[0]
    tm = 2 * TOKEN_TILE if n % (2 * TOKEN_TILE) == 0 else TOKEN_TILE
    assert n % tm == 0
    row = pl.BlockSpec((tm, D_MODEL), lambda t: (t, 0))
    once = pl.Buffered(1)
    weight = lambda shape: pl.BlockSpec(shape, lambda t: (0, 0), pipeline_mode=once)
    return pl.pallas_call(
        _ffn_kernel,
        grid=(n // tm,),
        in_specs=[row, _const_spec((1, D_MODEL)), weight((D_MODEL, D_FF)),
                  weight((D_MODEL, D_FF)), weight((D_FF, D_MODEL)),
                  _const_spec((1, D_MODEL))],
        out_specs=row,
        out_shape=jax.ShapeDtypeStruct((n, D_MODEL), F32),
        scratch_shapes=[pltpu.VMEM((tm, D_MODEL), F32)],
        compiler_params=_params(("arbitrary",)),
        name="ffn",
    )(x2d, pre.reshape(1, -1), wg.astype(BF16), wu.astype(BF16), wd.astype(BF16),
      post.reshape(1, -1))


def _mixproj_kernel(x_ref, pre_ref, w_ref, wgt_ref, lng_ref, lnb_ref, ws_ref, bs_ref, ga_ref,
                    ya_ref, q_ref, cmp_ref, ksel_ref, vsel_ref, kwin_ref, vwin_ref, gt_ref, cmps_ref):
    t = pl.program_id(1)
    tm = x_ref.shape[0]
    lane = lax.broadcasted_iota(jnp.int32, (tm, LANE), 1)
    flag_col = jnp.where(lane == HEAD_DIM, MASK_BF16, 0.0).astype(BF16)

    @pl.when(t == 0)
    def _pad_rows():
        for g in range(NSA_KV):
            ksel_ref[g, :, 0:LANE] = flag_col
            ksel_ref[g, :, LANE:2 * LANE] = jnp.zeros((tm, LANE), BF16)
            kwin_ref[g] = flag_col
            vsel_ref[g] = jnp.zeros((tm, 2 * LANE), BF16)
            vwin_ref[g] = jnp.zeros((tm, 2 * LANE), BF16)

    def chain(r0, n):
        rows = slice(r0, r0 + n)
        h = _rms(x_ref[rows, :], pre_ref[...]).astype(BF16)
        yield
        u = jax.nn.gelu(_dot(h, w_ref[:, 0:GM_WIDTH]))
        v = jax.nn.gelu(_dot(h, w_ref[:, GM_WIDTH:2 * GM_WIDTH]))
        yield
        r_i = lax.broadcasted_iota(jnp.int32, (CHUNK, CHUNK), 0)
        c_i = lax.broadcasted_iota(jnp.int32, (CHUNK, CHUNK), 1)
        for g in range(GM_GROUPS):
            gs = slice(g * GM_CH, (g + 1) * GM_CH)
            vg = v[:, gs]
            mu = jnp.mean(vg, axis=-1, keepdims=True)
            var = jnp.mean(jnp.square(vg - mu), axis=-1, keepdims=True)
            vn = ((vg - mu) * lax.rsqrt(var + EPS) * lng_ref[:, gs] + lnb_ref[:, gs]).astype(BF16)
            w = jnp.where(c_i <= r_i, ws_ref[g], 0.0).astype(BF16)
            for c in range(n // CHUNK):
                rs = slice(c * CHUNK, (c + 1) * CHUNK)
                s = _dot(w, vn[rs]) + bs_ref[:, g:g + 1]
                ya_ref[r0 + c * CHUNK:r0 + (c + 1) * CHUNK, gs] = u[rs, gs] * s
        yield
        ya_ref[rows, :] = _rms(ya_ref[rows, :], ga_ref[...])
        lane = lax.broadcasted_iota(jnp.int32, (n, LANE), 1)
        lane_lo = lane < HEAD_DIM

        def halves(z):
            return z, pltpu.roll(z, HEAD_DIM, 1)

        one_col = jnp.where(lane == HEAD_DIM, 1.0, 0.0)
        zq = _dot(h, w_ref[:, _O_Q:_O_C])
        yield
        for a in range(NSA_HEADS // 2):
            for hd, qh in zip((2 * a, 2 * a + 1), halves(zq[:, a * LANE:(a + 1) * LANE])):
                q_ref[hd, rows, :] = jnp.where(lane_lo, qh, one_col).astype(BF16)
        zc = _dot(h, w_ref[:, _O_C:_O_KV])
        yield
        nr = n // CMP_STRIDE
        urows = slice(r0 // CMP_STRIDE, r0 // CMP_STRIDE + nr)
        lo_u = lax.broadcasted_iota(jnp.int32, (nr, LANE), 1) < HEAD_DIM
        for kv in range(2):
            cmps_ref[kv, rows, :] = zc[:, kv * KV_W:(kv + 1) * KV_W]
            for a in range(CMP_STRIDE // 2):
                za = cmps_ref[kv, pl.ds(r0 + 2 * a, nr, stride=CMP_STRIDE), :]
                zb = cmps_ref[kv, pl.ds(r0 + 2 * a + 1, nr, stride=CMP_STRIDE), :]
                cmp_ref[kv, urows, a * LANE:(a + 1) * LANE] = jnp.where(
                    lo_u, za, pltpu.roll(zb, HEAD_DIM, 1))
                cmp_ref[kv, urows, (CMP_STRIDE // 2 + a) * LANE:(CMP_STRIDE // 2 + a + 1) * LANE] = (
                    jnp.where(lo_u, pltpu.roll(za, HEAD_DIM, 1), zb))
        zkv = _dot(h, w_ref[:, _O_KV:_O_GATE])
        yield
        k_sel, v_sel, k_win, v_win = (halves(zkv[:, j * LANE:(j + 1) * LANE]) for j in range(4))
        pos_blk = ((t - 1) * tm + r0 + lax.broadcasted_iota(jnp.int32, (n, LANE), 0)) // SEL_BLOCK
        blk_onehot = jnp.where(lane == pos_blk, MASK_BF16, 0.0).astype(BF16)

        def v_aug(v, g):
            return jnp.concatenate([jnp.where(lane_lo, v[g], 1.0), jnp.where(lane_lo, 1.0, v[1 - g])],
                                   axis=1).astype(BF16)

        for g in range(NSA_KV):
            ksel_ref[g, rows, 0:LANE] = jnp.where(lane_lo, k_sel[g], 0.0).astype(BF16)
            ksel_ref[g, rows, LANE:2 * LANE] = blk_onehot
            vsel_ref[g, rows, :] = v_aug(v_sel, g)
            kwin_ref[g, rows, :] = jnp.where(lane_lo, k_win[g], 0.0).astype(BF16)
            vwin_ref[g, rows, :] = v_aug(v_win, g)
        sg = jax.nn.sigmoid(_dot(h, wgt_ref[...]))
        hi = sg.astype(BF16)
        gt_ref[rows, 0:LANE] = hi
        gt_ref[rows, LANE:2 * LANE] = (sg - hi.astype(F32)).astype(BF16)

    @pl.when(t > 0)
    def _tile():
        _interleave([chain(k * tm // 2, tm // 2) for k in range(2)])


_O_Q = 2 * GM_WIDTH
_O_C = _O_Q + NSA_WIDTH
_O_KV = _O_C + 2 * KV_W
_O_GATE = _O_KV + 4 * KV_W


def _mixproj(x2d, batch, seq, mix_pre, w_in, ln_g, ln_b, ws, bs, gain_a):
    tm = TOKEN_TILE
    nt = seq // tm
    assert seq % tm == 0 and KV_PAD == tm
    scale = HEAD_DIM ** -0.5 * LOG2E
    w_main = jnp.concatenate([w_in[:, :_O_Q], w_in[:, _O_Q:_O_C] * scale, w_in[:, _O_C:_O_GATE]],
                             axis=1).astype(BF16)
    n_gate = 3 * NSA_HEADS
    w_g = w_in[:, _O_GATE:].reshape(D_MODEL, NSA_KV, NSA_REP, 3).transpose(0, 1, 3, 2)
    w_g = jnp.pad(w_g.reshape(D_MODEL, n_gate), ((0, 0), (0, LANE - n_gate))).astype(BF16)

    def tile(b, t):
        return b * nt + jnp.maximum(t - 1, 0)

    n = batch * seq
    sp = seq + KV_PAD
    out_shape = [
        jax.ShapeDtypeStruct((n, GM_WIDTH), F32),
        jax.ShapeDtypeStruct((batch, NSA_HEADS, seq, LANE), BF16),
        jax.ShapeDtypeStruct((2, n // CMP_STRIDE, CMP_STRIDE * KV_W), F32),
        jax.ShapeDtypeStruct((batch, NSA_KV, sp, 2 * LANE), BF16),
        jax.ShapeDtypeStruct((batch, NSA_KV, sp, 2 * LANE), BF16),
        jax.ShapeDtypeStruct((batch, NSA_KV, sp, LANE), BF16),
        jax.ShapeDtypeStruct((batch, NSA_KV, sp, 2 * LANE), BF16),
        jax.ShapeDtypeStruct((n, 2 * LANE), BF16),
    ]
    padded = lambda w: pl.BlockSpec((None, NSA_KV, tm, w), lambda b, t: (b, 0, t, 0))
    out_specs = [
        pl.BlockSpec((tm, GM_WIDTH), lambda b, t: (tile(b, t), 0)),
        pl.BlockSpec((None, NSA_HEADS, tm, LANE), lambda b, t: (b, 0, jnp.maximum(t - 1, 0), 0)),
        pl.BlockSpec((2, tm // CMP_STRIDE, CMP_STRIDE * KV_W), lambda b, t: (0, tile(b, t), 0)),
        padded(2 * LANE), padded(2 * LANE), padded(LANE), padded(2 * LANE),
        pl.BlockSpec((tm, 2 * LANE), lambda b, t: (tile(b, t), 0)),
    ]
    in_specs = [
        pl.BlockSpec((tm, D_MODEL), lambda b, t: (tile(b, t), 0)),
        _const_spec((1, D_MODEL)),
        _const_spec(w_main.shape), _const_spec(w_g.shape),
        _const_spec((1, GM_WIDTH)), _const_spec((1, GM_WIDTH)),
        _const_spec((GM_GROUPS, CHUNK, CHUNK)), _const_spec((CHUNK, GM_GROUPS)),
        _const_spec((1, GM_WIDTH)),
    ]
    return pl.pallas_call(
        _mixproj_kernel,
        grid=(batch, nt + 1),
        in_specs=in_specs,
        out_specs=out_specs,
        out_shape=out_shape,
        scratch_shapes=[pltpu.VMEM((2, tm, KV_W), F32)],
        compiler_params=_params(("arbitrary", "arbitrary")),
        name="mixproj",
    )(x2d, mix_pre.reshape(1, -1), w_main, w_g, ln_g.reshape(1, -1), ln_b.reshape(1, -1),
      ws, bs.T, gain_a.reshape(1, -1))


def _compress_kernel(raw_ref, pe_ref, w1_ref, b1_ref, w2_ref, o_ref):
    is_v = pl.program_id(0) == 1
    nu = raw_ref.shape[0]
    half = CMP_STRIDE * HEAD_DIM
    outs = []
    for g in range(NSA_KV):
        u = raw_ref[:, g * half:(g + 1) * half]
        a = _dot((u + pe_ref[0]).astype(BF16), w1_ref[0:half, :])
        b = _dot((u + pe_ref[1]).astype(BF16), w1_ref[half:2 * half, :])
        hid = jax.nn.gelu(a + pltpu.roll(b, nu - 1, 0) + b1_ref[...])
        outs.append(_dot(hid.astype(BF16), w2_ref[...]))
    lane = lax.broadcasted_iota(jnp.int32, (nu, LANE), 1)
    ci = lax.broadcasted_iota(jnp.int32, (nu, LANE), 0)
    u_c = CMP_BLOCK // CMP_STRIDE
    u_s = SEL_BLOCK // CMP_STRIDE
    overlap = jnp.maximum(jnp.minimum(ci + u_c, u_s * (lane + 1)) - jnp.maximum(ci, u_s * lane), 0)
    extra = jnp.where(is_v, overlap.astype(F32), 0.0)
    ones_hi = jnp.where(jnp.logical_and(is_v, lane >= HEAD_DIM), 1.0, 0.0)
    lane_p = lax.broadcasted_iota(jnp.int32, (CMP_PAD, LANE), 1)
    pad_lo = jnp.where(jnp.logical_and(jnp.logical_not(is_v), lane_p == HEAD_DIM), MASK_BF16, 0.0)
    for g in range(NSA_KV):
        o_ref[g, 0:CMP_PAD, 0:LANE] = pad_lo
        o_ref[g, 0:CMP_PAD, LANE:2 * LANE] = jnp.zeros((CMP_PAD, LANE), F32)
        o_ref[g, CMP_PAD:CMP_PAD + nu, 0:LANE] = outs[g] + ones_hi
        o_ref[g, CMP_PAD:CMP_PAD + nu, LANE:2 * LANE] = extra


def _compress(cmp_raw, batch, seq, pe, w1, b1, w2):
    nu = seq // CMP_STRIDE
    half = CMP_STRIDE * HEAD_DIM
    kd = NSA_KV * half
    raw = cmp_raw.reshape(2, batch, nu, kd)
    w2p = jnp.pad(w2, ((0, 0), (0, 0), (0, LANE - HEAD_DIM)))
    rows = CMP_PAD + nu
    return pl.pallas_call(
        _compress_kernel,
        grid=(2, batch),
        in_specs=[
            pl.BlockSpec((None, None, nu, kd), lambda k, b: (k, b, 0, 0)),
            pl.BlockSpec((None, 2, 1, half), lambda k, b: (k, 0, 0, 0)),
            pl.BlockSpec((None, 2 * half, CMP_HIDDEN), lambda k, b: (k, 0, 0)),
            pl.BlockSpec((None, 1, CMP_HIDDEN), lambda k, b: (k, 0, 0)),
            pl.BlockSpec((None, CMP_HIDDEN, LANE), lambda k, b: (k, 0, 0)),
        ],
        out_specs=pl.BlockSpec((None, None, NSA_KV, rows, 2 * LANE), lambda k, b: (k, b, 0, 0, 0)),
        out_shape=jax.ShapeDtypeStruct((2, batch, NSA_KV, rows, 2 * LANE), F32),
        compiler_params=_params(("arbitrary", "arbitrary")),
        name="compress",
    )(raw, pe.reshape(2, 2, 1, half), w1.astype(BF16), b1.reshape(2, 1, CMP_HIDDEN), w2p.astype(BF16))


def _t5_bucket(dist):
    n = jnp.maximum(dist, 0)
    max_exact = N_BUCKETS // 2
    nf = jnp.maximum(n, max_exact).astype(F32)
    large = max_exact + jnp.floor(jnp.log(nf * (1.0 / max_exact)) / math.log(MAX_DISTANCE / max_exact)
                                  * (N_BUCKETS - max_exact)).astype(jnp.int32)
    large = jnp.minimum(large, N_BUCKETS - 1)
    return jnp.where(n < max_exact, n, large)


def _tables_kernel(rb_ref, tb_ref, tc_ref, wm_ref):
    def fill(out_ref, dist):
        bk = _t5_bucket(dist)
        for h in range(NSA_HEADS):
            far = rb_ref[N_BUCKETS - 1, h]
            val = jnp.zeros(dist.shape, F32)
            for b in range(N_BUCKETS - 1):
                val = jnp.where(bk == b, (rb_ref[b, h] - far) * LOG2E, val)
            out_ref[h] = jnp.where(dist >= 0, val, MASK_F32)

    tq = lax.broadcasted_iota(jnp.int32, (Q_BLOCK, 2 * Q_BLOCK), 0)
    j = lax.broadcasted_iota(jnp.int32, (Q_BLOCK, 2 * Q_BLOCK), 1)
    fill(tb_ref, tq + Q_BLOCK - j)
    tq = lax.broadcasted_iota(jnp.int32, (Q_BLOCK, LANE), 0)
    c = lax.broadcasted_iota(jnp.int32, (Q_BLOCK, LANE), 1)
    cmp_first = Q_BLOCK // CMP_STRIDE - LANE
    fill(tc_ref, tq - CMP_STRIDE * (cmp_first + c) - (CMP_BLOCK - 1))
    wm_ref[...] = jnp.where(c > tq, 0.0, MASK_F32)


def _bias_tables(rel_bias):
    return pl.pallas_call(
        _tables_kernel,
        in_specs=[pl.BlockSpec(memory_space=pltpu.SMEM)],
        out_shape=[jax.ShapeDtypeStruct((NSA_HEADS, Q_BLOCK, 2 * Q_BLOCK), F32),
                   jax.ShapeDtypeStruct((NSA_HEADS, Q_BLOCK, LANE), F32),
                   jax.ShapeDtypeStruct((Q_BLOCK, LANE), F32)],
        name="bias_tables",
    )(rel_bias)


_N_GROUP_SCRATCH = 8


def _oddeven_merge_sort(n):
    def merge(lo, hi, r):
        step = 2 * r
        if step < hi - lo:
            yield from merge(lo, hi, step)
            yield from merge(lo + r, hi, step)
            yield from ((k, k + r) for k in range(lo + r, hi - r, step))
        else:
            yield (lo, lo + r)

    def sort(lo, hi):
        if hi > lo:
            mid = lo + (hi - lo) // 2
            yield from sort(lo, mid)
            yield from sort(mid + 1, hi)
            yield from merge(lo, hi, 1)

    return tuple(sort(0, n - 1))


def _bitonic_merge(n):
    pairs, k = [], n // 2
    while k >= 1:
        pairs += [(a, a + k) for a in range(n) if not a & k]
        k //= 2
    return tuple(pairs)


_SORT16 = _oddeven_merge_sort(N_SELECT)
_BITONIC16 = _bitonic_merge(N_SELECT)


def _nsa_kernel(q_ref, gt_ref, kc_ref, vc_ref, ks_ref, vs_ref, kw_ref, vw_ref,
                 tb_ref, tc_ref, wm_ref, gx_ref, o_ref, *scratch, n_sel, select_all):
    i = pl.program_id(1)
    rows = NSA_REP * Q_BLOCK
    per_group = [scratch[g * _N_GROUP_SCRATCH:(g + 1) * _N_GROUP_SCRATCH] for g in range(NSA_KV)]
    qsel_ref = [s[0] for s in per_group]
    s_ref = [(s[1], s[2]) for s in per_group]
    m_ref = [(s[3], s[4]) for s in per_group]
    al_ref = [(s[5], s[6]) for s in per_group]
    acc_ref = [s[7] for s in per_group]

    zero_tb = jnp.zeros((rows, KV_TILE - 2 * Q_BLOCK), F32)
    c0 = pl.multiple_of(8 * i + 8 + CMP_PAD - CMP_SPAN, 8)
    t0 = pl.multiple_of(Q_BLOCK * (i + 1) + KV_PAD - KV_TILE, Q_BLOCK)
    w1 = pl.multiple_of(t0 - Q_BLOCK, Q_BLOCK)
    n_tiles = (i + 1 + KV_TILE // Q_BLOCK - 1) // (KV_TILE // Q_BLOCK)
    lane = lax.broadcasted_iota(jnp.int32, (Q_BLOCK, LANE), 1)

    def bcast(col):
        return jnp.broadcast_to(col, (rows, LANE))

    def lanes(x, n):
        return jnp.concatenate([x] * n, axis=1)

    def head_out(acc, r):
        rs = slice(r * Q_BLOCK, (r + 1) * Q_BLOCK)
        a, b = acc[rs, 0:LANE], acc[rs, LANE:2 * LANE]
        return a * (1.0 / b) if r % 2 == 0 else b * (1.0 / a)

    def heads_of(g):
        return slice(g * NSA_REP, (g + 1) * NSA_REP)

    def q_of(g):
        return q_ref[heads_of(g)].reshape(rows, LANE)

    def tile0_bias(g):
        return jnp.concatenate([zero_tb, tb_ref[heads_of(g)].reshape(rows, 2 * Q_BLOCK)], axis=1)

    fronts = [None] * NSA_KV

    def front(g):
        hs = heads_of(g)
        qa = q_of(g)
        s = _dot_nt(qa, kc_ref[g, pl.ds(c0, CMP_SPAN), :].astype(BF16))
        s = s + jnp.concatenate([jnp.zeros((rows, CMP_SPAN - LANE), F32),
                                 tc_ref[hs].reshape(rows, LANE)], axis=1)
        yield
        m = jnp.maximum(jnp.max(s, axis=1, keepdims=True), MAX_CLAMP)
        p = _softmax_exp(s - m).astype(BF16)
        yield
        acc = _dot(p, vc_ref[g, pl.ds(c0, CMP_SPAN), :].astype(BF16))
        num_den = acc[:, 0:LANE]
        lane_r = lax.broadcasted_iota(jnp.int32, (rows, LANE), 1)
        den = jnp.where(lane_r < HEAD_DIM, pltpu.roll(num_den, HEAD_DIM, 1), num_den)
        inv = 1.0 / jnp.maximum(den, 1e-30)
        oc = num_den * inv
        impu = acc[:, LANE:2 * LANE] * inv
        imp = impu[0:Q_BLOCK]
        for r in range(1, NSA_REP):
            imp = imp + impu[r * Q_BLOCK:(r + 1) * Q_BLOCK]
        yield
        s0 = _dot_nt(qa, kw_ref[g, pl.ds(t0, KV_TILE), :]) + tile0_bias(g)
        s1 = _dot_nt(qa, kw_ref[g, pl.ds(w1, Q_BLOCK), :]) + jnp.tile(wm_ref[...], (NSA_REP, 1))
        yield
        m = jnp.maximum(jnp.max(s0, axis=1, keepdims=True), jnp.max(s1, axis=1, keepdims=True))
        p0 = _softmax_exp(s0 - m).astype(BF16)
        p1 = _softmax_exp(s1 - m).astype(BF16)
        yield
        accw = (_dot(p0, vw_ref[g, pl.ds(t0, KV_TILE), :]) + _dot(p1, vw_ref[g, pl.ds(w1, Q_BLOCK), :]))
        gx = _dot(gt_ref[...], gx_ref[g])

        def gate(branch, r):
            c = branch * NSA_REP + r
            return gx[:, c * LANE:(c + 1) * LANE]

        partial = []
        for r in range(NSA_REP):
            rs = slice(r * Q_BLOCK, (r + 1) * Q_BLOCK)
            oc_r = oc[rs] if r % 2 == 0 else pltpu.roll(oc[rs], HEAD_DIM, 1)
            partial.append(gate(0, r) * oc_r + gate(2, r) * head_out(accw, r))
        fronts[g] = (partial, [gate(1, r) for r in range(NSA_REP)])
        yield
        notsel_q = not_selected(imp).astype(BF16)
        qsel_ref[g][:, 0:LANE] = qa
        for r in range(NSA_REP):
            qsel_ref[g][r * Q_BLOCK:(r + 1) * Q_BLOCK, LANE:2 * LANE] = notsel_q
        yield
        s = _dot_nt(qsel_ref[g][...], ks_ref[g, pl.ds(t0, KV_TILE), :]) + tile0_bias(g)
        yield
        s_ref[g][0][...] = s
        m_ref[g][0][...] = bcast(jnp.max(s, axis=1, keepdims=True))
        al_ref[g][0][...] = jnp.ones((rows, LANE), F32)
        acc_ref[g][...] = jnp.zeros((rows, 2 * LANE), F32)

    def not_selected(imp):
        jb = lax.broadcasted_iota(jnp.int32, (LANE, Q_BLOCK), 0)
        t = i * Q_BLOCK + lax.broadcasted_iota(jnp.int32, (LANE, Q_BLOCK), 1)
        cur = t // SEL_BLOCK
        forced = (jb == 0) | (jb == cur) | (jb == cur - 1)
        score = jnp.where(forced, FORCE_SCORE, jnp.where(jb * SEL_BLOCK <= t, imp.T, -1.0))
        if select_all:
            return jnp.zeros((Q_BLOCK, LANE), F32)
        sub = 8
        x = [score[sub * v:sub * (v + 1), :] for v in range(LANE // sub)]
        y = list(x)
        for a, b in _SORT16:
            y[a], y[b] = jnp.maximum(y[a], y[b]), jnp.minimum(y[a], y[b])
        for shift in (4, 2, 1):
            z = [jnp.maximum(y[k], pltpu.roll(y[n_sel - 1 - k], shift, 0)) for k in range(n_sel)]
            if shift > 1:
                for a, b in _BITONIC16:
                    z[a], z[b] = jnp.maximum(z[a], z[b]), jnp.minimum(z[a], z[b])
            y = z
        while len(y) > 1:
            y = [jnp.minimum(y[2 * k], y[2 * k + 1]) for k in range(len(y) // 2)]
        tau = y[0]
        gt = jnp.concatenate([jnp.where(xv > tau, 1.0, 0.0) for xv in x], axis=0)
        eq = jnp.concatenate([jnp.where(xv == tau, 1.0, 0.0) for xv in x], axis=0)
        kb = lax.broadcasted_iota(jnp.int32, (LANE, LANE), 1)
        n_above = _dot(jnp.ones((LANE, LANE), BF16), gt.astype(BF16))
        n_tied_before = _dot(jnp.where(kb < jb, 1.0, 0.0).astype(BF16), eq.astype(BF16))
        keep = (gt > 0.0) | ((eq > 0.0) & (n_above + n_tied_before < n_sel))
        return jnp.where(keep, 0.0, 1.0).T


    def qk_stage(g, k, prev, slot):
        st = pl.multiple_of(t0 - KV_TILE * k, Q_BLOCK)
        s_n = _dot_nt(qsel_ref[g][...], ks_ref[g, pl.ds(st, KV_TILE), :])
        s_ref[g][slot][...] = s_n
        m_c = m_ref[g][prev][...]
        m_n = jnp.maximum(m_c, bcast(jnp.max(s_n, axis=1, keepdims=True)))
        m_ref[g][slot][...] = m_n
        al_ref[g][slot][...] = _softmax_exp(m_c - m_n)

    def pv_stage(g, k, slot):
        st = pl.multiple_of(t0 - KV_TILE * k, Q_BLOCK)
        p = _softmax_exp(s_ref[g][slot][...] - lanes(m_ref[g][slot][...], KV_TILE // LANE))
        acc_ref[g][...] = (lanes(al_ref[g][slot][...], 2) * acc_ref[g][...]
                           + _dot(p.astype(BF16), vs_ref[g, pl.ds(st, KV_TILE), :]))

    groups = range(NSA_KV)
    _interleave([front(g) for g in groups])

    def tile_pair(j):
        for g in groups:
            qk_stage(g, 2 * j + 1, 0, 1)
        for g in groups:
            pv_stage(g, 2 * j, 0)
        for g in groups:
            qk_stage(g, 2 * j + 2, 1, 0)
        for g in groups:
            pv_stage(g, 2 * j + 1, 1)

    def two_pairs(j, carry):
        tile_pair(2 * j)
        tile_pair(2 * j + 1)
        return carry

    def one_pair(j, carry):
        tile_pair(j)
        return carry

    n_pairs = (n_tiles - 1) // 2
    lax.fori_loop(0, n_pairs // 2, two_pairs, 0)
    lax.fori_loop(2 * (n_pairs // 2), n_pairs, one_pair, 0)
    odd_left = n_tiles - 1 - 2 * n_pairs

    @pl.when(odd_left == 1)
    def _two_left():
        for g in groups:
            pv_stage(g, n_tiles - 2, 0)
        for g in groups:
            qk_stage(g, n_tiles - 1, 0, 0)

    for g in groups:
        pv_stage(g, n_tiles - 1, 0)
        partial, sel_gate = fronts[g]
        acc = acc_ref[g][...]
        heads = [partial[r] + sel_gate[r] * head_out(acc, r) for r in range(NSA_REP)]
        for pr in range(NSA_REP // 2):
            col = (g * NSA_REP // 2 + pr) * LANE
            o_ref[:, col:col + LANE] = jnp.where(lane < HEAD_DIM, heads[2 * pr], heads[2 * pr + 1])


def _nsa(q_aug, gates, cmp_aug, ksel, vsel, kwin, vwin, tb, tc, wm, batch, seq):
    nq = seq // Q_BLOCK
    sp = seq + KV_PAD
    crow = cmp_aug.shape[3]
    n_sel = min(N_SELECT, seq // SEL_BLOCK)
    assert seq // SEL_BLOCK <= LANE and KV_PAD >= KV_TILE and KV_PAD >= WINDOW
    assert seq // CMP_STRIDE <= CMP_SPAN <= CMP_PAD and crow == CMP_PAD + seq // CMP_STRIDE
    assert seq // SEL_BLOCK <= N_SELECT or n_sel == LANE // 8
    n_gate = 3 * NSA_REP
    col = jnp.arange(n_gate * LANE) // LANE
    row = jnp.arange(2 * LANE) % LANE
    gate_spread = jnp.stack([(row[:, None] == g * n_gate + col[None, :]) for g in range(NSA_KV)]).astype(BF16)
    once = pl.Buffered(1)
    kv_spec = lambda w: pl.BlockSpec((None, NSA_KV, sp, w), lambda b, i: (b, 0, 0, 0), pipeline_mode=once)
    cmp_spec = lambda k, w: pl.BlockSpec((None, None, NSA_KV, crow, w), lambda b, i: (k, b, 0, 0, 0),
                                         pipeline_mode=once)
    in_specs = [
        pl.BlockSpec((None, NSA_HEADS, Q_BLOCK, LANE), lambda b, i: (b, 0, i, 0)),
        pl.BlockSpec((Q_BLOCK, 2 * LANE), lambda b, i: (b * nq + i, 0)),
        cmp_spec(0, LANE), cmp_spec(1, 2 * LANE),
        kv_spec(2 * LANE), kv_spec(2 * LANE), kv_spec(LANE), kv_spec(2 * LANE),
        _const_spec((NSA_HEADS, Q_BLOCK, 2 * Q_BLOCK)),
        _const_spec((NSA_HEADS, Q_BLOCK, LANE)),
        _const_spec((Q_BLOCK, LANE)),
        _const_spec((NSA_KV, 2 * LANE, n_gate * LANE)),
    ]
    rows = NSA_REP * Q_BLOCK
    group_scratch = [pltpu.VMEM((rows, 2 * LANE), BF16),
                     pltpu.VMEM((rows, KV_TILE), F32),
                     pltpu.VMEM((rows, KV_TILE), F32),
                     pltpu.VMEM((rows, LANE), F32),
                     pltpu.VMEM((rows, LANE), F32),
                     pltpu.VMEM((rows, LANE), F32),
                     pltpu.VMEM((rows, LANE), F32),
                     pltpu.VMEM((rows, 2 * LANE), F32)]
    assert len(group_scratch) == _N_GROUP_SCRATCH
    return pl.pallas_call(
        functools.partial(_nsa_kernel, n_sel=n_sel, select_all=seq // SEL_BLOCK <= N_SELECT),
        grid=(batch, nq),
        in_specs=in_specs,
        out_specs=pl.BlockSpec((Q_BLOCK, NSA_WIDTH), lambda b, i: (b * nq + i, 0)),
        out_shape=jax.ShapeDtypeStruct((batch * seq, NSA_WIDTH), F32),
        scratch_shapes=group_scratch * NSA_KV,
        compiler_params=_params(("arbitrary", "arbitrary")),
        name="nsa",
    )(q_aug, gates, cmp_aug, cmp_aug, ksel, vsel, kwin, vwin, tb, tc, wm, gate_spread)


def _memkv_kernel(mem_ref, g_ref, wk_ref, wv_ref, k_ref, v_ref):
    h = _rms(mem_ref[...], g_ref[...]).astype(BF16)
    k_ref[...] = _dot(h, wk_ref[...]).astype(BF16)
    v_ref[...] = _dot(h, wv_ref[...]).astype(BF16)


def _mem_kv(mem, g_mem, wkv):
    batch, m, _ = mem.shape
    blk = pl.BlockSpec((None, m, D_MODEL), lambda b: (b, 0, 0))
    return pl.pallas_call(
        _memkv_kernel,
        grid=(batch,),
        in_specs=[blk, _const_spec((1, D_MODEL)), _const_spec((D_MODEL, D_MODEL)),
                  _const_spec((D_MODEL, D_MODEL))],
        out_specs=[blk, blk],
        out_shape=[jax.ShapeDtypeStruct((batch, m, D_MODEL), BF16)] * 2,
        compiler_params=_params(("arbitrary",)),
        name="mem_kv",
    )(mem, g_mem.reshape(1, -1), wkv[:, :D_MODEL].astype(BF16), wkv[:, D_MODEL:].astype(BF16))


def _mixout_kernel(x_ref, ya_ref, yb_ref, gb_ref, wo_ref, post_ref, xpre_ref, wq_ref,
                   km_ref, vm_ref, wxo_ref, xpost_ref, o_ref):
    tm = x_ref.shape[0]

    def chain(rows):
        yb = _rms(yb_ref[rows, :], gb_ref[...]).astype(BF16)
        ya = ya_ref[rows, :].astype(BF16)
        yield
        y = _dot(ya, wo_ref[0:GM_WIDTH, :]) + _dot(yb, wo_ref[GM_WIDTH:, :])
        yield
        x1 = x_ref[rows, :] + _rms(y, post_ref[...])
        h = _rms(x1, xpre_ref[...]).astype(BF16)
        yield
        q = _dot(h, wq_ref[...]).astype(BF16)
        yield
        outs = []
        for hd in range(XA_HEADS):
            hs = slice(hd * XA_HEAD_DIM, (hd + 1) * XA_HEAD_DIM)
            s = _dot_nt(q[:, hs], km_ref[:, hs])
            yield
            e = _softmax_exp(s - jnp.max(s, axis=1, keepdims=True))
            den = jnp.sum(e, axis=1, keepdims=True)
            e = e.astype(BF16)
            yield
            outs.append((_dot(e, vm_ref[:, hs]) / den).astype(BF16))
        o = _dot(jnp.concatenate(outs, axis=1), wxo_ref[...])
        yield
        o_ref[rows, :] = x1 + _rms(o, xpost_ref[...])

    _interleave([chain(slice(k * tm // 2, (k + 1) * tm // 2)) for k in range(2)])


def _mix_out(x2d, ya, yb, batch, seq, gain_b, w_out, mix_post, xa_pre, wq, kmem, vmem, wo, xa_post):
    tm = 2 * TOKEN_TILE if seq % (2 * TOKEN_TILE) == 0 else TOKEN_TILE
    nt = seq // tm
    row = lambda w: pl.BlockSpec((tm, w), lambda b, t: (b * nt + t, 0))
    mem_spec = pl.BlockSpec((None, kmem.shape[1], D_MODEL), lambda b, t: (b, 0, 0))
    sq = _const_spec((D_MODEL, D_MODEL))
    vec = _const_spec((1, D_MODEL))
    return pl.pallas_call(
        _mixout_kernel,
        grid=(batch, nt),
        in_specs=[row(D_MODEL), row(GM_WIDTH), row(NSA_WIDTH), _const_spec((1, NSA_WIDTH)), sq, vec,
                  vec, sq, mem_spec, mem_spec, sq, vec],
        out_specs=row(D_MODEL),
        out_shape=jax.ShapeDtypeStruct(x2d.shape, F32),
        compiler_params=_params(("arbitrary", "arbitrary")),
        name="mix_out",
    )(x2d, ya, yb, gain_b.reshape(1, -1), w_out.astype(BF16), mix_post.reshape(1, -1),
      xa_pre.reshape(1, -1), (wq * (XA_HEAD_DIM ** -0.5 * LOG2E)).astype(BF16), kmem, vmem,
      wo.astype(BF16), xa_post.reshape(1, -1))


def kernel(x, mem, ffn1_pre, ffn1_post, ffn1_wg, ffn1_wu, ffn1_wd, mix_pre, mix_post, w_in, gm_ln_g, gm_ln_b, gm_ws, gm_bs, ck_pe, ck_w1, ck_b1, ck_w2, cv_pe, cv_w1, cv_b1, cv_w2, rel_bias, out_gain_a, out_gain_b, w_out, xa_pre, xa_post, mem_norm, xa_wq, xa_wkv, xa_wo, ffn2_pre, ffn2_post, ffn2_wg, ffn2_wu, ffn2_wd):
    batch, seq, _ = x.shape
    depth = ffn1_pre.shape[0]
    x2d = x.reshape(batch * seq, D_MODEL)
    tb, tc, wm = _bias_tables(rel_bias)
    for l in range(depth):
        x2d = _ffn(x2d, ffn1_pre[l], ffn1_wg[l], ffn1_wu[l], ffn1_wd[l], ffn1_post[l])
        ya, q_aug, cmp_raw, ksel, vsel, kwin, vwin, gates = _mixproj(
            x2d, batch, seq, mix_pre[l], w_in[l], gm_ln_g[l], gm_ln_b[l], gm_ws[l], gm_bs[l],
            out_gain_a[l])
        cmp_aug = _compress(cmp_raw, batch, seq,
                            jnp.stack([ck_pe[l], cv_pe[l]]), jnp.stack([ck_w1[l], cv_w1[l]]),
                            jnp.stack([ck_b1[l], cv_b1[l]]), jnp.stack([ck_w2[l], cv_w2[l]]))
        yb = _nsa(q_aug, gates, cmp_aug, ksel, vsel, kwin, vwin, tb, tc, wm, batch, seq)
        kmem, vmem = _mem_kv(mem, mem_norm[l], xa_wkv[l])
        x2d = _mix_out(x2d, ya, yb, batch, seq, out_gain_b[l], w_out[l], mix_post[l], xa_pre[l],
                       xa_wq[l], kmem, vmem, xa_wo[l], xa_post[l])
        x2d = _ffn(x2d, ffn2_pre[l], ffn2_wg[l], ffn2_wu[l], ffn2_wd[l], ffn2_post[l])
    return x2d.reshape(batch, seq, D_MODEL)
```

```python
import functools
import math

import jax
import jax.numpy as jnp
from jax import lax
from jax.experimental import pallas as pl
from jax.experimental.pallas import tpu as pltpu

F32 = jnp.float32
BF16 = jnp.bfloat16

D_MODEL = 1024
MEM_LEN = 256
EPS = 1e-6
D_FF = 2816
GM_WIDTH = 512
GM_GROUPS = 4
GM_CH = GM_WIDTH // GM_GROUPS
CHUNK = 128
NSA_WIDTH = D_MODEL - GM_WIDTH
HEAD_DIM = 64
NSA_HEADS = NSA_WIDTH // HEAD_DIM
NSA_KV = 2
NSA_REP = NSA_HEADS // NSA_KV
KV_W = NSA_KV * HEAD_DIM
CMP_BLOCK = 32
CMP_STRIDE = 16
CMP_HIDDEN = 256
SEL_BLOCK = 64
N_SELECT = 16
WINDOW = 512
Q_BLOCK = 128
N_BUCKETS = 32
MAX_DISTANCE = 128
XA_HEADS = 4
XA_HEAD_DIM = D_MODEL // XA_HEADS
FORCE_SCORE = 1e4

LANE = 128
SUBLANE = 8
TOKEN_TILE = 512
FF_CHUNK = 256
KV_PAD = 512
KV_TILE = 512
CMP_PAD = 512
CMP_SPAN = 512
MASK_BF16 = -(2.0 ** 100)
MASK_F32 = -1e30
MAX_CLAMP = -(2.0 ** 99)
LOG2E = math.log2(math.e)
_softmax_exp = jnp.exp2
VMEM_LIMIT = 56 * 1024 * 1024

_NT = (((1,), (1,)), ((), ()))


def _rms(x, g):
    ms = jnp.mean(x * x, axis=-1, keepdims=True)
    return x * lax.rsqrt(ms + EPS) * g


def _dot(a, b):
    return jnp.dot(a, b, preferred_element_type=F32)


def _dot_nt(a, b):
    return lax.dot_general(a, b, _NT, preferred_element_type=F32)


def _params(sem, vmem=VMEM_LIMIT):
    return pltpu.CompilerParams(dimension_semantics=sem, vmem_limit_bytes=vmem)


def _const_spec(shape):
    nd = len(shape)
    return pl.BlockSpec(shape, lambda *_: (0,) * nd)


def _interleave(chains, skew=1):
    pending = list(chains)
    for _ in range(skew):
        next(pending[0])
    while pending:
        for gen in list(pending):
            if next(gen, pending) is pending:
                pending.remove(gen)


def _ffn_kernel(x_ref, pre_ref, wg_ref, wu_ref, wd_ref, post_ref, o_ref, acc_ref):
    tm = x_ref.shape[0]

    def chain(rows):
        h = _rms(x_ref[rows, :], pre_ref[...]).astype(BF16)
        yield
        for c in range(D_FF // FF_CHUNK):
            sl = slice(c * FF_CHUNK, (c + 1) * FF_CHUNK)
            g = _dot(h, wg_ref[:, sl])
            u = _dot(h, wu_ref[:, sl])
            a = (jax.nn.silu(g) * u).astype(BF16)
            d = _dot(a, wd_ref[sl, :])
            if c == 0:
                acc_ref[rows, :] = d
            else:
                acc_ref[rows, :] += d
            yield
        o_ref[rows, :] = x_ref[rows, :] + 0.5 * _rms(acc_ref[rows, :], post_ref[...])

    _interleave([chain(slice(k * tm // 2, (k + 1) * tm // 2)) for k in range(2)])


def _ffn(x2d, pre, wg, wu, wd, post):
    n = x2d.shape[0]
    tm = 2 * TOKEN_TILE if n % (2 * TOKEN_TILE) == 0 else TOKEN_TILE
    assert n % tm == 0
    row = pl.BlockSpec((tm, D_MODEL), lambda t: (t, 0))
    once = pl.Buffered(1)
    weight = lambda shape: pl.BlockSpec(shape, lambda t: (0, 0), pipeline_mode=once)
    return pl.pallas_call(
        _ffn_kernel,
        grid=(n // tm,),
        in_specs=[row, _const_spec((1, D_MODEL)), weight((D_MODEL, D_FF)),
                  weight((D_MODEL, D_FF)), weight((D_FF, D_MODEL)),
                  _const_spec((1, D_MODEL))],
        out_specs=row,
        out_shape=jax.ShapeDtypeStruct((n, D_MODEL), F32),
        scratch_shapes=[pltpu.VMEM((tm, D_MODEL), F32)],
        compiler_params=_params(("arbitrary",)),
        name="ffn",
    )(x2d, pre.reshape(1, -1), wg.astype(BF16), wu.astype(BF16), wd.astype(BF16),
      post.reshape(1, -1))


def _mixproj_kernel(x_ref, pre_ref, w_ref, wgt_ref, lng_ref, lnb_ref, ws_ref, bs_ref, ga_ref,
                    ya_ref, q_ref, cmp_ref, ksel_ref, vsel_ref, kwin_ref, vwin_ref, gt_ref, cmps_ref):
    t = pl.program_id(1)
    tm = x_ref.shape[0]
    lane = lax.broadcasted_iota(jnp.int32, (tm, LANE), 1)
    flag_col = jnp.where(lane == HEAD_DIM, MASK_BF16, 0.0).astype(BF16)

    @pl.when(t == 0)
    def _pad_rows():
        for g in range(NSA_KV):
            ksel_ref[g, :, 0:LANE] = flag_col
            ksel_ref[g, :, LANE:2 * LANE] = jnp.zeros((tm, LANE), BF16)
            kwin_ref[g] = flag_col
            vsel_ref[g] = jnp.zeros((tm, 2 * LANE), BF16)
            vwin_ref[g] = jnp.zeros((tm, 2 * LANE), BF16)

    def chain(r0, n):
        rows = slice(r0, r0 + n)
        h = _rms(x_ref[rows, :], pre_ref[...]).astype(BF16)
        yield
        u = jax.nn.gelu(_dot(h, w_ref[:, 0:GM_WIDTH]))
        v = jax.nn.gelu(_dot(h, w_ref[:, GM_WIDTH:2 * GM_WIDTH]))
        yield
        r_i = lax.broadcasted_iota(jnp.int32, (CHUNK, CHUNK), 0)
        c_i = lax.broadcasted_iota(jnp.int32, (CHUNK, CHUNK), 1)
        for g in range(GM_GROUPS):
            gs = slice(g * GM_CH, (g + 1) * GM_CH)
            vg = v[:, gs]
            mu = jnp.mean(vg, axis=-1, keepdims=True)
            var = jnp.mean(jnp.square(vg - mu), axis=-1, keepdims=True)
            vn = ((vg - mu) * lax.rsqrt(var + EPS) * lng_ref[:, gs] + lnb_ref[:, gs]).astype(BF16)
            w = jnp.where(c_i <= r_i, ws_ref[g], 0.0).astype(BF16)
            for c in range(n // CHUNK):
                rs = slice(c * CHUNK, (c + 1) * CHUNK)
                s = _dot(w, vn[rs]) + bs_ref[:, g:g + 1]
                ya_ref[r0 + c * CHUNK:r0 + (c + 1) * CHUNK, gs] = u[rs, gs] * s
        yield
        ya_ref[rows, :] = _rms(ya_ref[rows, :], ga_ref[...])
        lane = lax.broadcasted_iota(jnp.int32, (n, LANE), 1)
        lane_lo = lane < HEAD_DIM

        def halves(z):
            return z, pltpu.roll(z, HEAD_DIM, 1)

        one_col = jnp.where(lane == HEAD_DIM, 1.0, 0.0)
        zq = _dot(h, w_ref[:, _O_Q:_O_C])
        yield
        for a in range(NSA_HEADS // 2):
            for hd, qh in zip((2 * a, 2 * a + 1), halves(zq[:, a * LANE:(a + 1) * LANE])):
                q_ref[hd, rows, :] = jnp.where(lane_lo, qh, one_col).astype(BF16)
        zc = _dot(h, w_ref[:, _O_C:_O_KV])
        yield
        nr = n // CMP_STRIDE
        urows = slice(r0 // CMP_STRIDE, r0 // CMP_STRIDE + nr)
        lo_u = lax.broadcasted_iota(jnp.int32, (nr, LANE), 1) < HEAD_DIM
        for kv in range(2):
            cmps_ref[kv, rows, :] = zc[:, kv * KV_W:(kv + 1) * KV_W]
            for a in range(CMP_STRIDE // 2):
                za = cmps_ref[kv, pl.ds(r0 + 2 * a, nr, stride=CMP_STRIDE), :]
                zb = cmps_ref[kv, pl.ds(r0 + 2 * a + 1, nr, stride=CMP_STRIDE), :]
                cmp_ref[kv, urows, a * LANE:(a + 1) * LANE] = jnp.where(
                    lo_u, za, pltpu.roll(zb, HEAD_DIM, 1))
                cmp_ref[kv, urows, (CMP_STRIDE // 2 + a) * LANE:(CMP_STRIDE // 2 + a + 1) * LANE] = (
                    jnp.where(lo_u, pltpu.roll(za, HEAD_DIM, 1), zb))
        zkv = _dot(h, w_ref[:, _O_KV:_O_GATE])
        yield
        k_sel, v_sel, k_win, v_win = (halves(zkv[:, j * LANE:(j + 1) * LANE]) for j in range(4))
        pos_blk = ((t - 1) * tm + r0 + lax.broadcasted_iota(jnp.int32, (n, LANE), 0)) // SEL_BLOCK
        blk_onehot = jnp.where(lane == pos_blk, MASK_BF16, 0.0).astype(BF16)

        def v_aug(v, g):
            return jnp.concatenate([jnp.where(lane_lo, v[g], 1.0), jnp.where(lane_lo, 1.0, v[1 - g])],
                                   axis=1).astype(BF16)

        for g in range(NSA_KV):
            ksel_ref[g, rows, 0:LANE] = jnp.where(lane_lo, k_sel[g], 0.0).astype(BF16)
            ksel_ref[g, rows, LANE:2 * LANE] = blk_onehot
            vsel_ref[g, rows, :] = v_aug(v_sel, g)
            kwin_ref[g, rows, :] = jnp.where(lane_lo, k_win[g], 0.0).astype(BF16)
            vwin_ref[g, rows, :] = v_aug(v_win, g)
        sg = jax.nn.sigmoid(_dot(h, wgt_ref[...]))
        hi = sg.astype(BF16)
        gt_ref[rows, 0:LANE] = hi
        gt_ref[rows, LANE:2 * LANE] = (sg - hi.astype(F32)).astype(BF16)

    @pl.when(t > 0)
    def _tile():
        _interleave([chain(k * tm // 2, tm // 2) for k in range(2)])


_O_Q = 2 * GM_WIDTH
_O_C = _O_Q + NSA_WIDTH
_O_KV = _O_C + 2 * KV_W
_O_GATE = _O_KV + 4 * KV_W


def _mixproj(x2d, batch, seq, mix_pre, w_in, ln_g, ln_b, ws, bs, gain_a):
    tm = TOKEN_TILE
    nt = seq // tm
    assert seq % tm == 0 and KV_PAD == tm
    scale = HEAD_DIM ** -0.5 * LOG2E
    w_main = jnp.concatenate([w_in[:, :_O_Q], w_in[:, _O_Q:_O_C] * scale, w_in[:, _O_C:_O_GATE]],
                             axis=1).astype(BF16)
    n_gate = 3 * NSA_HEADS
    w_g = w_in[:, _O_GATE:].reshape(D_MODEL, NSA_KV, NSA_REP, 3).transpose(0, 1, 3, 2)
    w_g = jnp.pad(w_g.reshape(D_MODEL, n_gate), ((0, 0), (0, LANE - n_gate))).astype(BF16)

    def tile(b, t):
        return b * nt + jnp.maximum(t - 1, 0)

    n = batch * seq
    sp = seq + KV_PAD
    out_shape = [
        jax.ShapeDtypeStruct((n, GM_WIDTH), F32),
        jax.ShapeDtypeStruct((batch, NSA_HEADS, seq, LANE), BF16),
        jax.ShapeDtypeStruct((2, n // CMP_STRIDE, CMP_STRIDE * KV_W), F32),
        jax.ShapeDtypeStruct((batch, NSA_KV, sp, 2 * LANE), BF16),
        jax.ShapeDtypeStruct((batch, NSA_KV, sp, 2 * LANE), BF16),
        jax.ShapeDtypeStruct((batch, NSA_KV, sp, LANE), BF16),
        jax.ShapeDtypeStruct((batch, NSA_KV, sp, 2 * LANE), BF16),
        jax.ShapeDtypeStruct((n, 2 * LANE), BF16),
    ]
    padded = lambda w: pl.BlockSpec((None, NSA_KV, tm, w), lambda b, t: (b, 0, t, 0))
    out_specs = [
        pl.BlockSpec((tm, GM_WIDTH), lambda b, t: (tile(b, t), 0)),
        pl.BlockSpec((None, NSA_HEADS, tm, LANE), lambda b, t: (b, 0, jnp.maximum(t - 1, 0), 0)),
        pl.BlockSpec((2, tm // CMP_STRIDE, CMP_STRIDE * KV_W), lambda b, t: (0, tile(b, t), 0)),
        padded(2 * LANE), padded(2 * LANE), padded(LANE), padded(2 * LANE),
        pl.BlockSpec((tm, 2 * LANE), lambda b, t: (tile(b, t), 0)),
    ]
    in_specs = [
        pl.BlockSpec((tm, D_MODEL), lambda b, t: (tile(b, t), 0)),
        _const_spec((1, D_MODEL)),
        _const_spec(w_main.shape), _const_spec(w_g.shape),
        _const_spec((1, GM_WIDTH)), _const_spec((1, GM_WIDTH)),
        _const_spec((GM_GROUPS, CHUNK, CHUNK)), _const_spec((CHUNK, GM_GROUPS)),
        _const_spec((1, GM_WIDTH)),
    ]
    return pl.pallas_call(
        _mixproj_kernel,
        grid=(batch, nt + 1),
        in_specs=in_specs,
        out_specs=out_specs,
        out_shape=out_shape,
        scratch_shapes=[pltpu.VMEM((2, tm, KV_W), F32)],
        compiler_params=_params(("arbitrary", "arbitrary")),
        name="mixproj",
    )(x2d, mix_pre.reshape(1, -1), w_main, w_g, ln_g.reshape(1, -1), ln_b.reshape(1, -1),
      ws, bs.T, gain_a.reshape(1, -1))


def _compress_kernel(raw_ref, pe_ref, w1_ref, b1_ref, w2_ref, o_ref):
    is_v = pl.program_id(0) == 1
    nu = raw_ref.shape[0]
    half = CMP_STRIDE * HEAD_DIM
    outs = []
    for g in range(NSA_KV):
        u = raw_ref[:, g * half:(g + 1) * half]
        a = _dot((u + pe_ref[0]).astype(BF16), w1_ref[0:half, :])
        b = _dot((u + pe_ref[1]).astype(BF16), w1_ref[half:2 * half, :])
        hid = jax.nn.gelu(a + pltpu.roll(b, nu - 1, 0) + b1_ref[...])
        outs.append(_dot(hid.astype(BF16), w2_ref[...]))
    lane = lax.broadcasted_iota(jnp.int32, (nu, LANE), 1)
    ci = lax.broadcasted_iota(jnp.int32, (nu, LANE), 0)
    u_c = CMP_BLOCK // CMP_STRIDE
    u_s = SEL_BLOCK // CMP_STRIDE
    overlap = jnp.maximum(jnp.minimum(ci + u_c, u_s * (lane + 1)) - jnp.maximum(ci, u_s * lane), 0)
    extra = jnp.where(is_v, overlap.astype(F32), 0.0)
    ones_hi = jnp.where(jnp.logical_and(is_v, lane >= HEAD_DIM), 1.0, 0.0)
    lane_p = lax.broadcasted_iota(jnp.int32, (CMP_PAD, LANE), 1)
    pad_lo = jnp.where(jnp.logical_and(jnp.logical_not(is_v), lane_p == HEAD_DIM), MASK_BF16, 0.0)
    for g in range(NSA_KV):
        o_ref[g, 0:CMP_PAD, 0:LANE] = pad_lo
        o_ref[g, 0:CMP_PAD, LANE:2 * LANE] = jnp.zeros((CMP_PAD, LANE), F32)
        o_ref[g, CMP_PAD:CMP_PAD + nu, 0:LANE] = outs[g] + ones_hi
        o_ref[g, CMP_PAD:CMP_PAD + nu, LANE:2 * LANE] = extra


def _compress(cmp_raw, batch, seq, pe, w1, b1, w2):
    nu = seq // CMP_STRIDE
    half = CMP_STRIDE * HEAD_DIM
    kd = NSA_KV * half
    raw = cmp_raw.reshape(2, batch, nu, kd)
    w2p = jnp.pad(w2, ((0, 0), (0, 0), (0, LANE - HEAD_DIM)))
    rows = CMP_PAD + nu
    return pl.pallas_call(
        _compress_kernel,
        grid=(2, batch),
        in_specs=[
            pl.BlockSpec((None, None, nu, kd), lambda k, b: (k, b, 0, 0)),
            pl.BlockSpec((None, 2, 1, half), lambda k, b: (k, 0, 0, 0)),
            pl.BlockSpec((None, 2 * half, CMP_HIDDEN), lambda k, b: (k, 0, 0)),
            pl.BlockSpec((None, 1, CMP_HIDDEN), lambda k, b: (k, 0, 0)),
            pl.BlockSpec((None, CMP_HIDDEN, LANE), lambda k, b: (k, 0, 0)),
        ],
        out_specs=pl.BlockSpec((None, None, NSA_KV, rows, 2 * LANE), lambda k, b: (k, b, 0, 0, 0)),
        out_shape=jax.ShapeDtypeStruct((2, batch, NSA_KV, rows, 2 * LANE), F32),
        compiler_params=_params(("arbitrary", "arbitrary")),
        name="compress",
    )(raw, pe.reshape(2, 2, 1, half), w1.astype(BF16), b1.reshape(2, 1, CMP_HIDDEN), w2p.astype(BF16))


def _t5_bucket(dist):
    n = jnp.maximum(dist, 0)
    max_exact = N_BUCKETS // 2
    nf = jnp.maximum(n, max_exact).astype(F32)
    large = max_exact + jnp.floor(jnp.log(nf * (1.0 / max_exact)) / math.log(MAX_DISTANCE / max_exact)
                                  * (N_BUCKETS - max_exact)).astype(jnp.int32)
    large = jnp.minimum(large, N_BUCKETS - 1)
    return jnp.where(n < max_exact, n, large)


def _tables_kernel(rb_ref, tb_ref, tc_ref, wm_ref):
    def fill(out_ref, dist):
        bk = _t5_bucket(dist)
        for h in range(NSA_HEADS):
            far = rb_ref[N_BUCKETS - 1, h]
            val = jnp.zeros(dist.shape, F32)
            for b in range(N_BUCKETS - 1):
                val = jnp.where(bk == b, (rb_ref[b, h] - far) * LOG2E, val)
            out_ref[h] = jnp.where(dist >= 0, val, MASK_F32)

    tq = lax.broadcasted_iota(jnp.int32, (Q_BLOCK, 2 * Q_BLOCK), 0)
    j = lax.broadcasted_iota(jnp.int32, (Q_BLOCK, 2 * Q_BLOCK), 1)
    fill(tb_ref, tq + Q_BLOCK - j)
    tq = lax.broadcasted_iota(jnp.int32, (Q_BLOCK, LANE), 0)
    c = lax.broadcasted_iota(jnp.int32, (Q_BLOCK, LANE), 1)
    cmp_first = Q_BLOCK // CMP_STRIDE - LANE
    fill(tc_ref, tq - CMP_STRIDE * (cmp_first + c) - (CMP_BLOCK - 1))
    wm_ref[...] = jnp.where(c > tq, 0.0, MASK_F32)


def _bias_tables(rel_bias):
    return pl.pallas_call(
        _tables_kernel,
        in_specs=[pl.BlockSpec(memory_space=pltpu.SMEM)],
        out_shape=[jax.ShapeDtypeStruct((NSA_HEADS, Q_BLOCK, 2 * Q_BLOCK), F32),
                   jax.ShapeDtypeStruct((NSA_HEADS, Q_BLOCK, LANE), F32),
                   jax.ShapeDtypeStruct((Q_BLOCK, LANE), F32)],
        name="bias_tables",
    )(rel_bias)


_N_GROUP_SCRATCH = 8


def _oddeven_merge_sort(n):
    def merge(lo, hi, r):
        step = 2 * r
        if step < hi - lo:
            yield from merge(lo, hi, step)
            yield from merge(lo + r, hi, step)
            yield from ((k, k + r) for k in range(lo + r, hi - r, step))
        else:
            yield (lo, lo + r)

    def sort(lo, hi):
        if hi > lo:
            mid = lo + (hi - lo) // 2
            yield from sort(lo, mid)
            yield from sort(mid + 1, hi)
            yield from merge(lo, hi, 1)

    return tuple(sort(0, n - 1))


def _bitonic_merge(n):
    pairs, k = [], n // 2
    while k >= 1:
        pairs += [(a, a + k) for a in range(n) if not a & k]
        k //= 2
    return tuple(pairs)


_SORT16 = _oddeven_merge_sort(N_SELECT)
_BITONIC16 = _bitonic_merge(N_SELECT)


def _nsa_kernel(q_ref, gt_ref, kc_ref, vc_ref, ks_ref, vs_ref, kw_ref, vw_ref,
                tb_ref, tc_ref, wm_ref, gx_ref, o_ref, *scratch, n_sel, select_all):
    i = pl.program_id(1)
    rows = NSA_REP * Q_BLOCK
    per_group = [scratch[g * _N_GROUP_SCRATCH:(g + 1) * _N_GROUP_SCRATCH] for g in range(NSA_KV)]
    qsel_ref = [s[0] for s in per_group]
    s_ref = [(s[1], s[2]) for s in per_group]
    m_ref = [(s[3], s[4]) for s in per_group]
    al_ref = [(s[5], s[6]) for s in per_group]
    acc_ref = [s[7] for s in per_group]

    zero_tb = jnp.zeros((rows, KV_TILE - 2 * Q_BLOCK), F32)
    cmp_per_q = Q_BLOCK // CMP_STRIDE
    c0 = pl.multiple_of(cmp_per_q * (i + 1) + CMP_PAD - CMP_SPAN, cmp_per_q)
    t0 = pl.multiple_of(Q_BLOCK * (i + 1) + KV_PAD - KV_TILE, Q_BLOCK)
    w1 = pl.multiple_of(t0 - Q_BLOCK, Q_BLOCK)
    n_tiles = (i + 1 + KV_TILE // Q_BLOCK - 1) // (KV_TILE // Q_BLOCK)
    lane = lax.broadcasted_iota(jnp.int32, (Q_BLOCK, LANE), 1)

    def bcast(col):
        return jnp.broadcast_to(col, (rows, LANE))

    def lanes(x, n):
        return jnp.concatenate([x] * n, axis=1)

    def head_out(acc, r):
        rs = slice(r * Q_BLOCK, (r + 1) * Q_BLOCK)
        a, b = acc[rs, 0:LANE], acc[rs, LANE:2 * LANE]
        return a * (1.0 / b) if r % 2 == 0 else b * (1.0 / a)

    def heads_of(g):
        return slice(g * NSA_REP, (g + 1) * NSA_REP)

    def q_of(g):
        return q_ref[heads_of(g)].reshape(rows, LANE)

    def tile0_bias(g):
        return jnp.concatenate([zero_tb, tb_ref[heads_of(g)].reshape(rows, 2 * Q_BLOCK)], axis=1)

    fronts = [None] * NSA_KV

    def front(g):
        hs = heads_of(g)
        qa = q_of(g)
        s = _dot_nt(qa, kc_ref[g, pl.ds(c0, CMP_SPAN), :].astype(BF16))
        s = s + jnp.concatenate([jnp.zeros((rows, CMP_SPAN - LANE), F32),
                                 tc_ref[hs].reshape(rows, LANE)], axis=1)
        yield
        m = jnp.maximum(jnp.max(s, axis=1, keepdims=True), MAX_CLAMP)
        p = _softmax_exp(s - m).astype(BF16)
        yield
        acc = _dot(p, vc_ref[g, pl.ds(c0, CMP_SPAN), :].astype(BF16))
        num_den = acc[:, 0:LANE]
        lane_r = lax.broadcasted_iota(jnp.int32, (rows, LANE), 1)
        den = jnp.where(lane_r < HEAD_DIM, pltpu.roll(num_den, HEAD_DIM, 1), num_den)
        inv = 1.0 / jnp.maximum(den, 1e-30)
        oc = num_den * inv
        impu = acc[:, LANE:2 * LANE] * inv
        imp = impu[0:Q_BLOCK]
        for r in range(1, NSA_REP):
            imp = imp + impu[r * Q_BLOCK:(r + 1) * Q_BLOCK]
        yield
        s0 = _dot_nt(qa, kw_ref[g, pl.ds(t0, KV_TILE), :]) + tile0_bias(g)
        s1 = _dot_nt(qa, kw_ref[g, pl.ds(w1, Q_BLOCK), :]) + jnp.tile(wm_ref[...], (NSA_REP, 1))
        yield
        m = jnp.maximum(jnp.max(s0, axis=1, keepdims=True), jnp.max(s1, axis=1, keepdims=True))
        p0 = _softmax_exp(s0 - m).astype(BF16)
        p1 = _softmax_exp(s1 - m).astype(BF16)
        yield
        accw = (_dot(p0, vw_ref[g, pl.ds(t0, KV_TILE), :]) + _dot(p1, vw_ref[g, pl.ds(w1, Q_BLOCK), :]))
        gx = _dot(gt_ref[...], gx_ref[g])

        def gate(branch, r):
            c = branch * NSA_REP + r
            return gx[:, c * LANE:(c + 1) * LANE]

        partial = []
        for r in range(NSA_REP):
            rs = slice(r * Q_BLOCK, (r + 1) * Q_BLOCK)
            oc_r = oc[rs] if r % 2 == 0 else pltpu.roll(oc[rs], HEAD_DIM, 1)
            partial.append(gate(0, r) * oc_r + gate(2, r) * head_out(accw, r))
        fronts[g] = (partial, [gate(1, r) for r in range(NSA_REP)])
        yield
        notsel_q = not_selected(imp).astype(BF16)
        qsel_ref[g][:, 0:LANE] = qa
        for r in range(NSA_REP):
            qsel_ref[g][r * Q_BLOCK:(r + 1) * Q_BLOCK, LANE:2 * LANE] = notsel_q
        yield
        s = _dot_nt(qsel_ref[g][...], ks_ref[g, pl.ds(t0, KV_TILE), :]) + tile0_bias(g)
        yield
        s_ref[g][0][...] = s
        m_ref[g][0][...] = bcast(jnp.max(s, axis=1, keepdims=True))
        al_ref[g][0][...] = jnp.ones((rows, LANE), F32)
        acc_ref[g][...] = jnp.zeros((rows, 2 * LANE), F32)

    def not_selected(imp):
        jb = lax.broadcasted_iota(jnp.int32, (LANE, Q_BLOCK), 0)
        t = i * Q_BLOCK + lax.broadcasted_iota(jnp.int32, (LANE, Q_BLOCK), 1)
        cur = t // SEL_BLOCK
        forced = (jb == 0) | (jb == cur) | (jb == cur - 1)
        score = jnp.where(forced, FORCE_SCORE, jnp.where(jb * SEL_BLOCK <= t, imp.T, -1.0))
        if select_all:
            return jnp.zeros((Q_BLOCK, LANE), F32)
        x = [score[SUBLANE * v:SUBLANE * (v + 1), :] for v in range(LANE // SUBLANE)]
        y = list(x)
        for a, b in _SORT16:
            y[a], y[b] = jnp.maximum(y[a], y[b]), jnp.minimum(y[a], y[b])
        for shift in (4, 2, 1):
            z = [jnp.maximum(y[k], pltpu.roll(y[n_sel - 1 - k], shift, 0)) for k in range(n_sel)]
            if shift > 1:
                for a, b in _BITONIC16:
                    z[a], z[b] = jnp.maximum(z[a], z[b]), jnp.minimum(z[a], z[b])
            y = z
        while len(y) > 1:
            y = [jnp.minimum(y[2 * k], y[2 * k + 1]) for k in range(len(y) // 2)]
        tau = y[0]
        gt = jnp.concatenate([jnp.where(xv > tau, 1.0, 0.0) for xv in x], axis=0)
        eq = jnp.concatenate([jnp.where(xv == tau, 1.0, 0.0) for xv in x], axis=0)
        kb = lax.broadcasted_iota(jnp.int32, (LANE, LANE), 1)
        n_above = _dot(jnp.ones((LANE, LANE), BF16), gt.astype(BF16))
        n_tied_before = _dot(jnp.where(kb < jb, 1.0, 0.0).astype(BF16), eq.astype(BF16))
        keep = (gt > 0.0) | ((eq > 0.0) & (n_above + n_tied_before < n_sel))
        return jnp.where(keep, 0.0, 1.0).T

    def qk_stage(g, k, prev, slot):
        st = pl.multiple_of(t0 - KV_TILE * k, Q_BLOCK)
        s_n = _dot_nt(qsel_ref[g][...], ks_ref[g, pl.ds(st, KV_TILE), :])
        s_ref[g][slot][...] = s_n
        m_c = m_ref[g][prev][...]
        m_n = jnp.maximum(m_c, bcast(jnp.max(s_n, axis=1, keepdims=True)))
        m_ref[g][slot][...] = m_n
        al_ref[g][slot][...] = _softmax_exp(m_c - m_n)

    def pv_stage(g, k, slot):
        st = pl.multiple_of(t0 - KV_TILE * k, Q_BLOCK)
        p = _softmax_exp(s_ref[g][slot][...] - lanes(m_ref[g][slot][...], KV_TILE // LANE))
        acc_ref[g][...] = (lanes(al_ref[g][slot][...], 2) * acc_ref[g][...]
                           + _dot(p.astype(BF16), vs_ref[g, pl.ds(st, KV_TILE), :]))

    groups = range(NSA_KV)
    _interleave([front(g) for g in groups])

    def tile_pair(j):
        for g in groups:
            qk_stage(g, 2 * j + 1, 0, 1)
        for g in groups:
            pv_stage(g, 2 * j, 0)
        for g in groups:
            qk_stage(g, 2 * j + 2, 1, 0)
        for g in groups:
            pv_stage(g, 2 * j + 1, 1)

    def two_pairs(j, carry):
        tile_pair(2 * j)
        tile_pair(2 * j + 1)
        return carry

    def one_pair(j, carry):
        tile_pair(j)
        return carry

    n_pairs = (n_tiles - 1) // 2
    lax.fori_loop(0, n_pairs // 2, two_pairs, 0)
    lax.fori_loop(2 * (n_pairs // 2), n_pairs, one_pair, 0)
    odd_left = n_tiles - 1 - 2 * n_pairs

    @pl.when(odd_left == 1)
    def _two_left():
        for g in groups:
            pv_stage(g, n_tiles - 2, 0)
        for g in groups:
            qk_stage(g, n_tiles - 1, 0, 0)

    for g in groups:
        pv_stage(g, n_tiles - 1, 0)
        partial, sel_gate = fronts[g]
        acc = acc_ref[g][...]
        heads = [partial[r] + sel_gate[r] * head_out(acc, r) for r in range(NSA_REP)]
        for pr in range(NSA_REP // 2):
            col = (g * NSA_REP // 2 + pr) * LANE
            o_ref[:, col:col + LANE] = jnp.where(lane < HEAD_DIM, heads[2 * pr], heads[2 * pr + 1])


def _nsa(q_aug, gates, cmp_aug, ksel, vsel, kwin, vwin, tb, tc, wm, batch, seq):
    nq = seq // Q_BLOCK
    sp = seq + KV_PAD
    crow = cmp_aug.shape[3]
    n_sel = min(N_SELECT, seq // SEL_BLOCK)
    assert seq // SEL_BLOCK <= LANE and KV_PAD >= KV_TILE and KV_PAD >= WINDOW
    assert seq // CMP_STRIDE <= CMP_SPAN <= CMP_PAD and crow == CMP_PAD + seq // CMP_STRIDE
    assert seq // SEL_BLOCK <= N_SELECT or n_sel == LANE // 8
    n_gate = 3 * NSA_REP
    col = jnp.arange(n_gate * LANE) // LANE
    row = jnp.arange(2 * LANE) % LANE
    gate_spread = jnp.stack([(row[:, None] == g * n_gate + col[None, :]) for g in range(NSA_KV)]).astype(BF16)
    once = pl.Buffered(1)
    kv_spec = lambda w: pl.BlockSpec((None, NSA_KV, sp, w), lambda b, i: (b, 0, 0, 0), pipeline_mode=once)
    cmp_spec = lambda k, w: pl.BlockSpec((None, None, NSA_KV, crow, w), lambda b, i: (k, b, 0, 0, 0),
                                         pipeline_mode=once)
    in_specs = [
        pl.BlockSpec((None, NSA_HEADS, Q_BLOCK, LANE), lambda b, i: (b, 0, i, 0)),
        pl.BlockSpec((Q_BLOCK, 2 * LANE), lambda b, i: (b * nq + i, 0)),
        cmp_spec(0, LANE), cmp_spec(1, 2 * LANE),
        kv_spec(2 * LANE), kv_spec(2 * LANE), kv_spec(LANE), kv_spec(2 * LANE),
        _const_spec((NSA_HEADS, Q_BLOCK, 2 * Q_BLOCK)),
        _const_spec((NSA_HEADS, Q_BLOCK, LANE)),
        _const_spec((Q_BLOCK, LANE)),
        _const_spec((NSA_KV, 2 * LANE, n_gate * LANE)),
    ]
    rows = NSA_REP * Q_BLOCK
    group_scratch = [pltpu.VMEM((rows, 2 * LANE), BF16),
                     pltpu.VMEM((rows, KV_TILE), F32),
                     pltpu.VMEM((rows, KV_TILE), F32),
                     pltpu.VMEM((rows, LANE), F32),
                     pltpu.VMEM((rows, LANE), F32),
                     pltpu.VMEM((rows, LANE), F32),
                     pltpu.VMEM((rows, LANE), F32),
                     pltpu.VMEM((rows, 2 * LANE), F32)]
    assert len(group_scratch) == _N_GROUP_SCRATCH
    return pl.pallas_call(
        functools.partial(_nsa_kernel, n_sel=n_sel, select_all=seq // SEL_BLOCK <= N_SELECT),
        grid=(batch, nq),
        in_specs=in_specs,
        out_specs=pl.BlockSpec((Q_BLOCK, NSA_WIDTH), lambda b, i: (b * nq + i, 0)),
        out_shape=jax.ShapeDtypeStruct((batch * seq, NSA_WIDTH), F32),
        scratch_shapes=group_scratch * NSA_KV,
        compiler_params=_params(("arbitrary", "arbitrary")),
        name="nsa",
    )(q_aug, gates, cmp_aug, cmp_aug, ksel, vsel, kwin, vwin, tb, tc, wm, gate_spread)


def _memkv_kernel(mem_ref, g_ref, wk_ref, wv_ref, k_ref, v_ref):
    h = _rms(mem_ref[...], g_ref[...]).astype(BF16)
    k_ref[...] = _dot(h, wk_ref[...]).astype(BF16)
    v_ref[...] = _dot(h, wv_ref[...]).astype(BF16)


def _mem_kv(mem, g_mem, wkv):
    batch, m, _ = mem.shape
    blk = pl.BlockSpec((None, m, D_MODEL), lambda b: (b, 0, 0))
    return pl.pallas_call(
        _memkv_kernel,
        grid=(batch,),
        in_specs=[blk, _const_spec((1, D_MODEL)), _const_spec((D_MODEL, D_MODEL)),
                  _const_spec((D_MODEL, D_MODEL))],
        out_specs=[blk, blk],
        out_shape=[jax.ShapeDtypeStruct((batch, m, D_MODEL), BF16)] * 2,
        compiler_params=_params(("arbitrary",)),
        name="mem_kv",
    )(mem, g_mem.reshape(1, -1), wkv[:, :D_MODEL].astype(BF16), wkv[:, D_MODEL:].astype(BF16))


def _mixout_kernel(x_ref, ya_ref, yb_ref, gb_ref, wo_ref, post_ref, xpre_ref, wq_ref,
                   km_ref, vm_ref, wxo_ref, xpost_ref, o_ref):
    tm = x_ref.shape[0]

    def chain(rows):
        yb = _rms(yb_ref[rows, :], gb_ref[...]).astype(BF16)
        ya = ya_ref[rows, :].astype(BF16)
        yield
        y = _dot(ya, wo_ref[0:GM_WIDTH, :]) + _dot(yb, wo_ref[GM_WIDTH:, :])
        yield
        x1 = x_ref[rows, :] + _rms(y, post_ref[...])
        h = _rms(x1, xpre_ref[...]).astype(BF16)
        yield
        q = _dot(h, wq_ref[...]).astype(BF16)
        yield
        outs = []
        for hd in range(XA_HEADS):
            hs = slice(hd * XA_HEAD_DIM, (hd + 1) * XA_HEAD_DIM)
            s = _dot_nt(q[:, hs], km_ref[:, hs])
            yield
            e = _softmax_exp(s - jnp.max(s, axis=1, keepdims=True))
            den = jnp.sum(e, axis=1, keepdims=True)
            e = e.astype(BF16)
            yield
            outs.append((_dot(e, vm_ref[:, hs]) / den).astype(BF16))
        o = _dot(jnp.concatenate(outs, axis=1), wxo_ref[...])
        yield
        o_ref[rows, :] = x1 + _rms(o, xpost_ref[...])

    _interleave([chain(slice(k * tm // 2, (k + 1) * tm // 2)) for k in range(2)])


def _mix_out(x2d, ya, yb, batch, seq, gain_b, w_out, mix_post, xa_pre, wq, kmem, vmem, wo, xa_post):
    tm = 2 * TOKEN_TILE if seq % (2 * TOKEN_TILE) == 0 else TOKEN_TILE
    nt = seq // tm
    row = lambda w: pl.BlockSpec((tm, w), lambda b, t: (b * nt + t, 0))
    mem_spec = pl.BlockSpec((None, kmem.shape[1], D_MODEL), lambda b, t: (b, 0, 0))
    sq = _const_spec((D_MODEL, D_MODEL))
    vec = _const_spec((1, D_MODEL))
    return pl.pallas_call(
        _mixout_kernel,
        grid=(batch, nt),
        in_specs=[row(D_MODEL), row(GM_WIDTH), row(NSA_WIDTH), _const_spec((1, NSA_WIDTH)), sq, vec,
                  vec, sq, mem_spec, mem_spec, sq, vec],
        out_specs=row(D_MODEL),
        out_shape=jax.ShapeDtypeStruct(x2d.shape, F32),
        compiler_params=_params(("arbitrary", "arbitrary")),
        name="mix_out",
    )(x2d, ya, yb, gain_b.reshape(1, -1), w_out.astype(BF16), mix_post.reshape(1, -1),
      xa_pre.reshape(1, -1), (wq * (XA_HEAD_DIM ** -0.5 * LOG2E)).astype(BF16), kmem, vmem,
      wo.astype(BF16), xa_post.reshape(1, -1))


def kernel(x, mem, ffn1_pre, ffn1_post, ffn1_wg, ffn1_wu, ffn1_wd, mix_pre, mix_post, w_in, gm_ln_g, gm_ln_b, gm_ws, gm_bs, ck_pe, ck_w1, ck_b1, ck_w2, cv_pe, cv_w1, cv_b1, cv_w2, rel_bias, out_gain_a, out_gain_b, w_out, xa_pre, xa_post, mem_norm, xa_wq, xa_wkv, xa_wo, ffn2_pre, ffn2_post, ffn2_wg, ffn2_wu, ffn2_wd):
    batch, seq, _ = x.shape
    depth = ffn1_pre.shape[0]
    x2d = x.reshape(batch * seq, D_MODEL)
    tb, tc, wm = _bias_tables(rel_bias)
    for l in range(depth):
        x2d = _ffn(x2d, ffn1_pre[l], ffn1_wg[l], ffn1_wu[l], ffn1_wd[l], ffn1_post[l])
        ya, q_aug, cmp_raw, ksel, vsel, kwin, vwin, gates = _mixproj(
            x2d, batch, seq, mix_pre[l], w_in[l], gm_ln_g[l], gm_ln_b[l], gm_ws[l], gm_bs[l],
            out_gain_a[l])
        cmp_aug = _compress(cmp_raw, batch, seq,
                            jnp.stack([ck_pe[l], cv_pe[l]]), jnp.stack([ck_w1[l], cv_w1[l]]),
                            jnp.stack([ck_b1[l], cv_b1[l]]), jnp.stack([ck_w2[l], cv_w2[l]]))
        yb = _nsa(q_aug, gates, cmp_aug, ksel, vsel, kwin, vwin, tb, tc, wm, batch, seq)
        kmem, vmem = _mem_kv(mem, mem_norm[l], xa_wkv[l])
        x2d = _mix_out(x2d, ya, yb, batch, seq, out_gain_b[l], w_out[l], mix_post[l], xa_pre[l],
                       xa_wq[l], kmem, vmem, xa_wo[l], xa_post[l])
        x2d = _ffn(x2d, ffn2_pre[l], ffn2_wg[l], ffn2_wu[l], ffn2_wd[l], ffn2_post[l])
    return x2d.reshape(batch, seq, D_MODEL)
```

```python
import functools
import math

import jax
import jax.numpy as jnp
from jax import lax
from jax.experimental import pallas as pl
from jax.experimental.pallas import tpu as pltpu

F32 = jnp.float32
BF16 = jnp.bfloat16

D_MODEL = 1024
MEM_LEN = 256
EPS = 1e-6
D_FF = 2816
GM_WIDTH = 512
GM_GROUPS = 4
GM_CH = GM_WIDTH // GM_GROUPS
CHUNK = 128
NSA_WIDTH = D_MODEL - GM_WIDTH
HEAD_DIM = 64
NSA_HEADS = NSA_WIDTH // HEAD_DIM
NSA_KV = 2
NSA_REP = NSA_HEADS // NSA_KV
KV_W = NSA_KV * HEAD_DIM
CMP_BLOCK = 32
CMP_STRIDE = 16
CMP_HIDDEN = 256
SEL_BLOCK = 64
N_SELECT = 16
WINDOW = 512
Q_BLOCK = 128
N_BUCKETS = 32
MAX_DISTANCE = 128
XA_HEADS = 4
XA_HEAD_DIM = D_MODEL // XA_HEADS
FORCE_SCORE = 1e4

LANE = 128
SUBLANE = 8
TOKEN_TILE = 512
FF_CHUNK = 256
KV_PAD = 512
KV_TILE = 512
CMP_PAD = 512
CMP_SPAN = 512
MASK_BF16 = -(2.0 ** 100)
MASK_F32 = -1e30
MAX_CLAMP = -(2.0 ** 99)
LOG2E = math.log2(math.e)
_softmax_exp = jnp.exp2
VMEM_LIMIT = 56 * 1024 * 1024

_NT = (((1,), (1,)), ((), ()))


def _rms(x, g):
    ms = jnp.mean(x * x, axis=-1, keepdims=True)
    return x * lax.rsqrt(ms + EPS) * g


def _dot(a, b):
    return jnp.dot(a, b, preferred_element_type=F32)


def _dot_nt(a, b):
    return lax.dot_general(a, b, _NT, preferred_element_type=F32)


def _params(sem, vmem=VMEM_LIMIT):
    return pltpu.CompilerParams(dimension_semantics=sem, vmem_limit_bytes=vmem)


def _const_spec(shape):
    nd = len(shape)
    return pl.BlockSpec(shape, lambda *_: (0,) * nd)


def _interleave(chains, skew=1):
    pending = list(chains)
    for _ in range(skew):
        next(pending[0])
    while pending:
        for gen in list(pending):
            if next(gen, pending) is pending:
                pending.remove(gen)


def _ffn_kernel(x_ref, pre_ref, wg_ref, wu_ref, wd_ref, post_ref, o_ref, acc_ref):
    tm = x_ref.shape[0]

    def chain(rows):
        h = _rms(x_ref[rows, :], pre_ref[...]).astype(BF16)
        yield
        for c in range(D_FF // FF_CHUNK):
            sl = slice(c * FF_CHUNK, (c + 1) * FF_CHUNK)
            g = _dot(h, wg_ref[:, sl])
            u = _dot(h, wu_ref[:, sl])
            a = (jax.nn.silu(g) * u).astype(BF16)
            d = _dot(a, wd_ref[sl, :])
            if c == 0:
                acc_ref[rows, :] = d
            else:
                acc_ref[rows, :] += d
            yield
        o_ref[rows, :] = x_ref[rows, :] + 0.5 * _rms(acc_ref[rows, :], post_ref[...])

    _interleave([chain(slice(k * tm // 2, (k + 1) * tm // 2)) for k in range(2)])


def _ffn(x2d, pre, wg, wu, wd, post):
    n = x2d.shape[0]
    tm = 2 * TOKEN_TILE if n % (2 * TOKEN_TILE) == 0 else TOKEN_TILE
    assert n % tm == 0
    row = pl.BlockSpec((tm, D_MODEL), lambda t: (t, 0))
    once = pl.Buffered(1)
    weight = lambda shape: pl.BlockSpec(shape, lambda t: (0, 0), pipeline_mode=once)
    return pl.pallas_call(
        _ffn_kernel,
        grid=(n // tm,),
        in_specs=[row, _const_spec((1, D_MODEL)), weight((D_MODEL, D_FF)),
                  weight((D_MODEL, D_FF)), weight((D_FF, D_MODEL)),
                  _const_spec((1, D_MODEL))],
        out_specs=row,
        out_shape=jax.ShapeDtypeStruct((n, D_MODEL), F32),
        scratch_shapes=[pltpu.VMEM((tm, D_MODEL), F32)],
        compiler_params=_params(("arbitrary",)),
        name="ffn",
    )(x2d, pre.reshape(1, -1), wg.astype(BF16), wu.astype(BF16), wd.astype(BF16),
      post.reshape(1, -1))


def _mixproj_kernel(x_ref, pre_ref, win_ref, wgt_ref, lng_ref, lnb_ref, ws_ref, bs_ref, ga_ref,
                    ya_ref, q_ref, cmp_ref, ksel_ref, vsel_ref, kwin_ref, vwin_ref, gt_ref,
                    cmps_ref, w_ref, *, q_scale):
    t = pl.program_id(1)
    tm = x_ref.shape[0]
    lane = lax.broadcasted_iota(jnp.int32, (tm, LANE), 1)
    flag_col = jnp.where(lane == HEAD_DIM, MASK_BF16, 0.0).astype(BF16)

    @pl.when(t == 0)
    def _pad_rows():
        w_ref[...] = win_ref[:, 0:_O_GATE].astype(BF16)
        for g in range(NSA_KV):
            ksel_ref[g, :, 0:LANE] = flag_col
            ksel_ref[g, :, LANE:2 * LANE] = jnp.zeros((tm, LANE), BF16)
            kwin_ref[g] = flag_col
            vsel_ref[g] = jnp.zeros((tm, 2 * LANE), BF16)
            vwin_ref[g] = jnp.zeros((tm, 2 * LANE), BF16)

    def chain(r0, n):
        rows = slice(r0, r0 + n)
        h = _rms(x_ref[rows, :], pre_ref[...]).astype(BF16)
        yield
        u = jax.nn.gelu(_dot(h, w_ref[:, 0:GM_WIDTH]))
        v = jax.nn.gelu(_dot(h, w_ref[:, GM_WIDTH:2 * GM_WIDTH]))
        yield
        r_i = lax.broadcasted_iota(jnp.int32, (CHUNK, CHUNK), 0)
        c_i = lax.broadcasted_iota(jnp.int32, (CHUNK, CHUNK), 1)
        for g in range(GM_GROUPS):
            gs = slice(g * GM_CH, (g + 1) * GM_CH)
            vg = v[:, gs]
            mu = jnp.mean(vg, axis=-1, keepdims=True)
            var = jnp.mean(jnp.square(vg - mu), axis=-1, keepdims=True)
            vn = ((vg - mu) * lax.rsqrt(var + EPS) * lng_ref[:, gs] + lnb_ref[:, gs]).astype(BF16)
            w = jnp.where(c_i <= r_i, ws_ref[g], 0.0).astype(BF16)
            for c in range(n // CHUNK):
                rs = slice(c * CHUNK, (c + 1) * CHUNK)
                s = _dot(w, vn[rs]) + bs_ref[:, g:g + 1]
                ya_ref[r0 + c * CHUNK:r0 + (c + 1) * CHUNK, gs] = u[rs, gs] * s
        yield
        ya_ref[rows, :] = _rms(ya_ref[rows, :], ga_ref[...])
        lane = lax.broadcasted_iota(jnp.int32, (n, LANE), 1)
        lane_lo = lane < HEAD_DIM

        def halves(z):
            return z, pltpu.roll(z, HEAD_DIM, 1)

        one_col = jnp.where(lane == HEAD_DIM, 1.0, 0.0)
        zq = _dot(h, w_ref[:, _O_Q:_O_C]) * q_scale
        yield
        for a in range(NSA_HEADS // 2):
            for hd, qh in zip((2 * a, 2 * a + 1), halves(zq[:, a * LANE:(a + 1) * LANE])):
                q_ref[hd, rows, :] = jnp.where(lane_lo, qh, one_col).astype(BF16)
        zc = _dot(h, w_ref[:, _O_C:_O_KV])
        yield
        nr = n // CMP_STRIDE
        urows = slice(r0 // CMP_STRIDE, r0 // CMP_STRIDE + nr)
        lo_u = lax.broadcasted_iota(jnp.int32, (nr, LANE), 1) < HEAD_DIM
        for kv in range(2):
            cmps_ref[kv, rows, :] = zc[:, kv * KV_W:(kv + 1) * KV_W]
            for a in range(CMP_STRIDE // 2):
                za = cmps_ref[kv, pl.ds(r0 + 2 * a, nr, stride=CMP_STRIDE), :]
                zb = cmps_ref[kv, pl.ds(r0 + 2 * a + 1, nr, stride=CMP_STRIDE), :]
                cmp_ref[kv, urows, a * LANE:(a + 1) * LANE] = jnp.where(
                    lo_u, za, pltpu.roll(zb, HEAD_DIM, 1))
                cmp_ref[kv, urows, (CMP_STRIDE // 2 + a) * LANE:(CMP_STRIDE // 2 + a + 1) * LANE] = (
                    jnp.where(lo_u, pltpu.roll(za, HEAD_DIM, 1), zb))
        zkv = _dot(h, w_ref[:, _O_KV:_O_GATE])
        yield
        k_sel, v_sel, k_win, v_win = (halves(zkv[:, j * LANE:(j + 1) * LANE]) for j in range(4))
        pos_blk = ((t - 1) * tm + r0 + lax.broadcasted_iota(jnp.int32, (n, LANE), 0)) // SEL_BLOCK
        blk_onehot = jnp.where(lane == pos_blk, MASK_BF16, 0.0).astype(BF16)

        def v_aug(v, g):
            return jnp.concatenate([jnp.where(lane_lo, v[g], 1.0), jnp.where(lane_lo, 1.0, v[1 - g])],
                                   axis=1).astype(BF16)

        for g in range(NSA_KV):
            ksel_ref[g, rows, 0:LANE] = jnp.where(lane_lo, k_sel[g], 0.0).astype(BF16)
            ksel_ref[g, rows, LANE:2 * LANE] = blk_onehot
            vsel_ref[g, rows, :] = v_aug(v_sel, g)
            kwin_ref[g, rows, :] = jnp.where(lane_lo, k_win[g], 0.0).astype(BF16)
            vwin_ref[g, rows, :] = v_aug(v_win, g)
        sg = jax.nn.sigmoid(_dot(h, wgt_ref[...]))
        hi = sg.astype(BF16)
        gt_ref[rows, 0:LANE] = hi
        gt_ref[rows, LANE:2 * LANE] = (sg - hi.astype(F32)).astype(BF16)

    @pl.when(t > 0)
    def _tile():
        _interleave([chain(k * tm // 2, tm // 2) for k in range(2)])


_O_Q = 2 * GM_WIDTH
_O_C = _O_Q + NSA_WIDTH
_O_KV = _O_C + 2 * KV_W
_O_GATE = _O_KV + 4 * KV_W


def _mixproj(x2d, batch, seq, mix_pre, w_in, ln_g, ln_b, ws, bs, gain_a):
    tm = TOKEN_TILE
    nt = seq // tm
    assert seq % tm == 0 and KV_PAD == tm
    q_scale = HEAD_DIM ** -0.5 * LOG2E
    n_gate = 3 * NSA_HEADS
    w_g = w_in[:, _O_GATE:].reshape(D_MODEL, NSA_KV, NSA_REP, 3).transpose(0, 1, 3, 2)
    w_g = jnp.pad(w_g.reshape(D_MODEL, n_gate), ((0, 0), (0, LANE - n_gate))).astype(BF16)

    def tile(b, t):
        return b * nt + jnp.maximum(t - 1, 0)

    n = batch * seq
    sp = seq + KV_PAD
    out_shape = [
        jax.ShapeDtypeStruct((n, GM_WIDTH), F32),
        jax.ShapeDtypeStruct((batch, NSA_HEADS, seq, LANE), BF16),
        jax.ShapeDtypeStruct((2, n // CMP_STRIDE, CMP_STRIDE * KV_W), F32),
        jax.ShapeDtypeStruct((batch, NSA_KV, sp, 2 * LANE), BF16),
        jax.ShapeDtypeStruct((batch, NSA_KV, sp, 2 * LANE), BF16),
        jax.ShapeDtypeStruct((batch, NSA_KV, sp, LANE), BF16),
        jax.ShapeDtypeStruct((batch, NSA_KV, sp, 2 * LANE), BF16),
        jax.ShapeDtypeStruct((n, 2 * LANE), BF16),
    ]
    padded = lambda w: pl.BlockSpec((None, NSA_KV, tm, w), lambda b, t: (b, 0, t, 0))
    out_specs = [
        pl.BlockSpec((tm, GM_WIDTH), lambda b, t: (tile(b, t), 0)),
        pl.BlockSpec((None, NSA_HEADS, tm, LANE), lambda b, t: (b, 0, jnp.maximum(t - 1, 0), 0)),
        pl.BlockSpec((2, tm // CMP_STRIDE, CMP_STRIDE * KV_W), lambda b, t: (0, tile(b, t), 0)),
        padded(2 * LANE), padded(2 * LANE), padded(LANE), padded(2 * LANE),
        pl.BlockSpec((tm, 2 * LANE), lambda b, t: (tile(b, t), 0)),
    ]
    in_specs = [
        pl.BlockSpec((tm, D_MODEL), lambda b, t: (tile(b, t), 0)),
        _const_spec((1, D_MODEL)),
        pl.BlockSpec(w_in.shape, lambda b, t: (0, 0), pipeline_mode=pl.Buffered(1)), _const_spec(w_g.shape),
        _const_spec((1, GM_WIDTH)), _const_spec((1, GM_WIDTH)),
        _const_spec((GM_GROUPS, CHUNK, CHUNK)), _const_spec((CHUNK, GM_GROUPS)),
        _const_spec((1, GM_WIDTH)),
    ]
    return pl.pallas_call(
        functools.partial(_mixproj_kernel, q_scale=q_scale),
        grid=(batch, nt + 1),
        in_specs=in_specs,
        out_specs=out_specs,
        out_shape=out_shape,
        scratch_shapes=[pltpu.VMEM((2, tm, KV_W), F32),
                        pltpu.VMEM((D_MODEL, _O_GATE), BF16)],
        compiler_params=_params(("arbitrary", "arbitrary")),
        name="mixproj",
    )(x2d, mix_pre.reshape(1, -1), w_in, w_g, ln_g.reshape(1, -1), ln_b.reshape(1, -1),
      ws, bs.T, gain_a.reshape(1, -1))


def _compress_kernel(raw_ref, pe_ref, w1_ref, b1_ref, w2_ref, o_ref):
    is_v = pl.program_id(0) == 1
    nu = raw_ref.shape[0]
    half = CMP_STRIDE * HEAD_DIM
    outs = []
    for g in range(NSA_KV):
        u = raw_ref[:, g * half:(g + 1) * half]
        a = _dot((u + pe_ref[0]).astype(BF16), w1_ref[0:half, :])
        b = _dot((u + pe_ref[1]).astype(BF16), w1_ref[half:2 * half, :])
        hid = jax.nn.gelu(a + pltpu.roll(b, nu - 1, 0) + b1_ref[...])
        outs.append(_dot(hid.astype(BF16), w2_ref[...]))
    lane = lax.broadcasted_iota(jnp.int32, (nu, LANE), 1)
    ci = lax.broadcasted_iota(jnp.int32, (nu, LANE), 0)
    u_c = CMP_BLOCK // CMP_STRIDE
    u_s = SEL_BLOCK // CMP_STRIDE
    overlap = jnp.maximum(jnp.minimum(ci + u_c, u_s * (lane + 1)) - jnp.maximum(ci, u_s * lane), 0)
    extra = jnp.where(is_v, overlap.astype(F32), 0.0)
    ones_hi = jnp.where(jnp.logical_and(is_v, lane >= HEAD_DIM), 1.0, 0.0)
    lane_p = lax.broadcasted_iota(jnp.int32, (CMP_PAD, LANE), 1)
    pad_lo = jnp.where(jnp.logical_and(jnp.logical_not(is_v), lane_p == HEAD_DIM), MASK_BF16, 0.0)
    for g in range(NSA_KV):
        o_ref[g, 0:CMP_PAD, 0:LANE] = pad_lo
        o_ref[g, 0:CMP_PAD, LANE:2 * LANE] = jnp.zeros((CMP_PAD, LANE), F32)
        o_ref[g, CMP_PAD:CMP_PAD + nu, 0:LANE] = outs[g] + ones_hi
        o_ref[g, CMP_PAD:CMP_PAD + nu, LANE:2 * LANE] = extra


def _compress(cmp_raw, batch, seq, pe, w1, b1, w2):
    nu = seq // CMP_STRIDE
    half = CMP_STRIDE * HEAD_DIM
    kd = NSA_KV * half
    raw = cmp_raw.reshape(2, batch, nu, kd)
    w2p = jnp.pad(w2, ((0, 0), (0, 0), (0, LANE - HEAD_DIM)))
    rows = CMP_PAD + nu
    return pl.pallas_call(
        _compress_kernel,
        grid=(2, batch),
        in_specs=[
            pl.BlockSpec((None, None, nu, kd), lambda k, b: (k, b, 0, 0)),
            pl.BlockSpec((None, 2, 1, half), lambda k, b: (k, 0, 0, 0)),
            pl.BlockSpec((None, 2 * half, CMP_HIDDEN), lambda k, b: (k, 0, 0)),
            pl.BlockSpec((None, 1, CMP_HIDDEN), lambda k, b: (k, 0, 0)),
            pl.BlockSpec((None, CMP_HIDDEN, LANE), lambda k, b: (k, 0, 0)),
        ],
        out_specs=pl.BlockSpec((None, None, NSA_KV, rows, 2 * LANE), lambda k, b: (k, b, 0, 0, 0)),
        out_shape=jax.ShapeDtypeStruct((2, batch, NSA_KV, rows, 2 * LANE), F32),
        compiler_params=_params(("arbitrary", "arbitrary")),
        name="compress",
    )(raw, pe.reshape(2, 2, 1, half), w1.astype(BF16), b1.reshape(2, 1, CMP_HIDDEN), w2p.astype(BF16))


def _t5_bucket(dist):
    n = jnp.maximum(dist, 0)
    max_exact = N_BUCKETS // 2
    nf = jnp.maximum(n, max_exact).astype(F32)
    large = max_exact + jnp.floor(jnp.log(nf * (1.0 / max_exact)) / math.log(MAX_DISTANCE / max_exact)
                                  * (N_BUCKETS - max_exact)).astype(jnp.int32)
    large = jnp.minimum(large, N_BUCKETS - 1)
    return jnp.where(n < max_exact, n, large)


def _tables_kernel(rb_ref, tb_ref, tc_ref, wm_ref):
    def fill(out_ref, dist):
        bk = _t5_bucket(dist)
        for h in range(NSA_HEADS):
            far = rb_ref[N_BUCKETS - 1, h]
            val = jnp.zeros(dist.shape, F32)
            for b in range(N_BUCKETS - 1):
                val = jnp.where(bk == b, (rb_ref[b, h] - far) * LOG2E, val)
            out_ref[h] = jnp.where(dist >= 0, val, MASK_F32)

    tq = lax.broadcasted_iota(jnp.int32, (Q_BLOCK, 2 * Q_BLOCK), 0)
    j = lax.broadcasted_iota(jnp.int32, (Q_BLOCK, 2 * Q_BLOCK), 1)
    fill(tb_ref, tq + Q_BLOCK - j)
    tq = lax.broadcasted_iota(jnp.int32, (Q_BLOCK, LANE), 0)
    c = lax.broadcasted_iota(jnp.int32, (Q_BLOCK, LANE), 1)
    cmp_first = Q_BLOCK // CMP_STRIDE - LANE
    fill(tc_ref, tq - CMP_STRIDE * (cmp_first + c) - (CMP_BLOCK - 1))
    wm_ref[...] = jnp.where(c > tq, 0.0, MASK_F32)


def _bias_tables(rel_bias):
    return pl.pallas_call(
        _tables_kernel,
        in_specs=[pl.BlockSpec(memory_space=pltpu.SMEM)],
        out_shape=[jax.ShapeDtypeStruct((NSA_HEADS, Q_BLOCK, 2 * Q_BLOCK), F32),
                   jax.ShapeDtypeStruct((NSA_HEADS, Q_BLOCK, LANE), F32),
                   jax.ShapeDtypeStruct((Q_BLOCK, LANE), F32)],
        name="bias_tables",
    )(rel_bias)


_N_GROUP_SCRATCH = 8


def _oddeven_merge_sort(n):
    def merge(lo, hi, r):
        step = 2 * r
        if step < hi - lo:
            yield from merge(lo, hi, step)
            yield from merge(lo + r, hi, step)
            yield from ((k, k + r) for k in range(lo + r, hi - r, step))
        else:
            yield (lo, lo + r)

    def sort(lo, hi):
        if hi > lo:
            mid = lo + (hi - lo) // 2
            yield from sort(lo, mid)
            yield from sort(mid + 1, hi)
            yield from merge(lo, hi, 1)

    return tuple(sort(0, n - 1))


def _bitonic_merge(n):
    pairs, k = [], n // 2
    while k >= 1:
        pairs += [(a, a + k) for a in range(n) if not a & k]
        k //= 2
    return tuple(pairs)


_SORT16 = _oddeven_merge_sort(N_SELECT)
_BITONIC16 = _bitonic_merge(N_SELECT)


def _nsa_kernel(q_ref, gt_ref, kc_ref, vc_ref, ks_ref, vs_ref, kw_ref, vw_ref,
                tb_ref, tc_ref, wm_ref, gx_ref, o_ref, *scratch, n_sel, select_all):
    i = pl.program_id(1)
    rows = NSA_REP * Q_BLOCK
    per_group = [scratch[g * _N_GROUP_SCRATCH:(g + 1) * _N_GROUP_SCRATCH] for g in range(NSA_KV)]
    qsel_ref = [s[0] for s in per_group]
    s_ref = [(s[1], s[2]) for s in per_group]
    m_ref = [(s[3], s[4]) for s in per_group]
    al_ref = [(s[5], s[6]) for s in per_group]
    acc_ref = [s[7] for s in per_group]

    zero_tb = jnp.zeros((rows, KV_TILE - 2 * Q_BLOCK), F32)
    cmp_per_q = Q_BLOCK // CMP_STRIDE
    c0 = pl.multiple_of(cmp_per_q * (i + 1) + CMP_PAD - CMP_SPAN, cmp_per_q)
    t0 = pl.multiple_of(Q_BLOCK * (i + 1) + KV_PAD - KV_TILE, Q_BLOCK)
    w1 = pl.multiple_of(t0 - Q_BLOCK, Q_BLOCK)
    n_tiles = (i + 1 + KV_TILE // Q_BLOCK - 1) // (KV_TILE // Q_BLOCK)
    lane = lax.broadcasted_iota(jnp.int32, (Q_BLOCK, LANE), 1)

    def bcast(col):
        return jnp.broadcast_to(col, (rows, LANE))

    def lanes(x, n):
        return jnp.concatenate([x] * n, axis=1)

    def head_out(acc, r):
        rs = slice(r * Q_BLOCK, (r + 1) * Q_BLOCK)
        a, b = acc[rs, 0:LANE], acc[rs, LANE:2 * LANE]
        return a * (1.0 / b) if r % 2 == 0 else b * (1.0 / a)

    def heads_of(g):
        return slice(g * NSA_REP, (g + 1) * NSA_REP)

    def q_of(g):
        return q_ref[heads_of(g)].reshape(rows, LANE)

    def tile0_bias(g):
        return jnp.concatenate([zero_tb, tb_ref[heads_of(g)].reshape(rows, 2 * Q_BLOCK)], axis=1)

    fronts = [None] * NSA_KV

    def front(g):
        hs = heads_of(g)
        qa = q_of(g)
        s = _dot_nt(qa, kc_ref[g, pl.ds(c0, CMP_SPAN), :].astype(BF16))
        s = s + jnp.concatenate([jnp.zeros((rows, CMP_SPAN - LANE), F32),
                                 tc_ref[hs].reshape(rows, LANE)], axis=1)
        yield
        m = jnp.maximum(jnp.max(s, axis=1, keepdims=True), MAX_CLAMP)
        p = _softmax_exp(s - m).astype(BF16)
        yield
        acc = _dot(p, vc_ref[g, pl.ds(c0, CMP_SPAN), :].astype(BF16))
        num_den = acc[:, 0:LANE]
        lane_r = lax.broadcasted_iota(jnp.int32, (rows, LANE), 1)
        den = jnp.where(lane_r < HEAD_DIM, pltpu.roll(num_den, HEAD_DIM, 1), num_den)
        inv = 1.0 / jnp.maximum(den, 1e-30)
        oc = num_den * inv
        impu = acc[:, LANE:2 * LANE] * inv
        imp = impu[0:Q_BLOCK]
        for r in range(1, NSA_REP):
            imp = imp + impu[r * Q_BLOCK:(r + 1) * Q_BLOCK]
        yield
        s0 = _dot_nt(qa, kw_ref[g, pl.ds(t0, KV_TILE), :]) + tile0_bias(g)
        s1 = _dot_nt(qa, kw_ref[g, pl.ds(w1, Q_BLOCK), :]) + jnp.tile(wm_ref[...], (NSA_REP, 1))
        yield
        m = jnp.maximum(jnp.max(s0, axis=1, keepdims=True), jnp.max(s1, axis=1, keepdims=True))
        p0 = _softmax_exp(s0 - m).astype(BF16)
        p1 = _softmax_exp(s1 - m).astype(BF16)
        yield
        accw = (_dot(p0, vw_ref[g, pl.ds(t0, KV_TILE), :]) + _dot(p1, vw_ref[g, pl.ds(w1, Q_BLOCK), :]))
        gx = _dot(gt_ref[...], gx_ref[g])

        def gate(branch, r):
            c = branch * NSA_REP + r
            return gx[:, c * LANE:(c + 1) * LANE]

        partial = []
        for r in range(NSA_REP):
            rs = slice(r * Q_BLOCK, (r + 1) * Q_BLOCK)
            oc_r = oc[rs] if r % 2 == 0 else pltpu.roll(oc[rs], HEAD_DIM, 1)
            partial.append(gate(0, r) * oc_r + gate(2, r) * head_out(accw, r))
        fronts[g] = (partial, [gate(1, r) for r in range(NSA_REP)])
        yield
        notsel_q = not_selected(imp).astype(BF16)
        qsel_ref[g][:, 0:LANE] = qa
        for r in range(NSA_REP):
            qsel_ref[g][r * Q_BLOCK:(r + 1) * Q_BLOCK, LANE:2 * LANE] = notsel_q
        yield
        s = _dot_nt(qsel_ref[g][...], ks_ref[g, pl.ds(t0, KV_TILE), :]) + tile0_bias(g)
        yield
        s_ref[g][0][...] = s
        m_ref[g][0][...] = bcast(jnp.max(s, axis=1, keepdims=True))
        al_ref[g][0][...] = jnp.ones((rows, LANE), F32)
        acc_ref[g][...] = jnp.zeros((rows, 2 * LANE), F32)

    def not_selected(imp):
        jb = lax.broadcasted_iota(jnp.int32, (LANE, Q_BLOCK), 0)
        t = i * Q_BLOCK + lax.broadcasted_iota(jnp.int32, (LANE, Q_BLOCK), 1)
        cur = t // SEL_BLOCK
        forced = (jb == 0) | (jb == cur) | (jb == cur - 1)
        score = jnp.where(forced, FORCE_SCORE, jnp.where(jb * SEL_BLOCK <= t, imp.T, -1.0))
        if select_all:
            return jnp.zeros((Q_BLOCK, LANE), F32)
        x = [score[SUBLANE * v:SUBLANE * (v + 1), :] for v in range(LANE // SUBLANE)]
        y = list(x)
        for a, b in _SORT16:
            y[a], y[b] = jnp.maximum(y[a], y[b]), jnp.minimum(y[a], y[b])
        for shift in (4, 2, 1):
            z = [jnp.maximum(y[k], pltpu.roll(y[n_sel - 1 - k], shift, 0)) for k in range(n_sel)]
            if shift > 1:
                for a, b in _BITONIC16:
                    z[a], z[b] = jnp.maximum(z[a], z[b]), jnp.minimum(z[a], z[b])
            y = z
        while len(y) > 1:
            y = [jnp.minimum(y[2 * k], y[2 * k + 1]) for k in range(len(y) // 2)]
        tau = y[0]
        gt = jnp.concatenate([jnp.where(xv > tau, 1.0, 0.0) for xv in x], axis=0)
        eq = jnp.concatenate([jnp.where(xv == tau, 1.0, 0.0) for xv in x], axis=0)
        kb = lax.broadcasted_iota(jnp.int32, (LANE, LANE), 1)
        n_above = _dot(jnp.ones((LANE, LANE), BF16), gt.astype(BF16))
        n_tied_before = _dot(jnp.where(kb < jb, 1.0, 0.0).astype(BF16), eq.astype(BF16))
        keep = (gt > 0.0) | ((eq > 0.0) & (n_above + n_tied_before < n_sel))
        return jnp.where(keep, 0.0, 1.0).T

    def qk_stage(g, k, prev, slot):
        st = pl.multiple_of(t0 - KV_TILE * k, Q_BLOCK)
        s_n = _dot_nt(qsel_ref[g][...], ks_ref[g, pl.ds(st, KV_TILE), :])
        s_ref[g][slot][...] = s_n
        m_c = m_ref[g][prev][...]
        m_n = jnp.maximum(m_c, bcast(jnp.max(s_n, axis=1, keepdims=True)))
        m_ref[g][slot][...] = m_n
        al_ref[g][slot][...] = _softmax_exp(m_c - m_n)

    def pv_stage(g, k, slot):
        st = pl.multiple_of(t0 - KV_TILE * k, Q_BLOCK)
        p = _softmax_exp(s_ref[g][slot][...] - lanes(m_ref[g][slot][...], KV_TILE // LANE))
        acc_ref[g][...] = (lanes(al_ref[g][slot][...], 2) * acc_ref[g][...]
                           + _dot(p.astype(BF16), vs_ref[g, pl.ds(st, KV_TILE), :]))

    groups = range(NSA_KV)
    _interleave([front(g) for g in groups])

    def tile_pair(j):
        for g in groups:
            qk_stage(g, 2 * j + 1, 0, 1)
        for g in groups:
            pv_stage(g, 2 * j, 0)
        for g in groups:
            qk_stage(g, 2 * j + 2, 1, 0)
        for g in groups:
            pv_stage(g, 2 * j + 1, 1)

    def two_pairs(j, carry):
        tile_pair(2 * j)
        tile_pair(2 * j + 1)
        return carry

    def one_pair(j, carry):
        tile_pair(j)
        return carry

    n_pairs = (n_tiles - 1) // 2
    lax.fori_loop(0, n_pairs // 2, two_pairs, 0)
    lax.fori_loop(2 * (n_pairs // 2), n_pairs, one_pair, 0)
    odd_left = n_tiles - 1 - 2 * n_pairs

    @pl.when(odd_left == 1)
    def _two_left():
        for g in groups:
            pv_stage(g, n_tiles - 2, 0)
        for g in groups:
            qk_stage(g, n_tiles - 1, 0, 0)

    for g in groups:
        pv_stage(g, n_tiles - 1, 0)
        partial, sel_gate = fronts[g]
        acc = acc_ref[g][...]
        heads = [partial[r] + sel_gate[r] * head_out(acc, r) for r in range(NSA_REP)]
        for pr in range(NSA_REP // 2):
            col = (g * NSA_REP // 2 + pr) * LANE
            o_ref[:, col:col + LANE] = jnp.where(lane < HEAD_DIM, heads[2 * pr], heads[2 * pr + 1])


def _nsa(q_aug, gates, cmp_aug, ksel, vsel, kwin, vwin, tb, tc, wm, batch, seq):
    nq = seq // Q_BLOCK
    sp = seq + KV_PAD
    crow = cmp_aug.shape[3]
    n_sel = min(N_SELECT, seq // SEL_BLOCK)
    assert seq // SEL_BLOCK <= LANE and KV_PAD >= KV_TILE and KV_PAD >= WINDOW
    assert seq // CMP_STRIDE <= CMP_SPAN <= CMP_PAD and crow == CMP_PAD + seq // CMP_STRIDE
    assert seq // SEL_BLOCK <= N_SELECT or n_sel == LANE // 8
    n_gate = 3 * NSA_REP
    col = jnp.arange(n_gate * LANE) // LANE
    row = jnp.arange(2 * LANE) % LANE
    gate_spread = jnp.stack([(row[:, None] == g * n_gate + col[None, :]) for g in range(NSA_KV)]).astype(BF16)
    once = pl.Buffered(1)
    kv_spec = lambda w: pl.BlockSpec((None, NSA_KV, sp, w), lambda b, i: (b, 0, 0, 0), pipeline_mode=once)
    cmp_spec = lambda k, w: pl.BlockSpec((None, None, NSA_KV, crow, w), lambda b, i: (k, b, 0, 0, 0),
                                         pipeline_mode=once)
    in_specs = [
        pl.BlockSpec((None, NSA_HEADS, Q_BLOCK, LANE), lambda b, i: (b, 0, i, 0)),
        pl.BlockSpec((Q_BLOCK, 2 * LANE), lambda b, i: (b * nq + i, 0)),
        cmp_spec(0, LANE), cmp_spec(1, 2 * LANE),
        kv_spec(2 * LANE), kv_spec(2 * LANE), kv_spec(LANE), kv_spec(2 * LANE),
        _const_spec((NSA_HEADS, Q_BLOCK, 2 * Q_BLOCK)),
        _const_spec((NSA_HEADS, Q_BLOCK, LANE)),
        _const_spec((Q_BLOCK, LANE)),
        _const_spec((NSA_KV, 2 * LANE, n_gate * LANE)),
    ]
    rows = NSA_REP * Q_BLOCK
    group_scratch = [pltpu.VMEM((rows, 2 * LANE), BF16),
                     pltpu.VMEM((rows, KV_TILE), F32),
                     pltpu.VMEM((rows, KV_TILE), F32),
                     pltpu.VMEM((rows, LANE), F32),
                     pltpu.VMEM((rows, LANE), F32),
                     pltpu.VMEM((rows, LANE), F32),
                     pltpu.VMEM((rows, LANE), F32),
                     pltpu.VMEM((rows, 2 * LANE), F32)]
    assert len(group_scratch) == _N_GROUP_SCRATCH
    return pl.pallas_call(
        functools.partial(_nsa_kernel, n_sel=n_sel, select_all=seq // SEL_BLOCK <= N_SELECT),
        grid=(batch, nq),
        in_specs=in_specs,
        out_specs=pl.BlockSpec((Q_BLOCK, NSA_WIDTH), lambda b, i: (b * nq + i, 0)),
        out_shape=jax.ShapeDtypeStruct((batch * seq, NSA_WIDTH), F32),
        scratch_shapes=group_scratch * NSA_KV,
        compiler_params=_params(("arbitrary", "arbitrary")),
        name="nsa",
    )(q_aug, gates, cmp_aug, cmp_aug, ksel, vsel, kwin, vwin, tb, tc, wm, gate_spread)


def _memkv_kernel(mem_ref, g_ref, wk_ref, wv_ref, k_ref, v_ref):
    h = _rms(mem_ref[...], g_ref[...]).astype(BF16)
    k_ref[...] = _dot(h, wk_ref[...]).astype(BF16)
    v_ref[...] = _dot(h, wv_ref[...]).astype(BF16)


def _mem_kv(mem, g_mem, wkv):
    batch, m, _ = mem.shape
    blk = pl.BlockSpec((None, m, D_MODEL), lambda b: (b, 0, 0))
    return pl.pallas_call(
        _memkv_kernel,
        grid=(batch,),
        in_specs=[blk, _const_spec((1, D_MODEL)), _const_spec((D_MODEL, D_MODEL)),
                  _const_spec((D_MODEL, D_MODEL))],
        out_specs=[blk, blk],
        out_shape=[jax.ShapeDtypeStruct((batch, m, D_MODEL), BF16)] * 2,
        compiler_params=_params(("arbitrary",)),
        name="mem_kv",
    )(mem, g_mem.reshape(1, -1), wkv[:, :D_MODEL].astype(BF16), wkv[:, D_MODEL:].astype(BF16))


def _mixout_kernel(x_ref, ya_ref, yb_ref, gb_ref, wo_ref, post_ref, xpre_ref, wq_ref,
                   km_ref, vm_ref, wxo_ref, xpost_ref, o_ref):
    tm = x_ref.shape[0]

    def chain(rows):
        yb = _rms(yb_ref[rows, :], gb_ref[...]).astype(BF16)
        ya = ya_ref[rows, :].astype(BF16)
        yield
        y = _dot(ya, wo_ref[0:GM_WIDTH, :]) + _dot(yb, wo_ref[GM_WIDTH:, :])
        yield
        x1 = x_ref[rows, :] + _rms(y, post_ref[...])
        h = _rms(x1, xpre_ref[...]).astype(BF16)
        yield
        q = _dot(h, wq_ref[...]).astype(BF16)
        yield
        outs = []
        for hd in range(XA_HEADS):
            hs = slice(hd * XA_HEAD_DIM, (hd + 1) * XA_HEAD_DIM)
            s = _dot_nt(q[:, hs], km_ref[:, hs])
            yield
            e = _softmax_exp(s - jnp.max(s, axis=1, keepdims=True))
            den = jnp.sum(e, axis=1, keepdims=True)
            e = e.astype(BF16)
            yield
            outs.append((_dot(e, vm_ref[:, hs]) / den).astype(BF16))
        o = _dot(jnp.concatenate(outs, axis=1), wxo_ref[...])
        yield
        o_ref[rows, :] = x1 + _rms(o, xpost_ref[...])

    _interleave([chain(slice(k * tm // 2, (k + 1) * tm // 2)) for k in range(2)])


def _mix_out(x2d, ya, yb, batch, seq, gain_b, w_out, mix_post, xa_pre, wq, kmem, vmem, wo, xa_post):
    tm = 2 * TOKEN_TILE if seq % (2 * TOKEN_TILE) == 0 else TOKEN_TILE
    nt = seq // tm
    row = lambda w: pl.BlockSpec((tm, w), lambda b, t: (b * nt + t, 0))
    mem_spec = pl.BlockSpec((None, kmem.shape[1], D_MODEL), lambda b, t: (b, 0, 0))
    sq = _const_spec((D_MODEL, D_MODEL))
    vec = _const_spec((1, D_MODEL))
    return pl.pallas_call(
        _mixout_kernel,
        grid=(batch, nt),
        in_specs=[row(D_MODEL), row(GM_WIDTH), row(NSA_WIDTH), _const_spec((1, NSA_WIDTH)), sq, vec,
                  vec, sq, mem_spec, mem_spec, sq, vec],
        out_specs=row(D_MODEL),
        out_shape=jax.ShapeDtypeStruct(x2d.shape, F32),
        compiler_params=_params(("arbitrary", "arbitrary")),
        name="mix_out",
    )(x2d, ya, yb, gain_b.reshape(1, -1), w_out.astype(BF16), mix_post.reshape(1, -1),
      xa_pre.reshape(1, -1), (wq * (XA_HEAD_DIM ** -0.5 * LOG2E)).astype(BF16), kmem, vmem,
      wo.astype(BF16), xa_post.reshape(1, -1))


def kernel(x, mem, ffn1_pre, ffn1_post, ffn1_wg, ffn1_wu, ffn1_wd, mix_pre, mix_post, w_in, gm_ln_g, gm_ln_b, gm_ws, gm_bs, ck_pe, ck_w1, ck_b1, ck_w2, cv_pe, cv_w1, cv_b1, cv_w2, rel_bias, out_gain_a, out_gain_b, w_out, xa_pre, xa_post, mem_norm, xa_wq, xa_wkv, xa_wo, ffn2_pre, ffn2_post, ffn2_wg, ffn2_wu, ffn2_wd):
    batch, seq, _ = x.shape
    depth = ffn1_pre.shape[0]
    x2d = x.reshape(batch * seq, D_MODEL)
    tb, tc, wm = _bias_tables(rel_bias)
    for l in range(depth):
        x2d = _ffn(x2d, ffn1_pre[l], ffn1_wg[l], ffn1_wu[l], ffn1_wd[l], ffn1_post[l])
        ya, q_aug, cmp_raw, ksel, vsel, kwin, vwin, gates = _mixproj(
            x2d, batch, seq, mix_pre[l], w_in[l], gm_ln_g[l], gm_ln_b[l], gm_ws[l], gm_bs[l],
            out_gain_a[l])
        cmp_aug = _compress(cmp_raw, batch, seq,
                            jnp.stack([ck_pe[l], cv_pe[l]]), jnp.stack([ck_w1[l], cv_w1[l]]),
                            jnp.stack([ck_b1[l], cv_b1[l]]), jnp.stack([ck_w2[l], cv_w2[l]]))
        yb = _nsa(q_aug, gates, cmp_aug, ksel, vsel, kwin, vwin, tb, tc, wm, batch, seq)
        kmem, vmem = _mem_kv(mem, mem_norm[l], xa_wkv[l])
        x2d = _mix_out(x2d, ya, yb, batch, seq, out_gain_b[l], w_out[l], mix_post[l], xa_pre[l],
                       xa_wq[l], kmem, vmem, xa_wo[l], xa_post[l])
        x2d = _ffn(x2d, ffn2_pre[l], ffn2_wg[l], ffn2_wu[l], ffn2_wd[l], ffn2_post[l])
    return x2d.reshape(batch, seq, D_MODEL)
```

```python
import functools
import math

import jax
import jax.numpy as jnp
from jax import lax
from jax.experimental import pallas as pl
from jax.experimental.pallas import tpu as pltpu

F32 = jnp.float32
BF16 = jnp.bfloat16

D_MODEL = 1024
MEM_LEN = 256
EPS = 1e-6
D_FF = 2816
GM_WIDTH = 512
GM_GROUPS = 4
GM_CH = GM_WIDTH // GM_GROUPS
CHUNK = 128
NSA_WIDTH = D_MODEL - GM_WIDTH
HEAD_DIM = 64
NSA_HEADS = NSA_WIDTH // HEAD_DIM
NSA_KV = 2
NSA_REP = NSA_HEADS // NSA_KV
KV_W = NSA_KV * HEAD_DIM
CMP_BLOCK = 32
CMP_STRIDE = 16
CMP_HIDDEN = 256
SEL_BLOCK = 64
N_SELECT = 16
WINDOW = 512
Q_BLOCK = 128
N_BUCKETS = 32
MAX_DISTANCE = 128
XA_HEADS = 4
XA_HEAD_DIM = D_MODEL // XA_HEADS
FORCE_SCORE = 1e4

LANE = 128
SUBLANE = 8
TOKEN_TILE = 512
FF_CHUNK = 256
KV_PAD = 512
KV_TILE = 512
CMP_PAD = 512
CMP_SPAN = 512
MASK_BF16 = -(2.0 ** 100)
MASK_F32 = -1e30
MAX_CLAMP = -(2.0 ** 99)
LOG2E = math.log2(math.e)
_softmax_exp = jnp.exp2
VMEM_LIMIT = 56 * 1024 * 1024

_NT = (((1,), (1,)), ((), ()))


def _rms(x, g):
    ms = jnp.mean(x * x, axis=-1, keepdims=True)
    return x * lax.rsqrt(ms + EPS) * g


def _dot(a, b):
    return jnp.dot(a, b, preferred_element_type=F32)


def _dot_nt(a, b):
    return lax.dot_general(a, b, _NT, preferred_element_type=F32)


def _params(sem, vmem=VMEM_LIMIT):
    return pltpu.CompilerParams(dimension_semantics=sem, vmem_limit_bytes=vmem)


def _const_spec(shape):
    nd = len(shape)
    return pl.BlockSpec(shape, lambda *_: (0,) * nd)


def _interleave(chains, skew=1):
    pending = list(chains)
    for _ in range(skew):
        next(pending[0])
    while pending:
        for gen in list(pending):
            if next(gen, pending) is pending:
                pending.remove(gen)


def _ffn_kernel(x_ref, pre_ref, wg_ref, wu_ref, wd_ref, post_ref, o_ref, acc_ref):
    tm = x_ref.shape[0]

    def chain(rows):
        h = _rms(x_ref[rows, :], pre_ref[...]).astype(BF16)
        yield
        for c in range(D_FF // FF_CHUNK):
            sl = slice(c * FF_CHUNK, (c + 1) * FF_CHUNK)
            g = _dot(h, wg_ref[:, sl])
            u = _dot(h, wu_ref[:, sl])
            a = (jax.nn.silu(g) * u).astype(BF16)
            d = _dot(a, wd_ref[sl, :])
            if c == 0:
                acc_ref[rows, :] = d
            else:
                acc_ref[rows, :] += d
            yield
        o_ref[rows, :] = x_ref[rows, :] + 0.5 * _rms(acc_ref[rows, :], post_ref[...])

    _interleave([chain(slice(k * tm // 2, (k + 1) * tm // 2)) for k in range(2)])


def _ffn(x2d, pre, wg, wu, wd, post):
    n = x2d.shape[0]
    tm = 2 * TOKEN_TILE if n % (2 * TOKEN_TILE) == 0 else TOKEN_TILE
    assert n % tm == 0
    row = pl.BlockSpec((tm, D_MODEL), lambda t: (t, 0))
    once = pl.Buffered(1)
    weight = lambda shape: pl.BlockSpec(shape, lambda t: (0, 0), pipeline_mode=once)
    return pl.pallas_call(
        _ffn_kernel,
        grid=(n // tm,),
        in_specs=[row, _const_spec((1, D_MODEL)), weight((D_MODEL, D_FF)),
                  weight((D_MODEL, D_FF)), weight((D_FF, D_MODEL)),
                  _const_spec((1, D_MODEL))],
        out_specs=row,
        out_shape=jax.ShapeDtypeStruct((n, D_MODEL), F32),
        scratch_shapes=[pltpu.VMEM((tm, D_MODEL), F32)],
        compiler_params=_params(("arbitrary",)),
        name="ffn",
    )(x2d, pre.reshape(1, -1), wg.astype(BF16), wu.astype(BF16), wd.astype(BF16),
      post.reshape(1, -1))


def _mixproj_kernel(x_ref, pre_ref, win_ref, wgt_ref, lng_ref, lnb_ref, ws_ref, bs_ref, ga_ref,
                    ya_ref, q_ref, cmp_ref, ksel_ref, vsel_ref, kwin_ref, vwin_ref, gt_ref,
                    cmps_ref, w_ref, *, q_scale):
    t = pl.program_id(1)
    tm = x_ref.shape[0]
    lane = lax.broadcasted_iota(jnp.int32, (tm, LANE), 1)
    flag_col = jnp.where(lane == HEAD_DIM, MASK_BF16, 0.0).astype(BF16)

    @pl.when(t == 0)
    def _pad_rows():
        w_ref[...] = win_ref[0:_O_GATE, :].astype(BF16)
        for g in range(NSA_KV):
            ksel_ref[g, :, 0:LANE] = flag_col
            ksel_ref[g, :, LANE:2 * LANE] = jnp.zeros((tm, LANE), BF16)
            kwin_ref[g] = flag_col
            vsel_ref[g] = jnp.zeros((tm, 2 * LANE), BF16)
            vwin_ref[g] = jnp.zeros((tm, 2 * LANE), BF16)

    def chain(r0, n):
        rows = slice(r0, r0 + n)
        h = _rms(x_ref[rows, :], pre_ref[...]).astype(BF16)
        yield
        u = jax.nn.gelu(_dot_nt(h, w_ref[0:GM_WIDTH, :]))
        v = jax.nn.gelu(_dot_nt(h, w_ref[GM_WIDTH:2 * GM_WIDTH, :]))
        yield
        r_i = lax.broadcasted_iota(jnp.int32, (CHUNK, CHUNK), 0)
        c_i = lax.broadcasted_iota(jnp.int32, (CHUNK, CHUNK), 1)
        for g in range(GM_GROUPS):
            gs = slice(g * GM_CH, (g + 1) * GM_CH)
            vg = v[:, gs]
            mu = jnp.mean(vg, axis=-1, keepdims=True)
            var = jnp.mean(jnp.square(vg - mu), axis=-1, keepdims=True)
            vn = ((vg - mu) * lax.rsqrt(var + EPS) * lng_ref[:, gs] + lnb_ref[:, gs]).astype(BF16)
            w = jnp.where(c_i <= r_i, ws_ref[g], 0.0).astype(BF16)
            for c in range(n // CHUNK):
                rs = slice(c * CHUNK, (c + 1) * CHUNK)
                s = _dot(w, vn[rs]) + bs_ref[:, g:g + 1]
                ya_ref[r0 + c * CHUNK:r0 + (c + 1) * CHUNK, gs] = u[rs, gs] * s
        yield
        ya_ref[rows, :] = _rms(ya_ref[rows, :], ga_ref[...])
        lane = lax.broadcasted_iota(jnp.int32, (n, LANE), 1)
        lane_lo = lane < HEAD_DIM

        def halves(z):
            return z, pltpu.roll(z, HEAD_DIM, 1)

        one_col = jnp.where(lane == HEAD_DIM, 1.0, 0.0)
        zq = _dot_nt(h, w_ref[_O_Q:_O_C, :]) * q_scale
        yield
        for a in range(NSA_HEADS // 2):
            for hd, qh in zip((2 * a, 2 * a + 1), halves(zq[:, a * LANE:(a + 1) * LANE])):
                q_ref[hd, rows, :] = jnp.where(lane_lo, qh, one_col).astype(BF16)
        zc = _dot_nt(h, w_ref[_O_C:_O_KV, :])
        yield
        nr = n // CMP_STRIDE
        urows = slice(r0 // CMP_STRIDE, r0 // CMP_STRIDE + nr)
        lo_u = lax.broadcasted_iota(jnp.int32, (nr, LANE), 1) < HEAD_DIM
        for kv in range(2):
            cmps_ref[kv, rows, :] = zc[:, kv * KV_W:(kv + 1) * KV_W]
            for a in range(CMP_STRIDE // 2):
                za = cmps_ref[kv, pl.ds(r0 + 2 * a, nr, stride=CMP_STRIDE), :]
                zb = cmps_ref[kv, pl.ds(r0 + 2 * a + 1, nr, stride=CMP_STRIDE), :]
                cmp_ref[kv, urows, a * LANE:(a + 1) * LANE] = jnp.where(
                    lo_u, za, pltpu.roll(zb, HEAD_DIM, 1))
                cmp_ref[kv, urows, (CMP_STRIDE // 2 + a) * LANE:(CMP_STRIDE // 2 + a + 1) * LANE] = (
                    jnp.where(lo_u, pltpu.roll(za, HEAD_DIM, 1), zb))
        zkv = _dot_nt(h, w_ref[_O_KV:_O_GATE, :])
        yield
        k_sel, v_sel, k_win, v_win = (halves(zkv[:, j * LANE:(j + 1) * LANE]) for j in range(4))
        pos_blk = ((t - 1) * tm + r0 + lax.broadcasted_iota(jnp.int32, (n, LANE), 0)) // SEL_BLOCK
        blk_onehot = jnp.where(lane == pos_blk, MASK_BF16, 0.0).astype(BF16)

        def v_aug(v, g):
            return jnp.concatenate([jnp.where(lane_lo, v[g], 1.0), jnp.where(lane_lo, 1.0, v[1 - g])],
                                   axis=1).astype(BF16)

        for g in range(NSA_KV):
            ksel_ref[g, rows, 0:LANE] = jnp.where(lane_lo, k_sel[g], 0.0).astype(BF16)
            ksel_ref[g, rows, LANE:2 * LANE] = blk_onehot
            vsel_ref[g, rows, :] = v_aug(v_sel, g)
            kwin_ref[g, rows, :] = jnp.where(lane_lo, k_win[g], 0.0).astype(BF16)
            vwin_ref[g, rows, :] = v_aug(v_win, g)
        sg = jax.nn.sigmoid(_dot(h, wgt_ref[...]))
        hi = sg.astype(BF16)
        gt_ref[rows, 0:LANE] = hi
        gt_ref[rows, LANE:2 * LANE] = (sg - hi.astype(F32)).astype(BF16)

    @pl.when(t > 0)
    def _tile():
        _interleave([chain(k * tm // 2, tm // 2) for k in range(2)])


_O_Q = 2 * GM_WIDTH
_O_C = _O_Q + NSA_WIDTH
_O_KV = _O_C + 2 * KV_W
_O_GATE = _O_KV + 4 * KV_W


def _mixproj(x2d, batch, seq, mix_pre, w_in, ln_g, ln_b, ws, bs, gain_a):
    tm = TOKEN_TILE
    nt = seq // tm
    assert seq % tm == 0 and KV_PAD == tm
    q_scale = HEAD_DIM ** -0.5 * LOG2E
    n_gate = 3 * NSA_HEADS
    w_g = w_in[:, _O_GATE:].reshape(D_MODEL, NSA_KV, NSA_REP, 3).transpose(0, 1, 3, 2)
    w_g = jnp.pad(w_g.reshape(D_MODEL, n_gate), ((0, 0), (0, LANE - n_gate))).astype(BF16)

    def tile(b, t):
        return b * nt + jnp.maximum(t - 1, 0)

    n = batch * seq
    sp = seq + KV_PAD
    out_shape = [
        jax.ShapeDtypeStruct((n, GM_WIDTH), F32),
        jax.ShapeDtypeStruct((batch, NSA_HEADS, seq, LANE), BF16),
        jax.ShapeDtypeStruct((2, n // CMP_STRIDE, CMP_STRIDE * KV_W), F32),
        jax.ShapeDtypeStruct((batch, NSA_KV, sp, 2 * LANE), BF16),
        jax.ShapeDtypeStruct((batch, NSA_KV, sp, 2 * LANE), BF16),
        jax.ShapeDtypeStruct((batch, NSA_KV, sp, LANE), BF16),
        jax.ShapeDtypeStruct((batch, NSA_KV, sp, 2 * LANE), BF16),
        jax.ShapeDtypeStruct((n, 2 * LANE), BF16),
    ]
    padded = lambda w: pl.BlockSpec((None, NSA_KV, tm, w), lambda b, t: (b, 0, t, 0))
    out_specs = [
        pl.BlockSpec((tm, GM_WIDTH), lambda b, t: (tile(b, t), 0)),
        pl.BlockSpec((None, NSA_HEADS, tm, LANE), lambda b, t: (b, 0, jnp.maximum(t - 1, 0), 0)),
        pl.BlockSpec((2, tm // CMP_STRIDE, CMP_STRIDE * KV_W), lambda b, t: (0, tile(b, t), 0)),
        padded(2 * LANE), padded(2 * LANE), padded(LANE), padded(2 * LANE),
        pl.BlockSpec((tm, 2 * LANE), lambda b, t: (tile(b, t), 0)),
    ]
    in_specs = [
        pl.BlockSpec((tm, D_MODEL), lambda b, t: (tile(b, t), 0)),
        _const_spec((1, D_MODEL)),
        pl.BlockSpec(w_in.shape[::-1], lambda b, t: (0, 0), pipeline_mode=pl.Buffered(1)),
        _const_spec(w_g.shape),
        _const_spec((1, GM_WIDTH)), _const_spec((1, GM_WIDTH)),
        _const_spec((GM_GROUPS, CHUNK, CHUNK)), _const_spec((CHUNK, GM_GROUPS)),
        _const_spec((1, GM_WIDTH)),
    ]
    return pl.pallas_call(
        functools.partial(_mixproj_kernel, q_scale=q_scale),
        grid=(batch, nt + 1),
        in_specs=in_specs,
        out_specs=out_specs,
        out_shape=out_shape,
        scratch_shapes=[pltpu.VMEM((2, tm, KV_W), F32),
                        pltpu.VMEM((_O_GATE, D_MODEL), BF16)],
        compiler_params=_params(("arbitrary", "arbitrary")),
        name="mixproj",
    )(x2d, mix_pre.reshape(1, -1), w_in.T, w_g, ln_g.reshape(1, -1), ln_b.reshape(1, -1),
      ws, bs.T, gain_a.reshape(1, -1))


def _compress_kernel(raw_ref, pe_ref, w1_ref, b1_ref, w2_ref, o_ref):
    is_v = pl.program_id(0) == 1
    nu = raw_ref.shape[0]
    half = CMP_STRIDE * HEAD_DIM
    outs = []
    for g in range(NSA_KV):
        u = raw_ref[:, g * half:(g + 1) * half]
        a = _dot((u + pe_ref[0]).astype(BF16), w1_ref[0:half, :])
        b = _dot((u + pe_ref[1]).astype(BF16), w1_ref[half:2 * half, :])
        hid = jax.nn.gelu(a + pltpu.roll(b, nu - 1, 0) + b1_ref[...])
        outs.append(_dot(hid.astype(BF16), w2_ref[...]))
    lane = lax.broadcasted_iota(jnp.int32, (nu, LANE), 1)
    ci = lax.broadcasted_iota(jnp.int32, (nu, LANE), 0)
    u_c = CMP_BLOCK // CMP_STRIDE
    u_s = SEL_BLOCK // CMP_STRIDE
    overlap = jnp.maximum(jnp.minimum(ci + u_c, u_s * (lane + 1)) - jnp.maximum(ci, u_s * lane), 0)
    extra = jnp.where(is_v, overlap.astype(F32), 0.0)
    ones_hi = jnp.where(jnp.logical_and(is_v, lane >= HEAD_DIM), 1.0, 0.0)
    lane_p = lax.broadcasted_iota(jnp.int32, (CMP_PAD, LANE), 1)
    pad_lo = jnp.where(jnp.logical_and(jnp.logical_not(is_v), lane_p == HEAD_DIM), MASK_BF16, 0.0)
    for g in range(NSA_KV):
        o_ref[g, 0:CMP_PAD, 0:LANE] = pad_lo
        o_ref[g, 0:CMP_PAD, LANE:2 * LANE] = jnp.zeros((CMP_PAD, LANE), F32)
        o_ref[g, CMP_PAD:CMP_PAD + nu, 0:LANE] = outs[g] + ones_hi
        o_ref[g, CMP_PAD:CMP_PAD + nu, LANE:2 * LANE] = extra


def _compress(cmp_raw, batch, seq, pe, w1, b1, w2):
    nu = seq // CMP_STRIDE
    half = CMP_STRIDE * HEAD_DIM
    kd = NSA_KV * half
    raw = cmp_raw.reshape(2, batch, nu, kd)
    w2p = jnp.pad(w2, ((0, 0), (0, 0), (0, LANE - HEAD_DIM)))
    rows = CMP_PAD + nu
    return pl.pallas_call(
        _compress_kernel,
        grid=(2, batch),
        in_specs=[
            pl.BlockSpec((None, None, nu, kd), lambda k, b: (k, b, 0, 0)),
            pl.BlockSpec((None, 2, 1, half), lambda k, b: (k, 0, 0, 0)),
            pl.BlockSpec((None, 2 * half, CMP_HIDDEN), lambda k, b: (k, 0, 0)),
            pl.BlockSpec((None, 1, CMP_HIDDEN), lambda k, b: (k, 0, 0)),
            pl.BlockSpec((None, CMP_HIDDEN, LANE), lambda k, b: (k, 0, 0)),
        ],
        out_specs=pl.BlockSpec((None, None, NSA_KV, rows, 2 * LANE), lambda k, b: (k, b, 0, 0, 0)),
        out_shape=jax.ShapeDtypeStruct((2, batch, NSA_KV, rows, 2 * LANE), F32),
        compiler_params=_params(("arbitrary", "arbitrary")),
        name="compress",
    )(raw, pe.reshape(2, 2, 1, half), w1.astype(BF16), b1.reshape(2, 1, CMP_HIDDEN), w2p.astype(BF16))


def _t5_bucket(dist):
    n = jnp.maximum(dist, 0)
    max_exact = N_BUCKETS // 2
    nf = jnp.maximum(n, max_exact).astype(F32)
    large = max_exact + jnp.floor(jnp.log(nf * (1.0 / max_exact)) / math.log(MAX_DISTANCE / max_exact)
                                  * (N_BUCKETS - max_exact)).astype(jnp.int32)
    large = jnp.minimum(large, N_BUCKETS - 1)
    return jnp.where(n < max_exact, n, large)


def _tables_kernel(rb_ref, tb_ref, tc_ref, wm_ref):
    def fill(out_ref, dist):
        bk = _t5_bucket(dist)
        for h in range(NSA_HEADS):
            far = rb_ref[N_BUCKETS - 1, h]
            val = jnp.zeros(dist.shape, F32)
            for b in range(N_BUCKETS - 1):
                val = jnp.where(bk == b, (rb_ref[b, h] - far) * LOG2E, val)
            out_ref[h] = jnp.where(dist >= 0, val, MASK_F32)

    tq = lax.broadcasted_iota(jnp.int32, (Q_BLOCK, 2 * Q_BLOCK), 0)
    j = lax.broadcasted_iota(jnp.int32, (Q_BLOCK, 2 * Q_BLOCK), 1)
    fill(tb_ref, tq + Q_BLOCK - j)
    tq = lax.broadcasted_iota(jnp.int32, (Q_BLOCK, LANE), 0)
    c = lax.broadcasted_iota(jnp.int32, (Q_BLOCK, LANE), 1)
    cmp_first = Q_BLOCK // CMP_STRIDE - LANE
    fill(tc_ref, tq - CMP_STRIDE * (cmp_first + c) - (CMP_BLOCK - 1))
    wm_ref[...] = jnp.where(c > tq, 0.0, MASK_F32)


def _bias_tables(rel_bias):
    return pl.pallas_call(
        _tables_kernel,
        in_specs=[pl.BlockSpec(memory_space=pltpu.SMEM)],
        out_shape=[jax.ShapeDtypeStruct((NSA_HEADS, Q_BLOCK, 2 * Q_BLOCK), F32),
                   jax.ShapeDtypeStruct((NSA_HEADS, Q_BLOCK, LANE), F32),
                   jax.ShapeDtypeStruct((Q_BLOCK, LANE), F32)],
        name="bias_tables",
    )(rel_bias)


_N_GROUP_SCRATCH = 8


def _oddeven_merge_sort(n):
    def merge(lo, hi, r):
        step = 2 * r
        if step < hi - lo:
            yield from merge(lo, hi, step)
            yield from merge(lo + r, hi, step)
            yield from ((k, k + r) for k in range(lo + r, hi - r, step))
        else:
            yield (lo, lo + r)

    def sort(lo, hi):
        if hi > lo:
            mid = lo + (hi - lo) // 2
            yield from sort(lo, mid)
            yield from sort(mid + 1, hi)
            yield from merge(lo, hi, 1)

    return tuple(sort(0, n - 1))


def _bitonic_merge(n):
    pairs, k = [], n // 2
    while k >= 1:
        pairs += [(a, a + k) for a in range(n) if not a & k]
        k //= 2
    return tuple(pairs)


_SORT16 = _oddeven_merge_sort(N_SELECT)
_BITONIC16 = _bitonic_merge(N_SELECT)


def _nsa_kernel(q_ref, gt_ref, kc_ref, vc_ref, ks_ref, vs_ref, kw_ref, vw_ref,
                tb_ref, tc_ref, wm_ref, gx_ref, o_ref, *scratch, n_sel, select_all):
    i = pl.program_id(1)
    rows = NSA_REP * Q_BLOCK
    per_group = [scratch[g * _N_GROUP_SCRATCH:(g + 1) * _N_GROUP_SCRATCH] for g in range(NSA_KV)]
    qsel_ref = [s[0] for s in per_group]
    s_ref = [(s[1], s[2]) for s in per_group]
    m_ref = [(s[3], s[4]) for s in per_group]
    al_ref = [(s[5], s[6]) for s in per_group]
    acc_ref = [s[7] for s in per_group]

    zero_tb = jnp.zeros((rows, KV_TILE - 2 * Q_BLOCK), F32)
    cmp_per_q = Q_BLOCK // CMP_STRIDE
    c0 = pl.multiple_of(cmp_per_q * (i + 1) + CMP_PAD - CMP_SPAN, cmp_per_q)
    t0 = pl.multiple_of(Q_BLOCK * (i + 1) + KV_PAD - KV_TILE, Q_BLOCK)
    w1 = pl.multiple_of(t0 - Q_BLOCK, Q_BLOCK)
    n_tiles = (i + 1 + KV_TILE // Q_BLOCK - 1) // (KV_TILE // Q_BLOCK)
    lane = lax.broadcasted_iota(jnp.int32, (Q_BLOCK, LANE), 1)

    def bcast(col):
        return jnp.broadcast_to(col, (rows, LANE))

    def lanes(x, n):
        return jnp.concatenate([x] * n, axis=1)

    def head_out(acc, r):
        rs = slice(r * Q_BLOCK, (r + 1) * Q_BLOCK)
        a, b = acc[rs, 0:LANE], acc[rs, LANE:2 * LANE]
        return a * (1.0 / b) if r % 2 == 0 else b * (1.0 / a)

    def heads_of(g):
        return slice(g * NSA_REP, (g + 1) * NSA_REP)

    def q_of(g):
        return q_ref[heads_of(g)].reshape(rows, LANE)

    def tile0_bias(g):
        return jnp.concatenate([zero_tb, tb_ref[heads_of(g)].reshape(rows, 2 * Q_BLOCK)], axis=1)

    fronts = [None] * NSA_KV

    def front(g):
        hs = heads_of(g)
        qa = q_of(g)
        s = _dot_nt(qa, kc_ref[g, pl.ds(c0, CMP_SPAN), :].astype(BF16))
        s = s + jnp.concatenate([jnp.zeros((rows, CMP_SPAN - LANE), F32),
                                 tc_ref[hs].reshape(rows, LANE)], axis=1)
        yield
        m = jnp.maximum(jnp.max(s, axis=1, keepdims=True), MAX_CLAMP)
        p = _softmax_exp(s - m).astype(BF16)
        yield
        acc = _dot(p, vc_ref[g, pl.ds(c0, CMP_SPAN), :].astype(BF16))
        num_den = acc[:, 0:LANE]
        lane_r = lax.broadcasted_iota(jnp.int32, (rows, LANE), 1)
        den = jnp.where(lane_r < HEAD_DIM, pltpu.roll(num_den, HEAD_DIM, 1), num_den)
        inv = 1.0 / jnp.maximum(den, 1e-30)
        oc = num_den * inv
        impu = acc[:, LANE:2 * LANE] * inv
        imp = impu[0:Q_BLOCK]
        for r in range(1, NSA_REP):
            imp = imp + impu[r * Q_BLOCK:(r + 1) * Q_BLOCK]
        yield
        s0 = _dot_nt(qa, kw_ref[g, pl.ds(t0, KV_TILE), :]) + tile0_bias(g)
        s1 = _dot_nt(qa, kw_ref[g, pl.ds(w1, Q_BLOCK), :]) + jnp.tile(wm_ref[...], (NSA_REP, 1))
        yield
        m = jnp.maximum(jnp.max(s0, axis=1, keepdims=True), jnp.max(s1, axis=1, keepdims=True))
        p0 = _softmax_exp(s0 - m).astype(BF16)
        p1 = _softmax_exp(s1 - m).astype(BF16)
        yield
        accw = (_dot(p0, vw_ref[g, pl.ds(t0, KV_TILE), :]) + _dot(p1, vw_ref[g, pl.ds(w1, Q_BLOCK), :]))
        gx = _dot(gt_ref[...], gx_ref[g])

        def gate(branch, r):
            c = branch * NSA_REP + r
            return gx[:, c * LANE:(c + 1) * LANE]

        partial = []
        for r in range(NSA_REP):
            rs = slice(r * Q_BLOCK, (r + 1) * Q_BLOCK)
            oc_r = oc[rs] if r % 2 == 0 else pltpu.roll(oc[rs], HEAD_DIM, 1)
            partial.append(gate(0, r) * oc_r + gate(2, r) * head_out(accw, r))
        fronts[g] = (partial, [gate(1, r) for r in range(NSA_REP)])
        yield
        notsel_q = not_selected(imp).astype(BF16)
        qsel_ref[g][:, 0:LANE] = qa
        for r in range(NSA_REP):
            qsel_ref[g][r * Q_BLOCK:(r + 1) * Q_BLOCK, LANE:2 * LANE] = notsel_q
        yield
        s = _dot_nt(qsel_ref[g][...], ks_ref[g, pl.ds(t0, KV_TILE), :]) + tile0_bias(g)
        yield
        s_ref[g][0][...] = s
        m_ref[g][0][...] = bcast(jnp.max(s, axis=1, keepdims=True))
        al_ref[g][0][...] = jnp.ones((rows, LANE), F32)
        acc_ref[g][...] = jnp.zeros((rows, 2 * LANE), F32)

    def not_selected(imp):
        jb = lax.broadcasted_iota(jnp.int32, (LANE, Q_BLOCK), 0)
        t = i * Q_BLOCK + lax.broadcasted_iota(jnp.int32, (LANE, Q_BLOCK), 1)
        cur = t // SEL_BLOCK
        forced = (jb == 0) | (jb == cur) | (jb == cur - 1)
        score = jnp.where(forced, FORCE_SCORE, jnp.where(jb * SEL_BLOCK <= t, imp.T, -1.0))
        if select_all:
            return jnp.zeros((Q_BLOCK, LANE), F32)
        x = [score[SUBLANE * v:SUBLANE * (v + 1), :] for v in range(LANE // SUBLANE)]
        y = list(x)
        for a, b in _SORT16:
            y[a], y[b] = jnp.maximum(y[a], y[b]), jnp.minimum(y[a], y[b])
        for shift in (4, 2, 1):
            z = [jnp.maximum(y[k], pltpu.roll(y[n_sel - 1 - k], shift, 0)) for k in range(n_sel)]
            if shift > 1:
                for a, b in _BITONIC16:
                    z[a], z[b] = jnp.maximum(z[a], z[b]), jnp.minimum(z[a], z[b])
            y = z
        while len(y) > 1:
            y = [jnp.minimum(y[2 * k], y[2 * k + 1]) for k in range(len(y) // 2)]
        tau = y[0]
        gt = jnp.concatenate([jnp.where(xv > tau, 1.0, 0.0) for xv in x], axis=0)
        eq = jnp.concatenate([jnp.where(xv == tau, 1.0, 0.0) for xv in x], axis=0)
        kb = lax.broadcasted_iota(jnp.int32, (LANE, LANE), 1)
        n_above = _dot(jnp.ones((LANE, LANE), BF16), gt.astype(BF16))
        n_tied_before = _dot(jnp.where(kb < jb, 1.0, 0.0).astype(BF16), eq.astype(BF16))
        keep = (gt > 0.0) | ((eq > 0.0) & (n_above + n_tied_before < n_sel))
        return jnp.where(keep, 0.0, 1.0).T

    def qk_stage(g, k, prev, slot):
        st = pl.multiple_of(t0 - KV_TILE * k, Q_BLOCK)
        s_n = _dot_nt(qsel_ref[g][...], ks_ref[g, pl.ds(st, KV_TILE), :])
        s_ref[g][slot][...] = s_n
        m_c = m_ref[g][prev][...]
        m_n = jnp.maximum(m_c, bcast(jnp.max(s_n, axis=1, keepdims=True)))
        m_ref[g][slot][...] = m_n
        al_ref[g][slot][...] = _softmax_exp(m_c - m_n)

    def pv_stage(g, k, slot):
        st = pl.multiple_of(t0 - KV_TILE * k, Q_BLOCK)
        p = _softmax_exp(s_ref[g][slot][...] - lanes(m_ref[g][slot][...], KV_TILE // LANE))
        acc_ref[g][...] = (lanes(al_ref[g][slot][...], 2) * acc_ref[g][...]
                           + _dot(p.astype(BF16), vs_ref[g, pl.ds(st, KV_TILE), :]))

    groups = range(NSA_KV)
    _interleave([front(g) for g in groups])

    def tile_pair(j):
        for g in groups:
            qk_stage(g, 2 * j + 1, 0, 1)
        for g in groups:
            pv_stage(g, 2 * j, 0)
        for g in groups:
            qk_stage(g, 2 * j + 2, 1, 0)
        for g in groups:
            pv_stage(g, 2 * j + 1, 1)

    def two_pairs(j, carry):
        tile_pair(2 * j)
        tile_pair(2 * j + 1)
        return carry

    def one_pair(j, carry):
        tile_pair(j)
        return carry

    n_pairs = (n_tiles - 1) // 2
    lax.fori_loop(0, n_pairs // 2, two_pairs, 0)
    lax.fori_loop(2 * (n_pairs // 2), n_pairs, one_pair, 0)
    odd_left = n_tiles - 1 - 2 * n_pairs

    @pl.when(odd_left == 1)
    def _two_left():
        for g in groups:
            pv_stage(g, n_tiles - 2, 0)
        for g in groups:
            qk_stage(g, n_tiles - 1, 0, 0)

    for g in groups:
        pv_stage(g, n_tiles - 1, 0)
        partial, sel_gate = fronts[g]
        acc = acc_ref[g][...]
        heads = [partial[r] + sel_gate[r] * head_out(acc, r) for r in range(NSA_REP)]
        for pr in range(NSA_REP // 2):
            col = (g * NSA_REP // 2 + pr) * LANE
            o_ref[:, col:col + LANE] = jnp.where(lane < HEAD_DIM, heads[2 * pr], heads[2 * pr + 1])


def _nsa(q_aug, gates, cmp_aug, ksel, vsel, kwin, vwin, tb, tc, wm, batch, seq):
    nq = seq // Q_BLOCK
    sp = seq + KV_PAD
    crow = cmp_aug.shape[3]
    n_sel = min(N_SELECT, seq // SEL_BLOCK)
    assert seq // SEL_BLOCK <= LANE and KV_PAD >= KV_TILE and KV_PAD >= WINDOW
    assert seq // CMP_STRIDE <= CMP_SPAN <= CMP_PAD and crow == CMP_PAD + seq // CMP_STRIDE
    assert seq // SEL_BLOCK <= N_SELECT or n_sel == LANE // 8
    n_gate = 3 * NSA_REP
    col = jnp.arange(n_gate * LANE) // LANE
    row = jnp.arange(2 * LANE) % LANE
    gate_spread = jnp.stack([(row[:, None] == g * n_gate + col[None, :]) for g in range(NSA_KV)]).astype(BF16)
    once = pl.Buffered(1)
    kv_spec = lambda w: pl.BlockSpec((None, NSA_KV, sp, w), lambda b, i: (b, 0, 0, 0), pipeline_mode=once)
    cmp_spec = lambda k, w: pl.BlockSpec((None, None, NSA_KV, crow, w), lambda b, i: (k, b, 0, 0, 0),
                                         pipeline_mode=once)
    in_specs = [
        pl.BlockSpec((None, NSA_HEADS, Q_BLOCK, LANE), lambda b, i: (b, 0, i, 0)),
        pl.BlockSpec((Q_BLOCK, 2 * LANE), lambda b, i: (b * nq + i, 0)),
        cmp_spec(0, LANE), cmp_spec(1, 2 * LANE),
        kv_spec(2 * LANE), kv_spec(2 * LANE), kv_spec(LANE), kv_spec(2 * LANE),
        _const_spec((NSA_HEADS, Q_BLOCK, 2 * Q_BLOCK)),
        _const_spec((NSA_HEADS, Q_BLOCK, LANE)),
        _const_spec((Q_BLOCK, LANE)),
        _const_spec((NSA_KV, 2 * LANE, n_gate * LANE)),
    ]
    rows = NSA_REP * Q_BLOCK
    group_scratch = [pltpu.VMEM((rows, 2 * LANE), BF16),
                     pltpu.VMEM((rows, KV_TILE), F32),
                     pltpu.VMEM((rows, KV_TILE), F32),
                     pltpu.VMEM((rows, LANE), F32),
                     pltpu.VMEM((rows, LANE), F32),
                     pltpu.VMEM((rows, LANE), F32),
                     pltpu.VMEM((rows, LANE), F32),
                     pltpu.VMEM((rows, 2 * LANE), F32)]
    assert len(group_scratch) == _N_GROUP_SCRATCH
    return pl.pallas_call(
        functools.partial(_nsa_kernel, n_sel=n_sel, select_all=seq // SEL_BLOCK <= N_SELECT),
        grid=(batch, nq),
        in_specs=in_specs,
        out_specs=pl.BlockSpec((Q_BLOCK, NSA_WIDTH), lambda b, i: (b * nq + i, 0)),
        out_shape=jax.ShapeDtypeStruct((batch * seq, NSA_WIDTH), F32),
        scratch_shapes=group_scratch * NSA_KV,
        compiler_params=_params(("arbitrary", "arbitrary")),
        name="nsa",
    )(q_aug, gates, cmp_aug, cmp_aug, ksel, vsel, kwin, vwin, tb, tc, wm, gate_spread)


def _memkv_kernel(mem_ref, g_ref, wk_ref, wv_ref, k_ref, v_ref):
    h = _rms(mem_ref[...], g_ref[...]).astype(BF16)
    k_ref[...] = _dot(h, wk_ref[...]).astype(BF16)
    v_ref[...] = _dot(h, wv_ref[...]).astype(BF16)


def _mem_kv(mem, g_mem, wkv):
    batch, m, _ = mem.shape
    blk = pl.BlockSpec((None, m, D_MODEL), lambda b: (b, 0, 0))
    return pl.pallas_call(
        _memkv_kernel,
        grid=(batch,),
        in_specs=[blk, _const_spec((1, D_MODEL)), _const_spec((D_MODEL, D_MODEL)),
                  _const_spec((D_MODEL, D_MODEL))],
        out_specs=[blk, blk],
        out_shape=[jax.ShapeDtypeStruct((batch, m, D_MODEL), BF16)] * 2,
        compiler_params=_params(("arbitrary",)),
        name="mem_kv",
    )(mem, g_mem.reshape(1, -1), wkv[:, :D_MODEL].astype(BF16), wkv[:, D_MODEL:].astype(BF16))


def _mixout_kernel(x_ref, ya_ref, yb_ref, gb_ref, wo_ref, post_ref, xpre_ref, wq_ref,
                   km_ref, vm_ref, wxo_ref, xpost_ref, o_ref):
    tm = x_ref.shape[0]

    def chain(rows):
        yb = _rms(yb_ref[rows, :], gb_ref[...]).astype(BF16)
        ya = ya_ref[rows, :].astype(BF16)
        yield
        y = _dot(ya, wo_ref[0:GM_WIDTH, :]) + _dot(yb, wo_ref[GM_WIDTH:, :])
        yield
        x1 = x_ref[rows, :] + _rms(y, post_ref[...])
        h = _rms(x1, xpre_ref[...]).astype(BF16)
        yield
        q = _dot(h, wq_ref[...]).astype(BF16)
        yield
        outs = []
        for hd in range(XA_HEADS):
            hs = slice(hd * XA_HEAD_DIM, (hd + 1) * XA_HEAD_DIM)
            s = _dot_nt(q[:, hs], km_ref[:, hs])
            yield
            e = _softmax_exp(s - jnp.max(s, axis=1, keepdims=True))
            den = jnp.sum(e, axis=1, keepdims=True)
            e = e.astype(BF16)
            yield
            outs.append((_dot(e, vm_ref[:, hs]) / den).astype(BF16))
        o = _dot(jnp.concatenate(outs, axis=1), wxo_ref[...])
        yield
        o_ref[rows, :] = x1 + _rms(o, xpost_ref[...])

    _interleave([chain(slice(k * tm // 2, (k + 1) * tm // 2)) for k in range(2)])


def _mix_out(x2d, ya, yb, batch, seq, gain_b, w_out, mix_post, xa_pre, wq, kmem, vmem, wo, xa_post):
    tm = 2 * TOKEN_TILE if seq % (2 * TOKEN_TILE) == 0 else TOKEN_TILE
    nt = seq // tm
    row = lambda w: pl.BlockSpec((tm, w), lambda b, t: (b * nt + t, 0))
    mem_spec = pl.BlockSpec((None, kmem.shape[1], D_MODEL), lambda b, t: (b, 0, 0))
    sq = _const_spec((D_MODEL, D_MODEL))
    vec = _const_spec((1, D_MODEL))
    return pl.pallas_call(
        _mixout_kernel,
        grid=(batch, nt),
        in_specs=[row(D_MODEL), row(GM_WIDTH), row(NSA_WIDTH), _const_spec((1, NSA_WIDTH)), sq, vec,
                  vec, sq, mem_spec, mem_spec, sq, vec],
        out_specs=row(D_MODEL),
        out_shape=jax.ShapeDtypeStruct(x2d.shape, F32),
        compiler_params=_params(("arbitrary", "arbitrary")),
        name="mix_out",
    )(x2d, ya, yb, gain_b.reshape(1, -1), w_out.astype(BF16), mix_post.reshape(1, -1),
      xa_pre.reshape(1, -1), (wq * (XA_HEAD_DIM ** -0.5 * LOG2E)).astype(BF16), kmem, vmem,
      wo.astype(BF16), xa_post.reshape(1, -1))


def kernel(x, mem, ffn1_pre, ffn1_post, ffn1_wg, ffn1_wu, ffn1_wd, mix_pre, mix_post, w_in, gm_ln_g, gm_ln_b, gm_ws, gm_bs, ck_pe, ck_w1, ck_b1, ck_w2, cv_pe, cv_w1, cv_b1, cv_w2, rel_bias, out_gain_a, out_gain_b, w_out, xa_pre, xa_post, mem_norm, xa_wq, xa_wkv, xa_wo, ffn2_pre, ffn2_post, ffn2_wg, ffn2_wu, ffn2_wd):
    batch, seq, _ = x.shape
    depth = ffn1_pre.shape[0]
    x2d = x.reshape(batch * seq, D_MODEL)
    tb, tc, wm = _bias_tables(rel_bias)
    for l in range(depth):
        x2d = _ffn(x2d, ffn1_pre[l], ffn1_wg[l], ffn1_wu[l], ffn1_wd[l], ffn1_post[l])
        ya, q_aug, cmp_raw, ksel, vsel, kwin, vwin, gates = _mixproj(
            x2d, batch, seq, mix_pre[l], w_in[l], gm_ln_g[l], gm_ln_b[l], gm_ws[l], gm_bs[l],
            out_gain_a[l])
        cmp_aug = _compress(cmp_raw, batch, seq,
                            jnp.stack([ck_pe[l], cv_pe[l]]), jnp.stack([ck_w1[l], cv_w1[l]]),
                            jnp.stack([ck_b1[l], cv_b1[l]]), jnp.stack([ck_w2[l], cv_w2[l]]))
        yb = _nsa(q_aug, gates, cmp_aug, ksel, vsel, kwin, vwin, tb, tc, wm, batch, seq)
        kmem, vmem = _mem_kv(mem, mem_norm[l], xa_wkv[l])
        x2d = _mix_out(x2d, ya, yb, batch, seq, out_gain_b[l], w_out[l], mix_post[l], xa_pre[l],
                       xa_wq[l], kmem, vmem, xa_wo[l], xa_post[l])
        x2d = _ffn(x2d, ffn2_pre[l], ffn2_wg[l], ffn2_wu[l], ffn2_wd[l], ffn2_post[l])
    return x2d.reshape(batch, seq, D_MODEL)
```

```python
import functools
import math

import jax
import jax.numpy as jnp
from jax import lax
from jax.experimental import pallas as pl
from jax.experimental.pallas import tpu as pltpu

F32 = jnp.float32
BF16 = jnp.bfloat16

D_MODEL = 1024
MEM_LEN = 256
EPS = 1e-6
D_FF = 2816
GM_WIDTH = 512
GM_GROUPS = 4
GM_CH = GM_WIDTH // GM_GROUPS
CHUNK = 128
NSA_WIDTH = D_MODEL - GM_WIDTH
HEAD_DIM = 64
NSA_HEADS = NSA_WIDTH // HEAD_DIM
NSA_KV = 2
NSA_REP = NSA_HEADS // NSA_KV
KV_W = NSA_KV * HEAD_DIM
CMP_BLOCK = 32
CMP_STRIDE = 16
CMP_HIDDEN = 256
SEL_BLOCK = 64
N_SELECT = 16
WINDOW = 512
Q_BLOCK = 128
N_BUCKETS = 32
MAX_DISTANCE = 128
XA_HEADS = 4
XA_HEAD_DIM = D_MODEL // XA_HEADS
FORCE_SCORE = 1e4

LANE = 128
SUBLANE = 8
TOKEN_TILE = 512
FF_CHUNK = 256
KV_PAD = 512
KV_TILE = 512
CMP_PAD = 512
CMP_SPAN = 512
MASK_BF16 = -(2.0 ** 100)
MASK_F32 = -1e30
MAX_CLAMP = -(2.0 ** 99)
LOG2E = math.log2(math.e)
_softmax_exp = jnp.exp2
VMEM_LIMIT = 56 * 1024 * 1024

_NT = (((1,), (1,)), ((), ()))


def _rms(x, g):
    ms = jnp.mean(x * x, axis=-1, keepdims=True)
    return x * lax.rsqrt(ms + EPS) * g


def _dot(a, b):
    return jnp.dot(a, b, preferred_element_type=F32)


def _dot_nt(a, b):
    return lax.dot_general(a, b, _NT, preferred_element_type=F32)


def _params(sem, vmem=VMEM_LIMIT):
    return pltpu.CompilerParams(dimension_semantics=sem, vmem_limit_bytes=vmem)


def _const_spec(shape):
    nd = len(shape)
    return pl.BlockSpec(shape, lambda *_: (0,) * nd)


def _interleave(chains, skew=1):
    pending = list(chains)
    for _ in range(skew):
        next(pending[0])
    while pending:
        for gen in list(pending):
            if next(gen, pending) is pending:
                pending.remove(gen)


def _ffn_kernel(x_ref, pre_ref, wg_ref, wu_ref, wd_ref, post_ref, o_ref, acc_ref):
    tm = x_ref.shape[0]

    def chain(rows):
        h = _rms(x_ref[rows, :], pre_ref[...]).astype(BF16)
        yield
        for c in range(D_FF // FF_CHUNK):
            sl = slice(c * FF_CHUNK, (c + 1) * FF_CHUNK)
            g = _dot(h, wg_ref[:, sl])
            u = _dot(h, wu_ref[:, sl])
            a = (jax.nn.silu(g) * u).astype(BF16)
            d = _dot(a, wd_ref[sl, :])
            if c == 0:
                acc_ref[rows, :] = d
            else:
                acc_ref[rows, :] += d
            yield
        o_ref[rows, :] = x_ref[rows, :] + 0.5 * _rms(acc_ref[rows, :], post_ref[...])

    _interleave([chain(slice(k * tm // 2, (k + 1) * tm // 2)) for k in range(2)])


def _ffn(x2d, pre, wg, wu, wd, post):
    n = x2d.shape[0]
    tm = 2 * TOKEN_TILE if n % (2 * TOKEN_TILE) == 0 else TOKEN_TILE
    assert n % tm == 0
    row = pl.BlockSpec((tm, D_MODEL), lambda t: (t, 0))
    once = pl.Buffered(1)
    weight = lambda shape: pl.BlockSpec(shape, lambda t: (0, 0), pipeline_mode=once)
    return pl.pallas_call(
        _ffn_kernel,
        grid=(n // tm,),
        in_specs=[row, _const_spec((1, D_MODEL)), weight((D_MODEL, D_FF)),
                  weight((D_MODEL, D_FF)), weight((D_FF, D_MODEL)),
                  _const_spec((1, D_MODEL))],
        out_specs=row,
        out_shape=jax.ShapeDtypeStruct((n, D_MODEL), F32),
        scratch_shapes=[pltpu.VMEM((tm, D_MODEL), F32)],
        compiler_params=_params(("arbitrary",)),
        name="ffn",
    )(x2d, pre.reshape(1, -1), wg.astype(BF16), wu.astype(BF16), wd.astype(BF16),
      post.reshape(1, -1))


def _mixproj_kernel(x_ref, pre_ref, win_ref, wgt_ref, lng_ref, lnb_ref, ws_ref, bs_ref, ga_ref,
                    ya_ref, q_ref, cmp_ref, ksel_ref, vsel_ref, kwin_ref, vwin_ref, gt_ref,
                    cmps_ref, w_ref, *, q_scale):
    t = pl.program_id(1)
    tm = x_ref.shape[0]
    lane = lax.broadcasted_iota(jnp.int32, (tm, LANE), 1)
    flag_col = jnp.where(lane == HEAD_DIM, MASK_BF16, 0.0).astype(BF16)

    @pl.when(t == 0)
    def _pad_rows():
        w_ref[...] = win_ref[:, 0:_O_GATE].astype(BF16)
        for g in range(NSA_KV):
            ksel_ref[g, :, 0:LANE] = flag_col
            ksel_ref[g, :, LANE:2 * LANE] = jnp.zeros((tm, LANE), BF16)
            kwin_ref[g] = flag_col
            vsel_ref[g] = jnp.zeros((tm, 2 * LANE), BF16)
            vwin_ref[g] = jnp.zeros((tm, 2 * LANE), BF16)

    def chain(r0, n):
        rows = slice(r0, r0 + n)
        h = _rms(x_ref[rows, :], pre_ref[...]).astype(BF16)
        yield
        u = jax.nn.gelu(_dot(h, w_ref[:, 0:GM_WIDTH]))
        v = jax.nn.gelu(_dot(h, w_ref[:, GM_WIDTH:2 * GM_WIDTH]))
        yield
        r_i = lax.broadcasted_iota(jnp.int32, (CHUNK, CHUNK), 0)
        c_i = lax.broadcasted_iota(jnp.int32, (CHUNK, CHUNK), 1)
        for g in range(GM_GROUPS):
            gs = slice(g * GM_CH, (g + 1) * GM_CH)
            vg = v[:, gs]
            mu = jnp.mean(vg, axis=-1, keepdims=True)
            var = jnp.mean(jnp.square(vg - mu), axis=-1, keepdims=True)
            vn = ((vg - mu) * lax.rsqrt(var + EPS) * lng_ref[:, gs] + lnb_ref[:, gs]).astype(BF16)
            w = jnp.where(c_i <= r_i, ws_ref[g], 0.0).astype(BF16)
            for c in range(n // CHUNK):
                rs = slice(c * CHUNK, (c + 1) * CHUNK)
                s = _dot(w, vn[rs]) + bs_ref[:, g:g + 1]
                ya_ref[r0 + c * CHUNK:r0 + (c + 1) * CHUNK, gs] = u[rs, gs] * s
        yield
        ya_ref[rows, :] = _rms(ya_ref[rows, :], ga_ref[...])
        lane = lax.broadcasted_iota(jnp.int32, (n, LANE), 1)
        lane_lo = lane < HEAD_DIM

        def halves(z):
            return z, pltpu.roll(z, HEAD_DIM, 1)

        one_col = jnp.where(lane == HEAD_DIM, 1.0, 0.0)
        zq = _dot(h, w_ref[:, _O_Q:_O_C]) * q_scale
        yield
        for a in range(NSA_HEADS // 2):
            for hd, qh in zip((2 * a, 2 * a + 1), halves(zq[:, a * LANE:(a + 1) * LANE])):
                q_ref[hd, rows, :] = jnp.where(lane_lo, qh, one_col).astype(BF16)
        zc = _dot(h, w_ref[:, _O_C:_O_KV])
        yield
        nr = n // CMP_STRIDE
        urows = slice(r0 // CMP_STRIDE, r0 // CMP_STRIDE + nr)
        lo_u = lax.broadcasted_iota(jnp.int32, (nr, LANE), 1) < HEAD_DIM
        for kv in range(2):
            cmps_ref[kv, rows, :] = zc[:, kv * KV_W:(kv + 1) * KV_W]
            for a in range(CMP_STRIDE // 2):
                za = cmps_ref[kv, pl.ds(r0 + 2 * a, nr, stride=CMP_STRIDE), :]
                zb = cmps_ref[kv, pl.ds(r0 + 2 * a + 1, nr, stride=CMP_STRIDE), :]
                cmp_ref[kv, urows, a * LANE:(a + 1) * LANE] = jnp.where(
                    lo_u, za, pltpu.roll(zb, HEAD_DIM, 1))
                cmp_ref[kv, urows, (CMP_STRIDE // 2 + a) * LANE:(CMP_STRIDE // 2 + a + 1) * LANE] = (
                    jnp.where(lo_u, pltpu.roll(za, HEAD_DIM, 1), zb))
        zkv = _dot(h, w_ref[:, _O_KV:_O_GATE])
        yield
        k_sel, v_sel, k_win, v_win = (halves(zkv[:, j * LANE:(j + 1) * LANE]) for j in range(4))
        pos_blk = ((t - 1) * tm + r0 + lax.broadcasted_iota(jnp.int32, (n, LANE), 0)) // SEL_BLOCK
        blk_onehot = jnp.where(lane == pos_blk, MASK_BF16, 0.0).astype(BF16)

        def v_aug(v, g):
            return jnp.concatenate([jnp.where(lane_lo, v[g], 1.0), jnp.where(lane_lo, 1.0, v[1 - g])],
                                   axis=1).astype(BF16)

        for g in range(NSA_KV):
            ksel_ref[g, rows, 0:LANE] = jnp.where(lane_lo, k_sel[g], 0.0).astype(BF16)
            ksel_ref[g, rows, LANE:2 * LANE] = blk_onehot
            vsel_ref[g, rows, :] = v_aug(v_sel, g)
            kwin_ref[g, rows, :] = jnp.where(lane_lo, k_win[g], 0.0).astype(BF16)
            vwin_ref[g, rows, :] = v_aug(v_win, g)
        sg = jax.nn.sigmoid(_dot(h, wgt_ref[...]))
        hi = sg.astype(BF16)
        gt_ref[rows, 0:LANE] = hi
        gt_ref[rows, LANE:2 * LANE] = (sg - hi.astype(F32)).astype(BF16)

    @pl.when(t > 0)
    def _tile():
        _interleave([chain(k * tm // 2, tm // 2) for k in range(2)])


_O_Q = 2 * GM_WIDTH
_O_C = _O_Q + NSA_WIDTH
_O_KV = _O_C + 2 * KV_W
_O_GATE = _O_KV + 4 * KV_W


def _mixproj(x2d, batch, seq, mix_pre, w_in, ln_g, ln_b, ws, bs, gain_a):
    tm = TOKEN_TILE
    nt = seq // tm
    assert seq % tm == 0 and KV_PAD == tm
    q_scale = HEAD_DIM ** -0.5 * LOG2E
    n_gate = 3 * NSA_HEADS
    w_g = w_in[:, _O_GATE:].reshape(D_MODEL, NSA_KV, NSA_REP, 3).transpose(0, 1, 3, 2)
    w_g = jnp.pad(w_g.reshape(D_MODEL, n_gate), ((0, 0), (0, LANE - n_gate))).astype(BF16)

    def tile(b, t):
        return b * nt + jnp.maximum(t - 1, 0)

    n = batch * seq
    sp = seq + KV_PAD
    out_shape = [
        jax.ShapeDtypeStruct((n, GM_WIDTH), F32),
        jax.ShapeDtypeStruct((batch, NSA_HEADS, seq, LANE), BF16),
        jax.ShapeDtypeStruct((2, n // CMP_STRIDE, CMP_STRIDE * KV_W), F32),
        jax.ShapeDtypeStruct((batch, NSA_KV, sp, 2 * LANE), BF16),
        jax.ShapeDtypeStruct((batch, NSA_KV, sp, 2 * LANE), BF16),
        jax.ShapeDtypeStruct((batch, NSA_KV, sp, LANE), BF16),
        jax.ShapeDtypeStruct((batch, NSA_KV, sp, 2 * LANE), BF16),
        jax.ShapeDtypeStruct((n, 2 * LANE), BF16),
    ]
    padded = lambda w: pl.BlockSpec((None, NSA_KV, tm, w), lambda b, t: (b, 0, t, 0))
    out_specs = [
        pl.BlockSpec((tm, GM_WIDTH), lambda b, t: (tile(b, t), 0)),
        pl.BlockSpec((None, NSA_HEADS, tm, LANE), lambda b, t: (b, 0, jnp.maximum(t - 1, 0), 0)),
        pl.BlockSpec((2, tm // CMP_STRIDE, CMP_STRIDE * KV_W), lambda b, t: (0, tile(b, t), 0)),
        padded(2 * LANE), padded(2 * LANE), padded(LANE), padded(2 * LANE),
        pl.BlockSpec((tm, 2 * LANE), lambda b, t: (tile(b, t), 0)),
    ]
    in_specs = [
        pl.BlockSpec((tm, D_MODEL), lambda b, t: (tile(b, t), 0)),
        _const_spec((1, D_MODEL)),
        pl.BlockSpec(w_in.shape, lambda b, t: (0, 0), pipeline_mode=pl.Buffered(1)), _const_spec(w_g.shape),
        _const_spec((1, GM_WIDTH)), _const_spec((1, GM_WIDTH)),
        _const_spec((GM_GROUPS, CHUNK, CHUNK)), _const_spec((CHUNK, GM_GROUPS)),
        _const_spec((1, GM_WIDTH)),
    ]
    return pl.pallas_call(
        functools.partial(_mixproj_kernel, q_scale=q_scale),
        grid=(batch, nt + 1),
        in_specs=in_specs,
        out_specs=out_specs,
        out_shape=out_shape,
        scratch_shapes=[pltpu.VMEM((2, tm, KV_W), F32),
                        pltpu.VMEM((D_MODEL, _O_GATE), BF16)],
        compiler_params=_params(("arbitrary", "arbitrary")),
        name="mixproj",
    )(x2d, mix_pre.reshape(1, -1), w_in, w_g, ln_g.reshape(1, -1), ln_b.reshape(1, -1),
      ws, bs.T, gain_a.reshape(1, -1))


def _compress_kernel(raw_ref, pe_ref, w1_ref, b1_ref, w2_ref, o_ref):
    is_v = pl.program_id(0) == 1
    nu = raw_ref.shape[0]
    half = CMP_STRIDE * HEAD_DIM
    outs = []
    for g in range(NSA_KV):
        u = raw_ref[:, g * half:(g + 1) * half]
        a = _dot((u + pe_ref[0]).astype(BF16), w1_ref[0:half, :])
        b = _dot((u + pe_ref[1]).astype(BF16), w1_ref[half:2 * half, :])
        hid = jax.nn.gelu(a + pltpu.roll(b, nu - 1, 0) + b1_ref[...])
        outs.append(_dot(hid.astype(BF16), w2_ref[...]))
    lane = lax.broadcasted_iota(jnp.int32, (nu, LANE), 1)
    ci = lax.broadcasted_iota(jnp.int32, (nu, LANE), 0)
    u_c = CMP_BLOCK // CMP_STRIDE
    u_s = SEL_BLOCK // CMP_STRIDE
    overlap = jnp.maximum(jnp.minimum(ci + u_c, u_s * (lane + 1)) - jnp.maximum(ci, u_s * lane), 0)
    extra = jnp.where(is_v, overlap.astype(F32), 0.0)
    ones_hi = jnp.where(jnp.logical_and(is_v, lane >= HEAD_DIM), 1.0, 0.0)
    lane_p = lax.broadcasted_iota(jnp.int32, (CMP_PAD, LANE), 1)
    pad_lo = jnp.where(jnp.logical_and(jnp.logical_not(is_v), lane_p == HEAD_DIM), MASK_BF16, 0.0)
    for g in range(NSA_KV):
        o_ref[g, 0:CMP_PAD, 0:LANE] = pad_lo
        o_ref[g, 0:CMP_PAD, LANE:2 * LANE] = jnp.zeros((CMP_PAD, LANE), F32)
        o_ref[g, CMP_PAD:CMP_PAD + nu, 0:LANE] = outs[g] + ones_hi
        o_ref[g, CMP_PAD:CMP_PAD + nu, LANE:2 * LANE] = extra


def _compress(cmp_raw, batch, seq, pe, w1, b1, w2):
    nu = seq // CMP_STRIDE
    half = CMP_STRIDE * HEAD_DIM
    kd = NSA_KV * half
    raw = cmp_raw.reshape(2, batch, nu, kd)
    w2p = jnp.pad(w2, ((0, 0), (0, 0), (0, LANE - HEAD_DIM)))
    rows = CMP_PAD + nu
    return pl.pallas_call(
        _compress_kernel,
        grid=(2, batch),
        in_specs=[
            pl.BlockSpec((None, None, nu, kd), lambda k, b: (k, b, 0, 0)),
            pl.BlockSpec((None, 2, 1, half), lambda k, b: (k, 0, 0, 0)),
            pl.BlockSpec((None, 2 * half, CMP_HIDDEN), lambda k, b: (k, 0, 0)),
            pl.BlockSpec((None, 1, CMP_HIDDEN), lambda k, b: (k, 0, 0)),
            pl.BlockSpec((None, CMP_HIDDEN, LANE), lambda k, b: (k, 0, 0)),
        ],
        out_specs=pl.BlockSpec((None, None, NSA_KV, rows, 2 * LANE), lambda k, b: (k, b, 0, 0, 0)),
        out_shape=jax.ShapeDtypeStruct((2, batch, NSA_KV, rows, 2 * LANE), F32),
        compiler_params=_params(("arbitrary", "arbitrary")),
        name="compress",
    )(raw, pe.reshape(2, 2, 1, half), w1.astype(BF16), b1.reshape(2, 1, CMP_HIDDEN), w2p.astype(BF16))


def _t5_bucket(dist):
    n = jnp.maximum(dist, 0)
    max_exact = N_BUCKETS // 2
    nf = jnp.maximum(n, max_exact).astype(F32)
    large = max_exact + jnp.floor(jnp.log(nf * (1.0 / max_exact)) / math.log(MAX_DISTANCE / max_exact)
                                  * (N_BUCKETS - max_exact)).astype(jnp.int32)
    large = jnp.minimum(large, N_BUCKETS - 1)
    return jnp.where(n < max_exact, n, large)


def _tables_kernel(rb_ref, tb_ref, tc_ref, wm_ref):
    def fill(out_ref, dist):
        bk = _t5_bucket(dist)
        for h in range(NSA_HEADS):
            far = rb_ref[N_BUCKETS - 1, h]
            val = jnp.zeros(dist.shape, F32)
            for b in range(N_BUCKETS - 1):
                val = jnp.where(bk == b, (rb_ref[b, h] - far) * LOG2E, val)
            out_ref[h] = jnp.where(dist >= 0, val, MASK_F32)

    tq = lax.broadcasted_iota(jnp.int32, (Q_BLOCK, 2 * Q_BLOCK), 0)
    j = lax.broadcasted_iota(jnp.int32, (Q_BLOCK, 2 * Q_BLOCK), 1)
    fill(tb_ref, tq + Q_BLOCK - j)
    tq = lax.broadcasted_iota(jnp.int32, (Q_BLOCK, LANE), 0)
    c = lax.broadcasted_iota(jnp.int32, (Q_BLOCK, LANE), 1)
    cmp_first = Q_BLOCK // CMP_STRIDE - LANE
    fill(tc_ref, tq - CMP_STRIDE * (cmp_first + c) - (CMP_BLOCK - 1))
    wm_ref[...] = jnp.where(c > tq, 0.0, MASK_F32)


def _bias_tables(rel_bias):
    return pl.pallas_call(
        _tables_kernel,
        in_specs=[pl.BlockSpec(memory_space=pltpu.SMEM)],
        out_shape=[jax.ShapeDtypeStruct((NSA_HEADS, Q_BLOCK, 2 * Q_BLOCK), F32),
                   jax.ShapeDtypeStruct((NSA_HEADS, Q_BLOCK, LANE), F32),
                   jax.ShapeDtypeStruct((Q_BLOCK, LANE), F32)],
        name="bias_tables",
    )(rel_bias)


_N_GROUP_SCRATCH = 8


def _oddeven_merge_sort(n):
    def merge(lo, hi, r):
        step = 2 * r
        if step < hi - lo:
            yield from merge(lo, hi, step)
            yield from merge(lo + r, hi, step)
            yield from ((k, k + r) for k in range(lo + r, hi - r, step))
        else:
            yield (lo, lo + r)

    def sort(lo, hi):
        if hi > lo:
            mid = lo + (hi - lo) // 2
            yield from sort(lo, mid)
            yield from sort(mid + 1, hi)
            yield from merge(lo, hi, 1)

    return tuple(sort(0, n - 1))


def _bitonic_merge(n):
    pairs, k = [], n // 2
    while k >= 1:
        pairs += [(a, a + k) for a in range(n) if not a & k]
        k //= 2
    return tuple(pairs)


_SORT16 = _oddeven_merge_sort(N_SELECT)
_BITONIC16 = _bitonic_merge(N_SELECT)


def _nsa_kernel(q_ref, gt_ref, kc_ref, vc_ref, ks_ref, vs_ref, kw_ref, vw_ref,
                tb_ref, tc_ref, wm_ref, gx_ref, o_ref, *scratch, n_sel, select_all):
    i = pl.program_id(1)
    rows = NSA_REP * Q_BLOCK
    per_group = [scratch[g * _N_GROUP_SCRATCH:(g + 1) * _N_GROUP_SCRATCH] for g in range(NSA_KV)]
    qsel_ref = [s[0] for s in per_group]
    s_ref = [(s[1], s[2]) for s in per_group]
    m_ref = [(s[3], s[4]) for s in per_group]
    al_ref = [(s[5], s[6]) for s in per_group]
    acc_ref = [s[7] for s in per_group]

    zero_tb = jnp.zeros((rows, KV_TILE - 2 * Q_BLOCK), F32)
    cmp_per_q = Q_BLOCK // CMP_STRIDE
    c0 = pl.multiple_of(cmp_per_q * (i + 1) + CMP_PAD - CMP_SPAN, cmp_per_q)
    t0 = pl.multiple_of(Q_BLOCK * (i + 1) + KV_PAD - KV_TILE, Q_BLOCK)
    w1 = pl.multiple_of(t0 - Q_BLOCK, Q_BLOCK)
    n_tiles = (i + 1 + KV_TILE // Q_BLOCK - 1) // (KV_TILE // Q_BLOCK)
    lane = lax.broadcasted_iota(jnp.int32, (Q_BLOCK, LANE), 1)

    def bcast(col):
        return jnp.broadcast_to(col, (rows, LANE))

    def lanes(x, n):
        return jnp.concatenate([x] * n, axis=1)

    def head_out(acc, r):
        rs = slice(r * Q_BLOCK, (r + 1) * Q_BLOCK)
        a, b = acc[rs, 0:LANE], acc[rs, LANE:2 * LANE]
        return a * (1.0 / b) if r % 2 == 0 else b * (1.0 / a)

    def heads_of(g):
        return slice(g * NSA_REP, (g + 1) * NSA_REP)

    def q_of(g):
        return q_ref[heads_of(g)].reshape(rows, LANE)

    def tile0_bias(g):
        return jnp.concatenate([zero_tb, tb_ref[heads_of(g)].reshape(rows, 2 * Q_BLOCK)], axis=1)

    fronts = [None] * NSA_KV

    def front(g):
        hs = heads_of(g)
        qa = q_of(g)
        s = _dot_nt(qa, kc_ref[g, pl.ds(c0, CMP_SPAN), :].astype(BF16))
        s = s + jnp.concatenate([jnp.zeros((rows, CMP_SPAN - LANE), F32),
                                 tc_ref[hs].reshape(rows, LANE)], axis=1)
        yield
        m = jnp.maximum(jnp.max(s, axis=1, keepdims=True), MAX_CLAMP)
        p = _softmax_exp(s - m).astype(BF16)
        yield
        acc = _dot(p, vc_ref[g, pl.ds(c0, CMP_SPAN), :].astype(BF16))
        num_den = acc[:, 0:LANE]
        lane_r = lax.broadcasted_iota(jnp.int32, (rows, LANE), 1)
        den = jnp.where(lane_r < HEAD_DIM, pltpu.roll(num_den, HEAD_DIM, 1), num_den)
        inv = 1.0 / jnp.maximum(den, 1e-30)
        oc = num_den * inv
        impu = acc[:, LANE:2 * LANE] * inv
        imp = impu[0:Q_BLOCK]
        for r in range(1, NSA_REP):
            imp = imp + impu[r * Q_BLOCK:(r + 1) * Q_BLOCK]
        yield
        s0 = _dot_nt(qa, kw_ref[g, pl.ds(t0, KV_TILE), :]) + tile0_bias(g)
        s1 = _dot_nt(qa, kw_ref[g, pl.ds(w1, Q_BLOCK), :]) + jnp.tile(wm_ref[...], (NSA_REP, 1))
        yield
        m = jnp.maximum(jnp.max(s0, axis=1, keepdims=True), jnp.max(s1, axis=1, keepdims=True))
        p0 = _softmax_exp(s0 - m).astype(BF16)
        p1 = _softmax_exp(s1 - m).astype(BF16)
        yield
        accw = (_dot(p0, vw_ref[g, pl.ds(t0, KV_TILE), :]) + _dot(p1, vw_ref[g, pl.ds(w1, Q_BLOCK), :]))
        gx = _dot(gt_ref[...], gx_ref[g])

        def gate(branch, r):
            c = branch * NSA_REP + r
            return gx[:, c * LANE:(c + 1) * LANE]

        partial = []
        for r in range(NSA_REP):
            rs = slice(r * Q_BLOCK, (r + 1) * Q_BLOCK)
            oc_r = oc[rs] if r % 2 == 0 else pltpu.roll(oc[rs], HEAD_DIM, 1)
            partial.append(gate(0, r) * oc_r + gate(2, r) * head_out(accw, r))
        fronts[g] = (partial, [gate(1, r) for r in range(NSA_REP)])
        yield
        notsel_q = not_selected(imp).astype(BF16)
        qsel_ref[g][:, 0:LANE] = qa
        for r in range(NSA_REP):
            qsel_ref[g][r * Q_BLOCK:(r + 1) * Q_BLOCK, LANE:2 * LANE] = notsel_q
        yield
        s = _dot_nt(qsel_ref[g][...], ks_ref[g, pl.ds(t0, KV_TILE), :]) + tile0_bias(g)
        yield
        s_ref[g][0][...] = s
        m_ref[g][0][...] = bcast(jnp.max(s, axis=1, keepdims=True))
        al_ref[g][0][...] = jnp.ones((rows, LANE), F32)
        acc_ref[g][...] = jnp.zeros((rows, 2 * LANE), F32)

    def not_selected(imp):
        jb = lax.broadcasted_iota(jnp.int32, (LANE, Q_BLOCK), 0)
        t = i * Q_BLOCK + lax.broadcasted_iota(jnp.int32, (LANE, Q_BLOCK), 1)
        cur = t // SEL_BLOCK
        forced = (jb == 0) | (jb == cur) | (jb == cur - 1)
        score = jnp.where(forced, FORCE_SCORE, jnp.where(jb * SEL_BLOCK <= t, imp.T, -1.0))
        if select_all:
            return jnp.zeros((Q_BLOCK, LANE), F32)
        x = [score[SUBLANE * v:SUBLANE * (v + 1), :] for v in range(LANE // SUBLANE)]
        y = list(x)
        for a, b in _SORT16:
            y[a], y[b] = jnp.maximum(y[a], y[b]), jnp.minimum(y[a], y[b])
        for shift in (4, 2, 1):
            z = [jnp.maximum(y[k], pltpu.roll(y[n_sel - 1 - k], shift, 0)) for k in range(n_sel)]
            if shift > 1:
                for a, b in _BITONIC16:
                    z[a], z[b] = jnp.maximum(z[a], z[b]), jnp.minimum(z[a], z[b])
            y = z
        while len(y) > 1:
            y = [jnp.minimum(y[2 * k], y[2 * k + 1]) for k in range(len(y) // 2)]
        tau = y[0]
        gt = jnp.concatenate([jnp.where(xv > tau, 1.0, 0.0) for xv in x], axis=0)
        eq = jnp.concatenate([jnp.where(xv == tau, 1.0, 0.0) for xv in x], axis=0)
        kb = lax.broadcasted_iota(jnp.int32, (LANE, LANE), 1)
        n_above = _dot(jnp.ones((LANE, LANE), BF16), gt.astype(BF16))
        n_tied_before = _dot(jnp.where(kb < jb, 1.0, 0.0).astype(BF16), eq.astype(BF16))
        keep = (gt > 0.0) | ((eq > 0.0) & (n_above + n_tied_before < n_sel))
        return jnp.where(keep, 0.0, 1.0).T

    def qk_stage(g, k, prev, slot):
        st = pl.multiple_of(t0 - KV_TILE * k, Q_BLOCK)
        s_n = _dot_nt(qsel_ref[g][...], ks_ref[g, pl.ds(st, KV_TILE), :])
        s_ref[g][slot][...] = s_n
        m_c = m_ref[g][prev][...]
        m_n = jnp.maximum(m_c, bcast(jnp.max(s_n, axis=1, keepdims=True)))
        m_ref[g][slot][...] = m_n
        al_ref[g][slot][...] = _softmax_exp(m_c - m_n)

    def pv_stage(g, k, slot):
        st = pl.multiple_of(t0 - KV_TILE * k, Q_BLOCK)
        p = _softmax_exp(s_ref[g][slot][...] - lanes(m_ref[g][slot][...], KV_TILE // LANE))
        acc_ref[g][...] = (lanes(al_ref[g][slot][...], 2) * acc_ref[g][...]
                           + _dot(p.astype(BF16), vs_ref[g, pl.ds(st, KV_TILE), :]))

    groups = range(NSA_KV)
    _interleave([front(g) for g in groups])

    def tile_pair(j):
        for g in groups:
            qk_stage(g, 2 * j + 1, 0, 1)
        for g in groups:
            pv_stage(g, 2 * j, 0)
        for g in groups:
            qk_stage(g, 2 * j + 2, 1, 0)
        for g in groups:
            pv_stage(g, 2 * j + 1, 1)

    def two_pairs(j, carry):
        tile_pair(2 * j)
        tile_pair(2 * j + 1)
        return carry

    def one_pair(j, carry):
        tile_pair(j)
        return carry

    n_pairs = (n_tiles - 1) // 2
    lax.fori_loop(0, n_pairs // 2, two_pairs, 0)
    lax.fori_loop(2 * (n_pairs // 2), n_pairs, one_pair, 0)
    odd_left = n_tiles - 1 - 2 * n_pairs

    @pl.when(odd_left == 1)
    def _two_left():
        for g in groups:
            pv_stage(g, n_tiles - 2, 0)
        for g in groups:
            qk_stage(g, n_tiles - 1, 0, 0)

    for g in groups:
        pv_stage(g, n_tiles - 1, 0)
        partial, sel_gate = fronts[g]
        acc = acc_ref[g][...]
        heads = [partial[r] + sel_gate[r] * head_out(acc, r) for r in range(NSA_REP)]
        for pr in range(NSA_REP // 2):
            col = (g * NSA_REP // 2 + pr) * LANE
            o_ref[:, col:col + LANE] = jnp.where(lane < HEAD_DIM, heads[2 * pr], heads[2 * pr + 1])


def _nsa(q_aug, gates, cmp_aug, ksel, vsel, kwin, vwin, tb, tc, wm, batch, seq):
    nq = seq // Q_BLOCK
    sp = seq + KV_PAD
    crow = cmp_aug.shape[3]
    n_sel = min(N_SELECT, seq // SEL_BLOCK)
    assert seq // SEL_BLOCK <= LANE and KV_PAD >= KV_TILE and KV_PAD >= WINDOW
    assert seq // CMP_STRIDE <= CMP_SPAN <= CMP_PAD and crow == CMP_PAD + seq // CMP_STRIDE
    assert seq // SEL_BLOCK <= N_SELECT or n_sel == LANE // 8
    n_gate = 3 * NSA_REP
    col = jnp.arange(n_gate * LANE) // LANE
    row = jnp.arange(2 * LANE) % LANE
    gate_spread = jnp.stack([(row[:, None] == g * n_gate + col[None, :]) for g in range(NSA_KV)]).astype(BF16)
    once = pl.Buffered(1)
    kv_spec = lambda w: pl.BlockSpec((None, NSA_KV, sp, w), lambda b, i: (b, 0, 0, 0), pipeline_mode=once)
    cmp_spec = lambda k, w: pl.BlockSpec((None, None, NSA_KV, crow, w), lambda b, i: (k, b, 0, 0, 0),
                                         pipeline_mode=once)
    in_specs = [
        pl.BlockSpec((None, NSA_HEADS, Q_BLOCK, LANE), lambda b, i: (b, 0, i, 0)),
        pl.BlockSpec((Q_BLOCK, 2 * LANE), lambda b, i: (b * nq + i, 0)),
        cmp_spec(0, LANE), cmp_spec(1, 2 * LANE),
        kv_spec(2 * LANE), kv_spec(2 * LANE), kv_spec(LANE), kv_spec(2 * LANE),
        _const_spec((NSA_HEADS, Q_BLOCK, 2 * Q_BLOCK)),
        _const_spec((NSA_HEADS, Q_BLOCK, LANE)),
        _const_spec((Q_BLOCK, LANE)),
        _const_spec((NSA_KV, 2 * LANE, n_gate * LANE)),
    ]
    rows = NSA_REP * Q_BLOCK
    group_scratch = [pltpu.VMEM((rows, 2 * LANE), BF16),
                     pltpu.VMEM((rows, KV_TILE), F32),
                     pltpu.VMEM((rows, KV_TILE), F32),
                     pltpu.VMEM((rows, LANE), F32),
                     pltpu.VMEM((rows, LANE), F32),
                     pltpu.VMEM((rows, LANE), F32),
                     pltpu.VMEM((rows, LANE), F32),
                     pltpu.VMEM((rows, 2 * LANE), F32)]
    assert len(group_scratch) == _N_GROUP_SCRATCH
    return pl.pallas_call(
        functools.partial(_nsa_kernel, n_sel=n_sel, select_all=seq // SEL_BLOCK <= N_SELECT),
        grid=(batch, nq),
        in_specs=in_specs,
        out_specs=pl.BlockSpec((Q_BLOCK, NSA_WIDTH), lambda b, i: (b * nq + i, 0)),
        out_shape=jax.ShapeDtypeStruct((batch * seq, NSA_WIDTH), F32),
        scratch_shapes=group_scratch * NSA_KV,
        compiler_params=_params(("arbitrary", "arbitrary")),
        name="nsa",
    )(q_aug, gates, cmp_aug, cmp_aug, ksel, vsel, kwin, vwin, tb, tc, wm, gate_spread)


def _memkv_kernel(mem_ref, g_ref, wk_ref, wv_ref, k_ref, v_ref):
    h = _rms(mem_ref[...], g_ref[...]).astype(BF16)
    k_ref[...] = _dot(h, wk_ref[...]).T.astype(BF16)
    v_ref[...] = _dot(h, wv_ref[...]).astype(BF16)


def _mem_kv(mem, g_mem, wkv):
    batch, m, _ = mem.shape
    blk = pl.BlockSpec((None, m, D_MODEL), lambda b: (b, 0, 0))
    blk_t = pl.BlockSpec((None, D_MODEL, m), lambda b: (b, 0, 0))
    return pl.pallas_call(
        _memkv_kernel,
        grid=(batch,),
        in_specs=[blk, _const_spec((1, D_MODEL)), _const_spec((D_MODEL, D_MODEL)),
                  _const_spec((D_MODEL, D_MODEL))],
        out_specs=[blk_t, blk],
        out_shape=[jax.ShapeDtypeStruct((batch, D_MODEL, m), BF16),
                   jax.ShapeDtypeStruct((batch, m, D_MODEL), BF16)],
        compiler_params=_params(("arbitrary",)),
        name="mem_kv",
    )(mem, g_mem.reshape(1, -1), wkv[:, :D_MODEL].astype(BF16), wkv[:, D_MODEL:].astype(BF16))


def _mixout_kernel(x_ref, ya_ref, yb_ref, gb_ref, wo_ref, post_ref, xpre_ref, wq_ref,
                   km_ref, vm_ref, wxo_ref, xpost_ref, o_ref):
    tm = x_ref.shape[0]

    def chain(rows):
        yb = _rms(yb_ref[rows, :], gb_ref[...]).astype(BF16)
        ya = ya_ref[rows, :].astype(BF16)
        yield
        y = _dot(ya, wo_ref[0:GM_WIDTH, :]) + _dot(yb, wo_ref[GM_WIDTH:, :])
        yield
        x1 = x_ref[rows, :] + _rms(y, post_ref[...])
        h = _rms(x1, xpre_ref[...]).astype(BF16)
        yield
        q = _dot(h, wq_ref[...]).astype(BF16)
        yield
        outs = []
        for hd in range(XA_HEADS):
            hs = slice(hd * XA_HEAD_DIM, (hd + 1) * XA_HEAD_DIM)
            s = _dot(q[:, hs], km_ref[hs, :])
            yield
            e = _softmax_exp(s - jnp.max(s, axis=1, keepdims=True))
            den = jnp.sum(e, axis=1, keepdims=True)
            e = e.astype(BF16)
            yield
            outs.append((_dot(e, vm_ref[:, hs]) / den).astype(BF16))
        o = _dot(jnp.concatenate(outs, axis=1), wxo_ref[...])
        yield
        o_ref[rows, :] = x1 + _rms(o, xpost_ref[...])

    _interleave([chain(slice(k * tm // 2, (k + 1) * tm // 2)) for k in range(2)])


def _mix_out(x2d, ya, yb, batch, seq, gain_b, w_out, mix_post, xa_pre, wq, kmem, vmem, wo, xa_post):
    tm = 2 * TOKEN_TILE if seq % (2 * TOKEN_TILE) == 0 else TOKEN_TILE
    nt = seq // tm
    row = lambda w: pl.BlockSpec((tm, w), lambda b, t: (b * nt + t, 0))
    kmem_spec = pl.BlockSpec((None,) + kmem.shape[1:], lambda b, t: (b, 0, 0))
    vmem_spec = pl.BlockSpec((None,) + vmem.shape[1:], lambda b, t: (b, 0, 0))
    sq = _const_spec((D_MODEL, D_MODEL))
    vec = _const_spec((1, D_MODEL))
    return pl.pallas_call(
        _mixout_kernel,
        grid=(batch, nt),
        in_specs=[row(D_MODEL), row(GM_WIDTH), row(NSA_WIDTH), _const_spec((1, NSA_WIDTH)), sq, vec,
                  vec, sq, kmem_spec, vmem_spec, sq, vec],
        out_specs=row(D_MODEL),
        out_shape=jax.ShapeDtypeStruct(x2d.shape, F32),
        compiler_params=_params(("arbitrary", "arbitrary")),
        name="mix_out",
    )(x2d, ya, yb, gain_b.reshape(1, -1), w_out.astype(BF16), mix_post.reshape(1, -1),
      xa_pre.reshape(1, -1), (wq * (XA_HEAD_DIM ** -0.5 * LOG2E)).astype(BF16), kmem, vmem,
      wo.astype(BF16), xa_post.reshape(1, -1))


def kernel(x, mem, ffn1_pre, ffn1_post, ffn1_wg, ffn1_wu, ffn1_wd, mix_pre, mix_post, w_in, gm_ln_g, gm_ln_b, gm_ws, gm_bs, ck_pe, ck_w1, ck_b1, ck_w2, cv_pe, cv_w1, cv_b1, cv_w2, rel_bias, out_gain_a, out_gain_b, w_out, xa_pre, xa_post, mem_norm, xa_wq, xa_wkv, xa_wo, ffn2_pre, ffn2_post, ffn2_wg, ffn2_wu, ffn2_wd):
    batch, seq, _ = x.shape
    depth = ffn1_pre.shape[0]
    x2d = x.reshape(batch * seq, D_MODEL)
    tb, tc, wm = _bias_tables(rel_bias)
    for l in range(depth):
        x2d = _ffn(x2d, ffn1_pre[l], ffn1_wg[l], ffn1_wu[l], ffn1_wd[l], ffn1_post[l])
        ya, q_aug, cmp_raw, ksel, vsel, kwin, vwin, gates = _mixproj(
            x2d, batch, seq, mix_pre[l], w_in[l], gm_ln_g[l], gm_ln_b[l], gm_ws[l], gm_bs[l],
            out_gain_a[l])
        cmp_aug = _compress(cmp_raw, batch, seq,
                            jnp.stack([ck_pe[l], cv_pe[l]]), jnp.stack([ck_w1[l], cv_w1[l]]),
                            jnp.stack([ck_b1[l], cv_b1[l]]), jnp.stack([ck_w2[l], cv_w2[l]]))
        yb = _nsa(q_aug, gates, cmp_aug, ksel, vsel, kwin, vwin, tb, tc, wm, batch, seq)
        kmem, vmem = _mem_kv(mem, mem_norm[l], xa_wkv[l])
        x2d = _mix_out(x2d, ya, yb, batch, seq, out_gain_b[l], w_out[l], mix_post[l], xa_pre[l],
                       xa_wq[l], kmem, vmem, xa_wo[l], xa_post[l])
        x2d = _ffn(x2d, ffn2_pre[l], ffn2_wg[l], ffn2_wu[l], ffn2_wd[l], ffn2_post[l])
    return x2d.reshape(batch, seq, D_MODEL)
```
